```python
import jax
import jax.numpy as jnp
from jax import lax
import numpy as np

D_MODEL = 1024
BATCH = 4
SEQ = 8192
DEPTH = 2

HEAD_DIM = 64
ROT_DIM = HEAD_DIM // 4
ROPE_THETA = 500000.0
BLOCK = 128
RMS_EPS = 1e-6
LN_EPS = 1e-5

A_HEADS = (D_MODEL // 2) // HEAD_DIM
A_KV_HEADS = 2
A_WINDOW = 128
CONV_CH = D_MODEL // 2
CONV_WIDTH = 31
C_HEADS = (D_MODEL // 2) // HEAD_DIM
DILATED_PAIRS = ((128, 1), (512, 4), (2048, 16))
D_CH = D_MODEL // 2
D_GROUPS = D_CH // HEAD_DIM
CHUNK = 128
D_FF = ((8 * D_MODEL // 3 + 255) // 256) * 256

A_Q = A_HEADS * HEAD_DIM
A_KV = A_KV_HEADS * HEAD_DIM
EVEN_IN = A_Q + 2 * A_KV + 2 * CONV_CH
EVEN_OUT = A_Q + CONV_CH
C_W = C_HEADS * HEAD_DIM
ODD_IN = 3 * C_W + 2 * D_CH
ODD_OUT = C_W + D_CH
N_EVEN = (DEPTH + 1) // 2
N_ODD = DEPTH // 2

kernel_name = 'hybrid_swa_sink_conformer_dilated_gmlp'


def rms_norm(x, g):
    xf = x.astype(jnp.float32)
    y = xf * lax.rsqrt(jnp.mean(xf * xf, axis=-1, keepdims=True) + RMS_EPS)
    return (y * g.astype(jnp.float32)).astype(x.dtype)


def layer_norm(x, g, b):
    xf = x.astype(jnp.float32)
    mu = jnp.mean(xf, axis=-1, keepdims=True)
    xc = xf - mu
    var = jnp.mean(xc * xc, axis=-1, keepdims=True)
    y = xc * lax.rsqrt(var + LN_EPS) * g.astype(jnp.float32) + b.astype(jnp.float32)
    return y.astype(x.dtype)


def rotary(x, pos):
    half = ROT_DIM // 2
    inv_freq = ROPE_THETA ** (-jnp.arange(half, dtype=jnp.float32) * (2.0 / ROT_DIM))
    ang = pos.astype(jnp.float32)[:, None] * inv_freq[None, :]
    cos = jnp.cos(ang)[None, :, None, :]
    sin = jnp.sin(ang)[None, :, None, :]
    xr = x[..., :ROT_DIM].astype(jnp.float32)
    x1, x2 = xr[..., :half], xr[..., half:]
    rot = jnp.concatenate([x1 * cos - x2 * sin, x2 * cos + x1 * sin], axis=-1).astype(x.dtype)
    return jnp.concatenate([rot, x[..., ROT_DIM:]], axis=-1)


def band_attention(q, k, v, max_dist, sink=None):
    B, S, Hq, hd = q.shape
    Hkv = k.shape[2]
    G = Hq // Hkv
    n = S // BLOCK
    qb = q.reshape(B, n, BLOCK, Hkv, G, hd)
    kb = k.reshape(B, n, BLOCK, Hkv, hd)
    vb = v.reshape(B, n, BLOCK, Hkv, hd)
    prev = lambda t: jnp.pad(t, ((0, 0), (1, 0), (0, 0), (0, 0), (0, 0)))[:, :-1]
    kk = jnp.concatenate([prev(kb), kb], axis=2)
    vv = jnp.concatenate([prev(vb), vb], axis=2)
    s = jnp.einsum('bnqhgd,bnjhd->bnhgqj', qb, kk,
                   preferred_element_type=jnp.float32) * (hd ** -0.5)
    qi = jnp.arange(BLOCK)[:, None]
    kj = jnp.arange(2 * BLOCK)[None, :]
    dist = qi + BLOCK - kj
    key_pos = jnp.arange(n)[:, None, None] * BLOCK + kj[None] - BLOCK
    valid = (dist >= 0)[None] & (dist <= max_dist)[None] & (key_pos >= 0)
    s = jnp.where(valid[None, :, None, None], s, -jnp.inf)
    m = jnp.max(s, axis=-1)
    if sink is not None:
        sink_b = sink.astype(jnp.float32).reshape(Hkv, G)[None, None, :, :, None]
        m = jnp.maximum(m, sink_b)
    p = jnp.exp(s - m[..., None])
    l = jnp.sum(p, axis=-1)
    if sink is not None:
        l = l + jnp.exp(sink_b - m)
    o = jnp.einsum('bnhgqj,bnjhd->bnqhgd', p.astype(v.dtype), vv,
                   preferred_element_type=jnp.float32)
    o = o / jnp.transpose(l, (0, 1, 4, 2, 3))[..., None]
    lse = jnp.transpose(m + jnp.log(l), (0, 1, 4, 2, 3)).reshape(B, S, Hq)
    return o.reshape(B, S, Hq, hd).astype(q.dtype), lse


def dilated_window_attention(q, k, v, window, dilation):
    B, S, H, hd = q.shape
    span = dilation * BLOCK
    s_pad = -(-S // span) * span
    sub = s_pad // dilation

    def fold(t):
        t = jnp.pad(t, ((0, 0), (0, s_pad - S), (0, 0), (0, 0)))
        t = t.reshape(B, sub, dilation, t.shape[2], hd)
        return jnp.transpose(t, (0, 2, 1, 3, 4)).reshape(B * dilation, sub, t.shape[3], hd)

    o, lse = band_attention(fold(q), fold(k), fold(v), window // dilation)
    o = jnp.transpose(o.reshape(B, dilation, sub, H, hd), (0, 2, 1, 3, 4)).reshape(B, s_pad, H, hd)
    lse = jnp.transpose(lse.reshape(B, dilation, sub, H), (0, 2, 1, 3)).reshape(B, s_pad, H)
    return o[:, :S], lse[:, :S]


def causal_depthwise_conv(x, w, b):
    C = x.shape[-1]
    y = lax.conv_general_dilated(
        x, w[:, None, :].astype(x.dtype), window_strides=(1,),
        padding=[(CONV_WIDTH - 1, 0)], dimension_numbers=('NWC', 'WIO', 'NWC'),
        feature_group_count=C)
    return y + b.astype(x.dtype)


def even_mixer(h, w_in, sinks, conv_w, conv_b, ln_g, ln_b, w_out, pos):
    B, S, _ = h.shape
    proj = jnp.einsum('bsd,de->bse', h, w_in)
    q, k, v, glu = jnp.split(proj, [A_Q, A_Q + A_KV, A_Q + 2 * A_KV], axis=-1)
    q = rotary(q.reshape(B, S, A_HEADS, HEAD_DIM), pos)
    k = rotary(k.reshape(B, S, A_KV_HEADS, HEAD_DIM), pos)
    v = v.reshape(B, S, A_KV_HEADS, HEAD_DIM)
    a, _ = band_attention(q, k, v, A_WINDOW - 1, sinks)
    a = a.reshape(B, S, A_Q)
    g_a, g_b = jnp.split(glu, 2, axis=-1)
    c = g_a * jax.nn.sigmoid(g_b)
    c = causal_depthwise_conv(c, conv_w, conv_b)
    c = jax.nn.silu(layer_norm(c, ln_g, ln_b))
    return jnp.einsum('bse,ed->bsd', jnp.concatenate([a, c], axis=-1), w_out)


def odd_mixer(h, w_in, sgu_ln_g, sgu_ln_b, spatial_w, spatial_b, w_out, pos):
    B, S, _ = h.shape
    proj = jnp.einsum('bsd,de->bse', h, w_in)
    q, k, v, z = jnp.split(proj, [C_W, 2 * C_W, 3 * C_W], axis=-1)
    q = rotary(q.reshape(B, S, C_HEADS, HEAD_DIM), pos)
    k = rotary(k.reshape(B, S, C_HEADS, HEAD_DIM), pos)
    v = v.reshape(B, S, C_HEADS, HEAD_DIM)
    outs, lses = [], []
    for window, dilation in DILATED_PAIRS:
        o_r, lse_r = dilated_window_attention(q, k, v, window, dilation)
        outs.append(o_r)
        lses.append(lse_r)
    alpha = jax.nn.softmax(jnp.stack(lses, axis=0), axis=0)
    c_out = jnp.einsum('rbsh,rbshd->bshd', alpha, jnp.stack(outs, axis=0).astype(jnp.float32))
    c_out = c_out.astype(h.dtype).reshape(B, S, C_W)
    z = jax.nn.gelu(z)
    u, g = jnp.split(z, 2, axis=-1)
    g = layer_norm(g, sgu_ln_g, sgu_ln_b).reshape(B, S // CHUNK, CHUNK, D_GROUPS, HEAD_DIM)
    causal = jnp.tril(jnp.ones((CHUNK, CHUNK), dtype=bool))
    w_s = jnp.where(causal[None], spatial_w, 0).astype(g.dtype)
    mixed = jnp.einsum('gts,bcsgd->bctgd', w_s, g) + spatial_b.T.astype(g.dtype)[None, None, :, :, None]
    d_out = u * mixed.reshape(B, S, D_CH)
    return jnp.einsum('bse,ed->bsd', jnp.concatenate([c_out, d_out], axis=-1), w_out)


def swiglu(h, w_gate, w_up, w_down):
    gate = jnp.einsum('bsd,df->bsf', h, w_gate)
    up = jnp.einsum('bsd,df->bsf', h, w_up)
    return jnp.einsum('bsf,fd->bsd', jax.nn.silu(gate) * up, w_down)


def setup_inputs(seed: int = 0) -> dict:
    key = jax.random.key(seed)
    ks = jax.random.split(key, 21)
    f32 = jnp.float32

    def nrm(k, shape, scale):
        return jax.random.normal(k, shape, f32) * scale

    return {
        'x': nrm(ks[0], (BATCH, SEQ, D_MODEL), 1.0),
        'ev_norm_g': 1.0 + nrm(ks[1], (N_EVEN, D_MODEL), 0.02),
        'ev_w_in': nrm(ks[2], (N_EVEN, D_MODEL, EVEN_IN), D_MODEL ** -0.5),
        'ev_sinks': nrm(ks[3], (N_EVEN, A_HEADS), 0.5),
        'ev_conv_w': nrm(ks[4], (N_EVEN, CONV_WIDTH, CONV_CH), CONV_WIDTH ** -0.5),
        'ev_conv_b': nrm(ks[5], (N_EVEN, CONV_CH), 0.02),
        'ev_conv_ln_g': 1.0 + nrm(ks[6], (N_EVEN, CONV_CH), 0.02),
        'ev_conv_ln_b': nrm(ks[7], (N_EVEN, CONV_CH), 0.02),
        'ev_w_out': nrm(ks[8], (N_EVEN, EVEN_OUT, D_MODEL), EVEN_OUT ** -0.5),
        'od_norm_g': 1.0 + nrm(ks[9], (N_ODD, D_MODEL), 0.02),
        'od_w_in': nrm(ks[10], (N_ODD, D_MODEL, ODD_IN), D_MODEL ** -0.5),
        'od_sgu_ln_g': 1.0 + nrm(ks[11], (N_ODD, D_CH), 0.02),
        'od_sgu_ln_b': nrm(ks[12], (N_ODD, D_CH), 0.02),
        'od_spatial_w': nrm(ks[13], (N_ODD, D_GROUPS, CHUNK, CHUNK), CHUNK ** -0.5),
        'od_spatial_b': 1.0 + nrm(ks[14], (N_ODD, D_GROUPS, CHUNK), 0.02),
        'od_w_out': nrm(ks[15], (N_ODD, ODD_OUT, D_MODEL), ODD_OUT ** -0.5),
        'ffn_norm_g': 1.0 + nrm(ks[16], (DEPTH, D_MODEL), 0.02),
        'ffn_w_gate': nrm(ks[17], (DEPTH, D_MODEL, D_FF), D_MODEL ** -0.5),
        'ffn_w_up': nrm(ks[18], (DEPTH, D_MODEL, D_FF), D_MODEL ** -0.5),
        'ffn_w_down': nrm(ks[19], (DEPTH, D_FF, D_MODEL), D_FF ** -0.5),
        'final_norm_g': 1.0 + nrm(ks[20], (D_MODEL,), 0.02),
    }


def reference(x, ev_norm_g, ev_w_in, ev_sinks, ev_conv_w, ev_conv_b, ev_conv_ln_g,
              ev_conv_ln_b, ev_w_out, od_norm_g, od_w_in, od_sgu_ln_g, od_sgu_ln_b,
              od_spatial_w, od_spatial_b, od_w_out, ffn_norm_g, ffn_w_gate, ffn_w_up,
              ffn_w_down, final_norm_g):
    pos = jnp.arange(x.shape[1], dtype=jnp.int32)
    h = x
    for layer in range(DEPTH):
        i = layer // 2
        if layer % 2 == 0:
            h = h + even_mixer(rms_norm(h, ev_norm_g[i]), ev_w_in[i], ev_sinks[i],
                               ev_conv_w[i], ev_conv_b[i], ev_conv_ln_g[i],
                               ev_conv_ln_b[i], ev_w_out[i], pos)
        else:
            h = h + odd_mixer(rms_norm(h, od_norm_g[i]), od_w_in[i], od_sgu_ln_g[i],
                              od_sgu_ln_b[i], od_spatial_w[i], od_spatial_b[i],
                              od_w_out[i], pos)
        h = h + swiglu(rms_norm(h, ffn_norm_g[layer]), ffn_w_gate[layer],
                       ffn_w_up[layer], ffn_w_down[layer])
    return rms_norm(h, final_norm_g)
```

```python
import functools

import jax
import jax.numpy as jnp
from jax import lax
from jax.experimental import pallas as pl
from jax.experimental.pallas import tpu as pltpu

F32 = jnp.float32
BF16 = jnp.bfloat16

D_MODEL = 1024
HEAD_DIM = 64
ROT_DIM = 16
ROPE_THETA = 500000.0
BLOCK = 128
RMS_EPS = 1e-6
LN_EPS = 1e-5
N_HEADS = 8
ATT_W = N_HEADS * HEAD_DIM
CONV_CH = 512
CONV_WIDTH = 31
CONV_HALO = 32
D_CH = 512
D_GROUPS = 8
CHUNK = 128
D_FF = 2816
DILATIONS = (1, 4, 16)
LANES = 128
NEG = -1e30

TOK_TILE = 512
FF_CHUNK = 256
CONV_ROWS = 32
VMEM_LIMIT = 56 * 1024 * 1024


def _cparams(n_axes):
    return pltpu.CompilerParams(
        dimension_semantics=("arbitrary",) * n_axes,
        vmem_limit_bytes=VMEM_LIMIT)


def _const_spec(shape):
    nd = len(shape)
    return pl.BlockSpec(shape, lambda *_: (0,) * nd, pipeline_mode=pl.Buffered(1))


def _rms(x, g):
    ms = jnp.mean(x * x, axis=-1, keepdims=True)
    return x * lax.rsqrt(ms + RMS_EPS) * g


def _layer_norm(x, g, b):
    mu = jnp.mean(x, axis=-1, keepdims=True)
    xc = x - mu
    var = jnp.mean(xc * xc, axis=-1, keepdims=True)
    return xc * lax.rsqrt(var + LN_EPS) * g + b


def _dot(a, b):
    return jnp.dot(a, b, preferred_element_type=F32)


def _dot_nt(a, b):
    return lax.dot_general(a, b, (((1,), (1,)), ((), ())), preferred_element_type=F32)


def _rope(t, cos, sin_lo, sin_hi):
    up = pltpu.roll(t, LANES - ROT_DIM // 2, 1)
    down = pltpu.roll(t, ROT_DIM // 2, 1)
    return t * cos + up * sin_lo + down * sin_hi


def _rope_tables(seq):
    half = ROT_DIM // 2
    inv_freq = ROPE_THETA ** (-jnp.arange(half, dtype=F32) * (2.0 / ROT_DIM))
    ang = jnp.arange(seq, dtype=jnp.int32).astype(F32)[:, None] * inv_freq[None, :]
    cos, sin = jnp.cos(ang), jnp.sin(ang)
    ones = jnp.ones((seq, HEAD_DIM - ROT_DIM), F32)
    zeros = jnp.zeros((seq, HEAD_DIM - ROT_DIM), F32)
    zh = jnp.zeros((seq, half), F32)
    cos_h = jnp.concatenate([cos, cos, ones], axis=1)
    lo_h = jnp.concatenate([-sin, zh, zeros], axis=1)
    hi_h = jnp.concatenate([zh, sin, zeros], axis=1)
    two = lambda t: jnp.concatenate([t, t], axis=1)
    return two(cos_h), two(lo_h), two(hi_h)


def _ev_front_kernel(x_ref, g_ref, w_ref, cos_ref, lo_ref, hi_ref,
                     q_ref, k_ref, v_ref, c_ref):
    hn = _rms(x_ref[...], g_ref[...]).astype(BF16)
    cos, lo, hi = cos_ref[...], lo_ref[...], hi_ref[...]
    qkv = _dot(hn, w_ref[:, 0:ATT_W + 2 * LANES])
    for j in range(ATT_W // LANES):
        t = qkv[:, j * LANES:(j + 1) * LANES]
        q_ref[:, j * LANES:(j + 1) * LANES] = (
            _rope(t, cos, lo, hi) * (HEAD_DIM ** -0.5)).astype(BF16)
    k_ref[...] = _rope(qkv[:, ATT_W:ATT_W + LANES], cos, lo, hi).astype(BF16)
    v_ref[...] = qkv[:, ATT_W + LANES:ATT_W + 2 * LANES].astype(BF16)
    base = ATT_W + 2 * LANES
    ga = _dot(hn, w_ref[:, base:base + CONV_CH])
    gb = _dot(hn, w_ref[:, base + CONV_CH:base + 2 * CONV_CH])
    c_ref[...] = ga * jax.nn.sigmoid(gb)


def _od_front_kernel(x_ref, g_ref, w_ref, cos_ref, lo_ref, hi_ref, lng_ref, lnb_ref,
                     q_ref, k_ref, v_ref, u_ref, gate_ref):
    hn = _rms(x_ref[...], g_ref[...]).astype(BF16)
    cos, lo, hi = cos_ref[...], lo_ref[...], hi_ref[...]
    q = _dot(hn, w_ref[:, 0:ATT_W])
    k = _dot(hn, w_ref[:, ATT_W:2 * ATT_W])
    for j in range(ATT_W // LANES):
        sl = slice(j * LANES, (j + 1) * LANES)
        q_ref[:, sl] = (_rope(q[:, sl], cos, lo, hi) * (HEAD_DIM ** -0.5)).astype(BF16)
        k_ref[:, sl] = _rope(k[:, sl], cos, lo, hi).astype(BF16)
    v_ref[...] = _dot(hn, w_ref[:, 2 * ATT_W:3 * ATT_W]).astype(BF16)
    u_ref[...] = jax.nn.gelu(_dot(hn, w_ref[:, 3 * ATT_W:3 * ATT_W + D_CH]))
    zg = jax.nn.gelu(_dot(hn, w_ref[:, 3 * ATT_W + D_CH:3 * ATT_W + 2 * D_CH]))
    gate_ref[...] = _layer_norm(zg, lng_ref[...], lnb_ref[...]).astype(BF16)


def _front_common(x2, seq):
    n_tok = x2.shape[0]
    nt = n_tok // TOK_TILE
    per_seq = seq // TOK_TILE
    x_spec = pl.BlockSpec((TOK_TILE, D_MODEL), lambda i: (i, 0))
    tab_spec = pl.BlockSpec((TOK_TILE, LANES), lambda i: (i % per_seq, 0))
    out = lambda w, dt: (jax.ShapeDtypeStruct((n_tok, w), dt),
                         pl.BlockSpec((TOK_TILE, w), lambda i: (i, 0)))
    return nt, x_spec, tab_spec, out


def _ev_front(x2, g, w, tabs, seq):
    nt, x_spec, tab_spec, out = _front_common(x2, seq)
    outs = [out(ATT_W, BF16), out(LANES, BF16), out(LANES, BF16), out(CONV_CH, F32)]
    return pl.pallas_call(
        _ev_front_kernel,
        grid=(nt,),
        in_specs=[x_spec, _const_spec(g.shape), _const_spec(w.shape),
                  tab_spec, tab_spec, tab_spec],
        out_specs=[o[1] for o in outs],
        out_shape=[o[0] for o in outs],
        compiler_params=_cparams(1),
        name="ev_front",
    )(x2, g, w, *tabs)


def _od_front(x2, g, w, tabs, lng, lnb, seq):
    nt, x_spec, tab_spec, out = _front_common(x2, seq)
    outs = [out(ATT_W, BF16), out(ATT_W, BF16), out(ATT_W, BF16),
            out(D_CH, F32), out(D_CH, BF16)]
    return pl.pallas_call(
        _od_front_kernel,
        grid=(nt,),
        in_specs=[x_spec, _const_spec(g.shape), _const_spec(w.shape),
                  tab_spec, tab_spec, tab_spec,
                  _const_spec(lng.shape), _const_spec(lnb.shape)],
        out_specs=[o[1] for o in outs],
        out_shape=[o[0] for o in outs],
        compiler_params=_cparams(1),
        name="od_front",
    )(x2, g, w, *tabs, lng, lnb)


def _band_masks(min_dist):
    qi = lax.broadcasted_iota(jnp.int32, (BLOCK, 2 * BLOCK), 0)
    kj = lax.broadcasted_iota(jnp.int32, (BLOCK, 2 * BLOCK), 1)
    dist = qi + BLOCK - kj
    band = (dist >= 0) & (dist <= BLOCK - 1 + min_dist)
    return band, kj < BLOCK


def _softmax_rows(s, band, prev_keys, prev_penalty, sink):
    s = jnp.where(band, s, NEG)
    if prev_penalty is not None:
        s = jnp.where(prev_keys, s + prev_penalty, s)
    m = jnp.max(s, axis=-1, keepdims=True)
    if sink is not None:
        m = jnp.maximum(m, sink)
    p = jnp.exp(s - m)
    l = jnp.sum(p, axis=-1, keepdims=True)
    if sink is not None:
        l = l + jnp.exp(sink - m)
    return p, m, l


def _ev_mix_kernel(sink_ref, x_ref, q_ref, k_ref, kh_ref, v_ref, vh_ref, c_ref, ch_ref,
                   convw_ref, convb_ref, lng_ref, lnb_ref, wout_ref, o_ref,
                   kbuf, vbuf, cbuf, mixbuf):
    i = pl.program_id(1)
    tile = q_ref.shape[0]
    n_pairs = ATT_W // LANES

    kbuf[0:BLOCK, :] = kh_ref[...]
    kbuf[BLOCK:, :] = k_ref[...]
    vbuf[0:BLOCK, :] = vh_ref[...]
    vbuf[BLOCK:, :] = v_ref[...]

    band, prev_keys = _band_masks(0)
    first_penalty = jnp.where(i == 0, NEG, 0.0).astype(F32)
    low_lanes = lax.broadcasted_iota(jnp.int32, (BLOCK, LANES), 1) < HEAD_DIM

    for jb in range(tile // BLOCK):
        rows = slice(jb * BLOCK, (jb + 1) * BLOCK)
        kk = kbuf[jb * BLOCK:(jb + 2) * BLOCK, :]
        vv = vbuf[jb * BLOCK:(jb + 2) * BLOCK, :]
        parts = []
        for half in range(2):
            keep = low_lanes if half == 0 else jnp.logical_not(low_lanes)
            for p in range(n_pairs):
                qp = q_ref[rows, p * LANES:(p + 1) * LANES]
                parts.append(jnp.where(keep, qp, jnp.zeros_like(qp)))
        s_all = _dot_nt(jnp.concatenate(parts, axis=0), kk)
        probs, inv_l = [], []
        for idx in range(2 * n_pairs):
            half, p = divmod(idx, n_pairs)
            sink = sink_ref[half * n_pairs + p]
            pr, _, l = _softmax_rows(s_all[idx * BLOCK:(idx + 1) * BLOCK], band, prev_keys,
                                     first_penalty if jb == 0 else None, sink)
            probs.append(pr.astype(BF16))
            inv_l.append(1.0 / l)
        o_all = _dot(jnp.concatenate(probs, axis=0), vv)
        for p in range(n_pairs):
            o_lo = o_all[p * BLOCK:(p + 1) * BLOCK] * inv_l[p]
            o_hi = o_all[(n_pairs + p) * BLOCK:(n_pairs + p + 1) * BLOCK] * inv_l[n_pairs + p]
            mixbuf[rows, p * LANES:(p + 1) * LANES] = jnp.where(low_lanes, o_lo, o_hi).astype(BF16)

    cbuf[0:CONV_HALO, :] = jnp.where(i > 0, ch_ref[...], 0.0)
    cbuf[CONV_HALO:, :] = c_ref[...]
    lead = CONV_HALO - (CONV_WIDTH - 1)
    for rc in range(tile // CONV_ROWS):
        r0 = rc * CONV_ROWS
        acc = jnp.broadcast_to(convb_ref[...], (CONV_ROWS, CONV_CH))
        for tap in range(CONV_WIDTH):
            acc = acc + cbuf[r0 + lead + tap:r0 + lead + tap + CONV_ROWS, :] * convw_ref[tap:tap + 1, :]
        y = _layer_norm(acc, lng_ref[...], lnb_ref[...])
        mixbuf[r0:r0 + CONV_ROWS, ATT_W:] = (y * jax.nn.sigmoid(y)).astype(BF16)

    o_ref[...] = x_ref[...] + _dot(mixbuf[...], wout_ref[...])


def _ev_mix(x2, q, k, v, c, sinks, convw, convb, lng, lnb, wout, batch, seq):
    tile = TOK_TILE
    nt = seq // tile
    tok = lambda w: pl.BlockSpec((tile, w), lambda b, i: (b * nt + i, 0))
    kv_halo = pl.BlockSpec(
        (BLOCK, LANES), lambda b, i: (jnp.maximum((b * nt + i) * (tile // BLOCK) - 1, 0), 0))
    c_halo = pl.BlockSpec(
        (CONV_HALO, CONV_CH),
        lambda b, i: (jnp.maximum((b * nt + i) * (tile // CONV_HALO) - 1, 0), 0))
    return pl.pallas_call(
        _ev_mix_kernel,
        grid=(batch, nt),
        in_specs=[pl.BlockSpec(memory_space=pltpu.SMEM),
                  tok(D_MODEL), tok(ATT_W), tok(LANES), kv_halo, tok(LANES), kv_halo,
                  tok(CONV_CH), c_halo,
                  _const_spec(convw.shape), _const_spec(convb.shape),
                  _const_spec(lng.shape), _const_spec(lnb.shape), _const_spec(wout.shape)],
        out_specs=tok(D_MODEL),
        out_shape=jax.ShapeDtypeStruct(x2.shape, F32),
        scratch_shapes=[pltpu.VMEM((tile + BLOCK, LANES), BF16),
                        pltpu.VMEM((tile + BLOCK, LANES), BF16),
                        pltpu.VMEM((tile + CONV_HALO, CONV_CH), F32),
                        pltpu.VMEM((tile, ATT_W + CONV_CH), BF16)],
        compiler_params=_cparams(2),
        name="ev_mix",
    )(sinks, x2, q, k, k, v, v, c, c, convw, convb, lng, lnb, wout)


def _branch_kernel(q_ref, k_ref, kh_ref, v_ref, vh_ref, o_ref, lse_ref, kbuf, vbuf):
    i = pl.program_id(2)
    tile = q_ref.shape[0]

    kbuf[0:BLOCK, :] = kh_ref[...]
    kbuf[BLOCK:, :] = k_ref[...]
    vbuf[0:BLOCK, :] = vh_ref[...]
    vbuf[BLOCK:, :] = v_ref[...]

    band, prev_keys = _band_masks(1)
    first_penalty = jnp.where(i == 0, NEG, 0.0).astype(F32)
    low_lanes = lax.broadcasted_iota(jnp.int32, (BLOCK, LANES), 1) < HEAD_DIM

    for jb in range(tile // BLOCK):
        rows = slice(jb * BLOCK, (jb + 1) * BLOCK)
        for p in range(ATT_W // LANES):
            lanes = slice(p * LANES, (p + 1) * LANES)
            kk = kbuf[jb * BLOCK:(jb + 2) * BLOCK, lanes]
            vv = vbuf[jb * BLOCK:(jb + 2) * BLOCK, lanes]
            qp = q_ref[rows, lanes]
            zero = jnp.zeros_like(qp)
            q2 = jnp.concatenate([jnp.where(low_lanes, qp, zero),
                                  jnp.where(low_lanes, zero, qp)], axis=0)
            s2 = _dot_nt(q2, kk)
            probs, inv_l, lse = [], [], []
            for half in range(2):
                pr, m, l = _softmax_rows(s2[half * BLOCK:(half + 1) * BLOCK], band, prev_keys,
                                         first_penalty if jb == 0 else None, None)
                probs.append(pr.astype(BF16))
                inv_l.append(1.0 / l)
                lse.append(m + jnp.log(l))
            o2 = _dot(jnp.concatenate(probs, axis=0), vv)
            o_ref[rows, lanes] = jnp.where(low_lanes, o2[0:BLOCK] * inv_l[0],
                                           o2[BLOCK:] * inv_l[1])
            lse_ref[rows, lanes] = jnp.where(low_lanes, lse[0], lse[1])


def _branch(q, k, v, batch, seq, dil):
    sub = seq // dil
    tile = min(TOK_TILE, sub)
    nt = sub // tile
    fold = lambda t: t.reshape(batch, sub, dil * ATT_W)
    cur = pl.BlockSpec((None, tile, ATT_W), lambda b, r, i: (b, i, r))
    halo = pl.BlockSpec(
        (None, BLOCK, ATT_W), lambda b, r, i: (b, jnp.maximum(i * (tile // BLOCK) - 1, 0), r))
    out_sds = jax.ShapeDtypeStruct((batch, sub, dil * ATT_W), F32)
    o, lse = pl.pallas_call(
        _branch_kernel,
        grid=(batch, dil, nt),
        in_specs=[cur, cur, halo, cur, halo],
        out_specs=[cur, cur],
        out_shape=[out_sds, out_sds],
        scratch_shapes=[pltpu.VMEM((tile + BLOCK, ATT_W), BF16),
                        pltpu.VMEM((tile + BLOCK, ATT_W), BF16)],
        compiler_params=_cparams(3),
        name=f"od_branch_d{dil}",
    )(fold(q), fold(k), fold(k), fold(v), fold(v))
    return o.reshape(batch * seq, ATT_W), lse.reshape(batch * seq, ATT_W)


def _od_mix_kernel(x_ref, o1_ref, o2_ref, o3_ref, l1_ref, l2_ref, l3_ref, u_ref, gate_ref,
                   sw_ref, sb_ref, wout_ref, out_ref, mixbuf):
    tile = x_ref.shape[0]
    l1, l2, l3 = l1_ref[...], l2_ref[...], l3_ref[...]
    m = jnp.maximum(jnp.maximum(l1, l2), l3)
    e1, e2, e3 = jnp.exp(l1 - m), jnp.exp(l2 - m), jnp.exp(l3 - m)
    mixed = (e1 * o1_ref[...] + e2 * o2_ref[...] + e3 * o3_ref[...]) / (e1 + e2 + e3)
    mixbuf[:, 0:ATT_W] = mixed.astype(BF16)

    ti = lax.broadcasted_iota(jnp.int32, (CHUNK, CHUNK), 0)
    si = lax.broadcasted_iota(jnp.int32, (CHUNK, CHUNK), 1)
    causal = ti >= si
    low_lanes = lax.broadcasted_iota(jnp.int32, (CHUNK, LANES), 1) < HEAD_DIM
    for p in range(D_CH // LANES):
        lanes = slice(p * LANES, (p + 1) * LANES)
        w2 = jnp.concatenate(
            [jnp.where(causal, sw_ref[2 * p], 0.0), jnp.where(causal, sw_ref[2 * p + 1], 0.0)],
            axis=0).astype(BF16)
        bias = sb_ref[:, lanes]
        for ch in range(tile // CHUNK):
            rows = slice(ch * CHUNK, (ch + 1) * CHUNK)
            y2 = _dot(w2, gate_ref[rows, lanes])
            y = jnp.where(low_lanes, y2[0:CHUNK], y2[CHUNK:]) + bias
            mixbuf[rows, ATT_W + p * LANES:ATT_W + (p + 1) * LANES] = (
                u_ref[rows, lanes] * y).astype(BF16)

    out_ref[...] = x_ref[...] + _dot(mixbuf[...], wout_ref[...])


def _od_mix(x2, outs, lses, u, gate, sw, sb, wout):
    n_tok = x2.shape[0]
    tile = TOK_TILE
    tok = lambda w: pl.BlockSpec((tile, w), lambda i: (i, 0))
    return pl.pallas_call(
        _od_mix_kernel,
        grid=(n_tok // tile,),
        in_specs=[tok(D_MODEL)] + [tok(ATT_W)] * 6 + [tok(D_CH), tok(D_CH),
                  _const_spec(sw.shape), _const_spec(sb.shape), _const_spec(wout.shape)],
        out_specs=tok(D_MODEL),
        out_shape=jax.ShapeDtypeStruct(x2.shape, F32),
        scratch_shapes=[pltpu.VMEM((tile, ATT_W + D_CH), BF16)],
        compiler_params=_cparams(1),
        name="od_mix",
    )(x2, *outs, *lses, u, gate, sw, sb, wout)


def _ffn_kernel(x_ref, g_ref, wg_ref, wu_ref, wd_ref, fg_ref, o_ref, actbuf, *, final_norm):
    x = x_ref[...]
    hn = _rms(x, g_ref[...]).astype(BF16)
    for c in range(D_FF // FF_CHUNK):
        cols = slice(c * FF_CHUNK, (c + 1) * FF_CHUNK)
        gate = _dot(hn, wg_ref[:, cols])
        up = _dot(hn, wu_ref[:, cols])
        actbuf[:, cols] = (gate * jax.nn.sigmoid(gate) * up).astype(BF16)
    y = x + _dot(actbuf[...], wd_ref[...])
    if final_norm:
        y = _rms(y, fg_ref[...])
    o_ref[...] = y


def _ffn(x2, g, wg, wu, wd, fg, final_norm):
    n_tok = x2.shape[0]
    tile = TOK_TILE
    tok = pl.BlockSpec((tile, D_MODEL), lambda i: (i, 0))
    return pl.pallas_call(
        functools.partial(_ffn_kernel, final_norm=final_norm),
        grid=(n_tok // tile,),
        in_specs=[tok, _const_spec(g.shape), _const_spec(wg.shape), _const_spec(wu.shape),
                  _const_spec(wd.shape), _const_spec(fg.shape)],
        out_specs=tok,
        out_shape=jax.ShapeDtypeStruct(x2.shape, F32),
        scratch_shapes=[pltpu.VMEM((tile, D_FF), BF16)],
        compiler_params=_cparams(1),
        name="ffn_final" if final_norm else "ffn",
    )(x2, g, wg, wu, wd, fg)


def kernel(x, ev_norm_g, ev_w_in, ev_sinks, ev_conv_w, ev_conv_b, ev_conv_ln_g, ev_conv_ln_b, ev_w_out, od_norm_g, od_w_in, od_sgu_ln_g, od_sgu_ln_b, od_spatial_w, od_spatial_b, od_w_out, ffn_norm_g, ffn_w_gate, ffn_w_up, ffn_w_down, final_norm_g):
    batch, seq, _ = x.shape
    assert seq % (max(DILATIONS) * BLOCK) == 0 and seq % TOK_TILE == 0
    row = lambda t: t.reshape(1, -1).astype(F32)
    x2 = x.reshape(batch * seq, D_MODEL)
    tabs = _rope_tables(seq)

    n_pairs = ATT_W // LANES
    head_order = [h for p in range(n_pairs) for h in (p, n_pairs + p)]
    q_cols = jnp.asarray([h * HEAD_DIM + d for h in head_order for d in range(HEAD_DIM)], jnp.int32)
    w_in0 = jnp.concatenate([ev_w_in[0][:, q_cols], ev_w_in[0][:, ATT_W:]], axis=1).astype(BF16)
    w_out0 = jnp.concatenate([ev_w_out[0][q_cols, :], ev_w_out[0][ATT_W:, :]], axis=0).astype(BF16)
    q, k, v, c = _ev_front(x2, row(ev_norm_g[0]), w_in0, tabs, seq)
    h = _ev_mix(x2, q, k, v, c, ev_sinks[0].astype(F32), ev_conv_w[0].astype(F32),
                row(ev_conv_b[0]), row(ev_conv_ln_g[0]), row(ev_conv_ln_b[0]), w_out0,
                batch, seq)
    h = _ffn(h, row(ffn_norm_g[0]), ffn_w_gate[0].astype(BF16), ffn_w_up[0].astype(BF16),
             ffn_w_down[0].astype(BF16), row(final_norm_g), final_norm=False)

    q, k, v, u, gate = _od_front(h, row(od_norm_g[0]), od_w_in[0].astype(BF16), tabs,
                                 row(od_sgu_ln_g[0]), row(od_sgu_ln_b[0]), seq)
    outs, lses = [], []
    for dil in DILATIONS:
        o_r, lse_r = _branch(q, k, v, batch, seq, dil)
        outs.append(o_r)
        lses.append(lse_r)
    sb = jnp.repeat(od_spatial_b[0].astype(F32).T, HEAD_DIM, axis=1)
    h = _od_mix(h, outs, lses, u, gate, od_spatial_w[0].astype(F32), sb,
                od_w_out[0].astype(BF16))
    h = _ffn(h, row(ffn_norm_g[1]), ffn_w_gate[1].astype(BF16), ffn_w_up[1].astype(BF16),
             ffn_w_down[1].astype(BF16), row(final_norm_g), final_norm=True)
    return h.reshape(batch, seq, D_MODEL)
```

```python
import functools

import jax
import jax.numpy as jnp
from jax import lax
from jax.experimental import pallas as pl
from jax.experimental.pallas import tpu as pltpu

F32 = jnp.float32
BF16 = jnp.bfloat16

D_MODEL = 1024
HEAD_DIM = 64
ROT_DIM = 16
ROPE_THETA = 500000.0
BLOCK = 128
RMS_EPS = 1e-6
LN_EPS = 1e-5
N_HEADS = 8
ATT_W = N_HEADS * HEAD_DIM
CONV_CH = 512
CONV_WIDTH = 31
CONV_HALO = 32
D_CH = 512
D_GROUPS = 8
CHUNK = 128
D_FF = 2816
DILATIONS = (1, 4, 16)
LANES = 128
SUBLANES = 8
NEG = -1e30

TOK_TILE = 512
FF_CHUNK = 256
CONV_ROWS = 32
VMEM_LIMIT = 56 * 1024 * 1024


def _cparams(n_axes):
    return pltpu.CompilerParams(
        dimension_semantics=("arbitrary",) * n_axes,
        vmem_limit_bytes=VMEM_LIMIT)


def _const_spec(shape):
    nd = len(shape)
    return pl.BlockSpec(shape, lambda *_: (0,) * nd, pipeline_mode=pl.Buffered(1))


def _rms(x, g):
    ms = jnp.mean(x * x, axis=-1, keepdims=True)
    return x * lax.rsqrt(ms + RMS_EPS) * g


def _layer_norm(x, g, b):
    mu = jnp.mean(x, axis=-1, keepdims=True)
    xc = x - mu
    var = jnp.mean(xc * xc, axis=-1, keepdims=True)
    return xc * lax.rsqrt(var + LN_EPS) * g + b


def _dot(a, b):
    return jnp.dot(a, b, preferred_element_type=F32)


def _dot_nt(a, b):
    return lax.dot_general(a, b, (((1,), (1,)), ((), ())), preferred_element_type=F32)


def _rope(t, cos, sin_lo, sin_hi):
    up = pltpu.roll(t, LANES - ROT_DIM // 2, 1)
    down = pltpu.roll(t, ROT_DIM // 2, 1)
    return t * cos + up * sin_lo + down * sin_hi


def _rope_tables(seq):
    half = ROT_DIM // 2
    inv_freq = ROPE_THETA ** (-jnp.arange(half, dtype=F32) * (2.0 / ROT_DIM))
    ang = jnp.arange(seq, dtype=jnp.int32).astype(F32)[:, None] * inv_freq[None, :]
    cos, sin = jnp.cos(ang), jnp.sin(ang)
    ones = jnp.ones((seq, HEAD_DIM - ROT_DIM), F32)
    zeros = jnp.zeros((seq, HEAD_DIM - ROT_DIM), F32)
    zh = jnp.zeros((seq, half), F32)
    cos_h = jnp.concatenate([cos, cos, ones], axis=1)
    lo_h = jnp.concatenate([-sin, zh, zeros], axis=1)
    hi_h = jnp.concatenate([zh, sin, zeros], axis=1)
    two = lambda t: jnp.concatenate([t, t], axis=1)
    return two(cos_h), two(lo_h), two(hi_h)


def _ev_front_kernel(x_ref, g_ref, w_ref, cos_ref, lo_ref, hi_ref,
                     q_ref, k_ref, v_ref, c_ref):
    hn = _rms(x_ref[...], g_ref[...]).astype(BF16)
    cos, lo, hi = cos_ref[...], lo_ref[...], hi_ref[...]
    qkv = _dot(hn, w_ref[:, 0:ATT_W + 2 * LANES])
    for j in range(ATT_W // LANES):
        t = qkv[:, j * LANES:(j + 1) * LANES]
        q_ref[:, j * LANES:(j + 1) * LANES] = (
            _rope(t, cos, lo, hi) * (HEAD_DIM ** -0.5)).astype(BF16)
    k_ref[...] = _rope(qkv[:, ATT_W:ATT_W + LANES], cos, lo, hi).astype(BF16)
    v_ref[...] = qkv[:, ATT_W + LANES:ATT_W + 2 * LANES].astype(BF16)
    base = ATT_W + 2 * LANES
    ga = _dot(hn, w_ref[:, base:base + CONV_CH])
    gb = _dot(hn, w_ref[:, base + CONV_CH:base + 2 * CONV_CH])
    c_ref[...] = ga * jax.nn.sigmoid(gb)


def _od_front_kernel(x_ref, g_ref, w_ref, cos_ref, lo_ref, hi_ref, lng_ref, lnb_ref,
                     qkv1_ref, qkv4_ref, qkv16_ref, u_ref, gate_ref, stage):
    tile = x_ref.shape[0]
    hn = _rms(x_ref[...], g_ref[...]).astype(BF16)
    cos, lo, hi = cos_ref[...], lo_ref[...], hi_ref[...]
    for s in range(3):
        t = _dot(hn, w_ref[:, s * ATT_W:(s + 1) * ATT_W])
        for j in range(ATT_W // LANES):
            sl = slice(j * LANES, (j + 1) * LANES)
            if s == 0:
                stage[j] = _rope(t[:, sl], cos, lo, hi) * (HEAD_DIM ** -0.5)
            elif s == 1:
                stage[j] = _rope(t[:, sl], cos, lo, hi)
            else:
                stage[j] = t[:, sl]
            qkv1_ref[s, 0, :, sl] = stage[j].astype(BF16)
            for dil, ref in ((DILATIONS[1], qkv4_ref), (DILATIONS[2], qkv16_ref)):
                for r in range(dil):
                    ref[s, r, :, sl] = stage[j, pl.ds(r, tile // dil, stride=dil), :].astype(BF16)
    u_ref[...] = jax.nn.gelu(_dot(hn, w_ref[:, 3 * ATT_W:3 * ATT_W + D_CH]))
    zg = jax.nn.gelu(_dot(hn, w_ref[:, 3 * ATT_W + D_CH:3 * ATT_W + 2 * D_CH]))
    gate_ref[...] = _layer_norm(zg, lng_ref[...], lnb_ref[...]).astype(BF16)


def _front_common(x2, seq):
    n_tok = x2.shape[0]
    nt = n_tok // TOK_TILE
    per_seq = seq // TOK_TILE
    x_spec = pl.BlockSpec((TOK_TILE, D_MODEL), lambda i: (i, 0))
    tab_spec = pl.BlockSpec((TOK_TILE, LANES), lambda i: (i % per_seq, 0))
    out = lambda w, dt: (jax.ShapeDtypeStruct((n_tok, w), dt),
                         pl.BlockSpec((TOK_TILE, w), lambda i: (i, 0)))
    return nt, x_spec, tab_spec, out


def _ev_front(x2, g, w, tabs, seq):
    nt, x_spec, tab_spec, out = _front_common(x2, seq)
    outs = [out(ATT_W, BF16), out(LANES, BF16), out(LANES, BF16), out(CONV_CH, F32)]
    return pl.pallas_call(
        _ev_front_kernel,
        grid=(nt,),
        in_specs=[x_spec, _const_spec(g.shape), _const_spec(w.shape),
                  tab_spec, tab_spec, tab_spec],
        out_specs=[o[1] for o in outs],
        out_shape=[o[0] for o in outs],
        compiler_params=_cparams(1),
        name="ev_front",
    )(x2, g, w, *tabs)


def _od_front(x2, g, w, tabs, lng, lnb, batch, seq):
    nt, x_spec, tab_spec, out = _front_common(x2, seq)
    per_seq = seq // TOK_TILE

    def folded(dil):
        return (jax.ShapeDtypeStruct((batch, 3, dil, seq // dil, ATT_W), BF16),
                pl.BlockSpec((None, 3, dil, TOK_TILE // dil, ATT_W),
                             lambda i: (i // per_seq, 0, 0, i % per_seq, 0)))

    outs = [folded(d) for d in DILATIONS] + [out(D_CH, F32), out(D_CH, BF16)]
    return pl.pallas_call(
        _od_front_kernel,
        grid=(nt,),
        in_specs=[x_spec, _const_spec(g.shape), _const_spec(w.shape),
                  tab_spec, tab_spec, tab_spec,
                  _const_spec(lng.shape), _const_spec(lnb.shape)],
        out_specs=[o[1] for o in outs],
        out_shape=[o[0] for o in outs],
        scratch_shapes=[pltpu.VMEM((ATT_W // LANES, TOK_TILE, LANES), F32)],
        compiler_params=_cparams(1),
        name="od_front",
    )(x2, g, w, *tabs, lng, lnb)


def _band_masks(min_dist):
    qi = lax.broadcasted_iota(jnp.int32, (BLOCK, 2 * BLOCK), 0)
    kj = lax.broadcasted_iota(jnp.int32, (BLOCK, 2 * BLOCK), 1)
    dist = qi + BLOCK - kj
    band = (dist >= 0) & (dist <= BLOCK - 1 + min_dist)
    return band, kj < BLOCK


def _softmax_rows(s, band, prev_keys, prev_penalty, sink):
    s = jnp.where(band, s, NEG)
    if prev_penalty is not None:
        s = jnp.where(prev_keys, s + prev_penalty, s)
    m = jnp.max(s, axis=-1, keepdims=True)
    if sink is not None:
        m = jnp.maximum(m, sink)
    p = jnp.exp(s - m)
    l = jnp.sum(p, axis=-1, keepdims=True)
    if sink is not None:
        l = l + jnp.exp(sink - m)
    return p, m, l


def _ev_mix_kernel(sink_ref, x_ref, q_ref, k_ref, kh_ref, v_ref, vh_ref, c_ref, ch_ref,
                   convw_ref, convb_ref, lng_ref, lnb_ref, wout_ref, o_ref,
                   kbuf, vbuf, cbuf, shifted, mixbuf):
    i = pl.program_id(1)
    tile = q_ref.shape[0]
    n_pairs = ATT_W // LANES

    kbuf[0:BLOCK, :] = kh_ref[...]
    kbuf[BLOCK:, :] = k_ref[...]
    vbuf[0:BLOCK, :] = vh_ref[...]
    vbuf[BLOCK:, :] = v_ref[...]

    band, prev_keys = _band_masks(0)
    first_penalty = jnp.where(i == 0, NEG, 0.0).astype(F32)
    low_lanes = lax.broadcasted_iota(jnp.int32, (BLOCK, LANES), 1) < HEAD_DIM

    for jb in range(tile // BLOCK):
        rows = slice(jb * BLOCK, (jb + 1) * BLOCK)
        kk = kbuf[jb * BLOCK:(jb + 2) * BLOCK, :]
        vv = vbuf[jb * BLOCK:(jb + 2) * BLOCK, :]
        parts = []
        for half in range(2):
            keep = low_lanes if half == 0 else jnp.logical_not(low_lanes)
            for p in range(n_pairs):
                qp = q_ref[rows, p * LANES:(p + 1) * LANES]
                parts.append(jnp.where(keep, qp, jnp.zeros_like(qp)))
        s_all = _dot_nt(jnp.concatenate(parts, axis=0), kk)
        probs, inv_l = [], []
        for idx in range(2 * n_pairs):
            half, p = divmod(idx, n_pairs)
            sink = sink_ref[half * n_pairs + p]
            pr, _, l = _softmax_rows(s_all[idx * BLOCK:(idx + 1) * BLOCK], band, prev_keys,
                                     first_penalty if jb == 0 else None, sink)
            probs.append(pr.astype(BF16))
            inv_l.append(1.0 / l)
        o_all = _dot(jnp.concatenate(probs, axis=0), vv)
        for p in range(n_pairs):
            o_lo = o_all[p * BLOCK:(p + 1) * BLOCK] * inv_l[p]
            o_hi = o_all[(n_pairs + p) * BLOCK:(n_pairs + p + 1) * BLOCK] * inv_l[n_pairs + p]
            mixbuf[rows, p * LANES:(p + 1) * LANES] = jnp.where(low_lanes, o_lo, o_hi).astype(BF16)

    cbuf[0:CONV_HALO, :] = jnp.where(i > 0, ch_ref[...], 0.0)
    cbuf[CONV_HALO:, :] = c_ref[...]
    for s in range(1, SUBLANES):
        shifted[s - 1, SUBLANES:, :] = cbuf[SUBLANES - s:tile + CONV_HALO - s, :]
    lead = CONV_HALO - (CONV_WIDTH - 1)
    for rc in range(tile // CONV_ROWS):
        r0 = rc * CONV_ROWS
        acc = jnp.broadcast_to(convb_ref[...], (CONV_ROWS, CONV_CH))
        for tap in range(CONV_WIDTH):
            s = (-(lead + tap)) % SUBLANES
            base = r0 + lead + tap + s
            src = cbuf if s == 0 else shifted.at[s - 1]
            acc = acc + src[base:base + CONV_ROWS, :] * convw_ref[tap:tap + 1, :]
        y = _layer_norm(acc, lng_ref[...], lnb_ref[...])
        mixbuf[r0:r0 + CONV_ROWS, ATT_W:] = (y * jax.nn.sigmoid(y)).astype(BF16)

    o_ref[...] = x_ref[...] + _dot(mixbuf[...], wout_ref[...])


def _ev_mix(x2, q, k, v, c, sinks, convw, convb, lng, lnb, wout, batch, seq):
    tile = TOK_TILE
    nt = seq // tile
    tok = lambda w: pl.BlockSpec((tile, w), lambda b, i: (b * nt + i, 0))
    kv_halo = pl.BlockSpec(
        (BLOCK, LANES), lambda b, i: (jnp.maximum((b * nt + i) * (tile // BLOCK) - 1, 0), 0))
    c_halo = pl.BlockSpec(
        (CONV_HALO, CONV_CH),
        lambda b, i: (jnp.maximum((b * nt + i) * (tile // CONV_HALO) - 1, 0), 0))
    return pl.pallas_call(
        _ev_mix_kernel,
        grid=(batch, nt),
        in_specs=[pl.BlockSpec(memory_space=pltpu.SMEM),
                  tok(D_MODEL), tok(ATT_W), tok(LANES), kv_halo, tok(LANES), kv_halo,
                  tok(CONV_CH), c_halo,
                  _const_spec(convw.shape), _const_spec(convb.shape),
                  _const_spec(lng.shape), _const_spec(lnb.shape), _const_spec(wout.shape)],
        out_specs=tok(D_MODEL),
        out_shape=jax.ShapeDtypeStruct(x2.shape, F32),
        scratch_shapes=[pltpu.VMEM((tile + BLOCK, LANES), BF16),
                        pltpu.VMEM((tile + BLOCK, LANES), BF16),
                        pltpu.VMEM((tile + CONV_HALO, CONV_CH), F32),
                        pltpu.VMEM((SUBLANES - 1, tile + CONV_HALO, CONV_CH), F32),
                        pltpu.VMEM((tile, ATT_W + CONV_CH), BF16)],
        compiler_params=_cparams(2),
        name="ev_mix",
    )(sinks, x2, q, k, k, v, v, c, c, convw, convb, lng, lnb, wout)


def _branch_kernel(q_ref, k_ref, kh_ref, v_ref, vh_ref, o_ref, lse_ref, kbuf, vbuf):
    i = pl.program_id(2)
    tile = q_ref.shape[0]

    kbuf[0:BLOCK, :] = kh_ref[...]
    kbuf[BLOCK:, :] = k_ref[...]
    vbuf[0:BLOCK, :] = vh_ref[...]
    vbuf[BLOCK:, :] = v_ref[...]

    band, prev_keys = _band_masks(1)
    first_penalty = jnp.where(i == 0, NEG, 0.0).astype(F32)
    low_lanes = lax.broadcasted_iota(jnp.int32, (BLOCK, LANES), 1) < HEAD_DIM

    for jb in range(tile // BLOCK):
        rows = slice(jb * BLOCK, (jb + 1) * BLOCK)
        for p in range(ATT_W // LANES):
            lanes = slice(p * LANES, (p + 1) * LANES)
            kk = kbuf[jb * BLOCK:(jb + 2) * BLOCK, lanes]
            vv = vbuf[jb * BLOCK:(jb + 2) * BLOCK, lanes]
            qp = q_ref[rows, lanes]
            zero = jnp.zeros_like(qp)
            q2 = jnp.concatenate([jnp.where(low_lanes, qp, zero),
                                  jnp.where(low_lanes, zero, qp)], axis=0)
            s2 = _dot_nt(q2, kk)
            probs, inv_l, lse = [], [], []
            for half in range(2):
                pr, m, l = _softmax_rows(s2[half * BLOCK:(half + 1) * BLOCK], band, prev_keys,
                                         first_penalty if jb == 0 else None, None)
                probs.append(pr.astype(BF16))
                inv_l.append(1.0 / l)
                lse.append(m + jnp.log(l))
            o2 = _dot(jnp.concatenate(probs, axis=0), vv)
            o_ref[rows, lanes] = jnp.where(low_lanes, o2[0:BLOCK] * inv_l[0],
                                           o2[BLOCK:] * inv_l[1])
            lse_ref[rows, lanes] = jnp.where(low_lanes, lse[0], lse[1])


def _branch(qkv, batch, seq, dil):
    sub = seq // dil
    tile = min(TOK_TILE, sub)
    nt = sub // tile

    def cur(s):
        return pl.BlockSpec((None, None, None, tile, ATT_W), lambda b, r, i: (b, s, r, i, 0))

    def halo(s):
        return pl.BlockSpec(
            (None, None, None, BLOCK, ATT_W),
            lambda b, r, i: (b, s, r, jnp.maximum(i * (tile // BLOCK) - 1, 0), 0))

    out_spec = pl.BlockSpec((None, None, tile, ATT_W), lambda b, r, i: (b, r, i, 0))
    out_sds = jax.ShapeDtypeStruct((batch, dil, sub, ATT_W), F32)
    return pl.pallas_call(
        _branch_kernel,
        grid=(batch, dil, nt),
        in_specs=[cur(0), cur(1), halo(1), cur(2), halo(2)],
        out_specs=[out_spec, out_spec],
        out_shape=[out_sds, out_sds],
        scratch_shapes=[pltpu.VMEM((tile + BLOCK, ATT_W), BF16),
                        pltpu.VMEM((tile + BLOCK, ATT_W), BF16)],
        compiler_params=_cparams(3),
        name=f"od_branch_d{dil}",
    )(qkv, qkv, qkv, qkv, qkv)


def _od_mix_kernel(x_ref, o1_ref, o2_ref, o3_ref, l1_ref, l2_ref, l3_ref, u_ref, gate_ref,
                   sw_ref, sb_ref, wout_ref, out_ref, mixbuf, tok_o2, tok_l2, tok_o3, tok_l3):
    tile = x_ref.shape[0]
    for j in range(ATT_W // LANES):
        sl = slice(j * LANES, (j + 1) * LANES)
        for src, dst in ((o2_ref, tok_o2), (l2_ref, tok_l2), (o3_ref, tok_o3), (l3_ref, tok_l3)):
            dil = src.shape[0]
            for r in range(dil):
                dst[j, pl.ds(r, tile // dil, stride=dil), :] = src[r, :, sl]
        l1, l2, l3 = l1_ref[0, :, sl], tok_l2[j], tok_l3[j]
        m = jnp.maximum(jnp.maximum(l1, l2), l3)
        e1, e2, e3 = jnp.exp(l1 - m), jnp.exp(l2 - m), jnp.exp(l3 - m)
        mixed = (e1 * o1_ref[0, :, sl] + e2 * tok_o2[j] + e3 * tok_o3[j]) / (e1 + e2 + e3)
        mixbuf[:, sl] = mixed.astype(BF16)

    ti = lax.broadcasted_iota(jnp.int32, (CHUNK, CHUNK), 0)
    si = lax.broadcasted_iota(jnp.int32, (CHUNK, CHUNK), 1)
    causal = ti >= si
    low_lanes = lax.broadcasted_iota(jnp.int32, (CHUNK, LANES), 1) < HEAD_DIM
    for p in range(D_CH // LANES):
        lanes = slice(p * LANES, (p + 1) * LANES)
        w2 = jnp.concatenate(
            [jnp.where(causal, sw_ref[2 * p], 0.0), jnp.where(causal, sw_ref[2 * p + 1], 0.0)],
            axis=0).astype(BF16)
        bias = sb_ref[:, lanes]
        for ch in range(tile // CHUNK):
            rows = slice(ch * CHUNK, (ch + 1) * CHUNK)
            y2 = _dot(w2, gate_ref[rows, lanes])
            y = jnp.where(low_lanes, y2[0:CHUNK], y2[CHUNK:]) + bias
            mixbuf[rows, ATT_W + p * LANES:ATT_W + (p + 1) * LANES] = (
                u_ref[rows, lanes] * y).astype(BF16)

    out_ref[...] = x_ref[...] + _dot(mixbuf[...], wout_ref[...])


def _od_mix(x2, outs, lses, u, gate, sw, sb, wout, seq):
    n_tok = x2.shape[0]
    tile = TOK_TILE
    per_seq = seq // tile
    tok = lambda w: pl.BlockSpec((tile, w), lambda i: (i, 0))
    folded = [pl.BlockSpec((None, d, tile // d, ATT_W),
                           lambda i: (i // per_seq, 0, i % per_seq, 0)) for d in DILATIONS]
    return pl.pallas_call(
        _od_mix_kernel,
        grid=(n_tok // tile,),
        in_specs=[tok(D_MODEL)] + folded + folded + [tok(D_CH), tok(D_CH),
                  _const_spec(sw.shape), _const_spec(sb.shape), _const_spec(wout.shape)],
        out_specs=tok(D_MODEL),
        out_shape=jax.ShapeDtypeStruct(x2.shape, F32),
        scratch_shapes=[pltpu.VMEM((tile, ATT_W + D_CH), BF16)]
                       + [pltpu.VMEM((ATT_W // LANES, tile, LANES), F32)] * 4,
        compiler_params=_cparams(1),
        name="od_mix",
    )(x2, *outs, *lses, u, gate, sw, sb, wout)


def _ffn_kernel(x_ref, g_ref, wg_ref, wu_ref, wd_ref, fg_ref, o_ref, actbuf, *, final_norm):
    x = x_ref[...]
    hn = _rms(x, g_ref[...]).astype(BF16)
    for c in range(D_FF // FF_CHUNK):
        cols = slice(c * FF_CHUNK, (c + 1) * FF_CHUNK)
        gate = _dot(hn, wg_ref[:, cols])
        up = _dot(hn, wu_ref[:, cols])
        actbuf[:, cols] = (gate * jax.nn.sigmoid(gate) * up).astype(BF16)
    y = x + _dot(actbuf[...], wd_ref[...])
    if final_norm:
        y = _rms(y, fg_ref[...])
    o_ref[...] = y


def _ffn(x2, g, wg, wu, wd, fg, final_norm):
    n_tok = x2.shape[0]
    tile = TOK_TILE
    tok = pl.BlockSpec((tile, D_MODEL), lambda i: (i, 0))
    return pl.pallas_call(
        functools.partial(_ffn_kernel, final_norm=final_norm),
        grid=(n_tok // tile,),
        in_specs=[tok, _const_spec(g.shape), _const_spec(wg.shape), _const_spec(wu.shape),
                  _const_spec(wd.shape), _const_spec(fg.shape)],
        out_specs=tok,
        out_shape=jax.ShapeDtypeStruct(x2.shape, F32),
        scratch_shapes=[pltpu.VMEM((tile, D_FF), BF16)],
        compiler_params=_cparams(1),
        name="ffn_final" if final_norm else "ffn",
    )(x2, g, wg, wu, wd, fg)


def kernel(x, ev_norm_g, ev_w_in, ev_sinks, ev_conv_w, ev_conv_b, ev_conv_ln_g, ev_conv_ln_b, ev_w_out, od_norm_g, od_w_in, od_sgu_ln_g, od_sgu_ln_b, od_spatial_w, od_spatial_b, od_w_out, ffn_norm_g, ffn_w_gate, ffn_w_up, ffn_w_down, final_norm_g):
    batch, seq, _ = x.shape
    assert seq % (max(DILATIONS) * BLOCK) == 0 and seq % TOK_TILE == 0
    row = lambda t: t.reshape(1, -1).astype(F32)
    x2 = x.reshape(batch * seq, D_MODEL)
    tabs = _rope_tables(seq)

    n_pairs = ATT_W // LANES
    head_order = [h for p in range(n_pairs) for h in (p, n_pairs + p)]
    q_cols = jnp.asarray([h * HEAD_DIM + d for h in head_order for d in range(HEAD_DIM)], jnp.int32)
    w_in0 = jnp.concatenate([ev_w_in[0][:, q_cols], ev_w_in[0][:, ATT_W:]], axis=1).astype(BF16)
    w_out0 = jnp.concatenate([ev_w_out[0][q_cols, :], ev_w_out[0][ATT_W:, :]], axis=0).astype(BF16)
    q, k, v, c = _ev_front(x2, row(ev_norm_g[0]), w_in0, tabs, seq)
    h = _ev_mix(x2, q, k, v, c, ev_sinks[0].astype(F32), ev_conv_w[0].astype(F32),
                row(ev_conv_b[0]), row(ev_conv_ln_g[0]), row(ev_conv_ln_b[0]), w_out0,
                batch, seq)
    h = _ffn(h, row(ffn_norm_g[0]), ffn_w_gate[0].astype(BF16), ffn_w_up[0].astype(BF16),
             ffn_w_down[0].astype(BF16), row(final_norm_g), final_norm=False)

    *qkvs, u, gate = _od_front(h, row(od_norm_g[0]), od_w_in[0].astype(BF16), tabs,
                               row(od_sgu_ln_g[0]), row(od_sgu_ln_b[0]), batch, seq)
    outs, lses = [], []
    for dil, qkv in zip(DILATIONS, qkvs):
        o_r, lse_r = _branch(qkv, batch, seq, dil)
        outs.append(o_r)
        lses.append(lse_r)
    sb = jnp.repeat(od_spatial_b[0].astype(F32).T, HEAD_DIM, axis=1)
    h = _od_mix(h, outs, lses, u, gate, od_spatial_w[0].astype(F32), sb,
                od_w_out[0].astype(BF16), seq)
    h = _ffn(h, row(ffn_norm_g[1]), ffn_w_gate[1].astype(BF16), ffn_w_up[1].astype(BF16),
             ffn_w_down[1].astype(BF16), row(final_norm_g), final_norm=True)
    return h.reshape(batch, seq, D_MODEL)
```

```python
import functools

import jax
import jax.numpy as jnp
import numpy as np
from jax import lax
from jax.experimental import pallas as pl
from jax.experimental.pallas import tpu as pltpu

F32 = jnp.float32
BF16 = jnp.bfloat16

D_MODEL = 1024
HEAD_DIM = 64
ROT_DIM = 16
ROPE_THETA = 500000.0
BLOCK = 128
RMS_EPS = 1e-6
LN_EPS = 1e-5
N_HEADS = 8
ATT_W = N_HEADS * HEAD_DIM
CONV_CH = 512
CONV_WIDTH = 31
CONV_HALO = 32
D_CH = 512
D_GROUPS = 8
CHUNK = 128
D_FF = 2816
DILATIONS = (1, 4, 16)
LANES = 128
SUBLANES = 8
NEG = -1e30

TOK_TILE = 512
FF_CHUNK = 256
CONV_TAPS = 16
VMEM_LIMIT = 56 * 1024 * 1024


def _cparams(n_axes):
    return pltpu.CompilerParams(
        dimension_semantics=("arbitrary",) * n_axes,
        vmem_limit_bytes=VMEM_LIMIT)


def _const_spec(shape):
    nd = len(shape)
    return pl.BlockSpec(shape, lambda *_: (0,) * nd, pipeline_mode=pl.Buffered(1))


def _rms(x, g):
    ms = jnp.mean(x * x, axis=-1, keepdims=True)
    return x * lax.rsqrt(ms + RMS_EPS) * g


def _layer_norm(x, g, b):
    mu = jnp.mean(x, axis=-1, keepdims=True)
    xc = x - mu
    var = jnp.mean(xc * xc, axis=-1, keepdims=True)
    return xc * lax.rsqrt(var + LN_EPS) * g + b


def _dot(a, b):
    return jnp.dot(a, b, preferred_element_type=F32)


def _dot_nt(a, b):
    return lax.dot_general(a, b, (((1,), (1,)), ((), ())), preferred_element_type=F32)


def _rope(t, cos, sin_lo, sin_hi):
    up = pltpu.roll(t, LANES - ROT_DIM // 2, 1)
    down = pltpu.roll(t, ROT_DIM // 2, 1)
    return t * cos + up * sin_lo + down * sin_hi


def _rope_tables(seq):
    half = ROT_DIM // 2
    inv_freq = ROPE_THETA ** (-np.arange(half, dtype=np.float64) * (2.0 / ROT_DIM))
    ang = np.arange(seq, dtype=np.float64)[:, None] * inv_freq[None, :]
    cos, sin = np.cos(ang), np.sin(ang)
    ones = np.ones((seq, HEAD_DIM - ROT_DIM))
    zeros = np.zeros((seq, HEAD_DIM - ROT_DIM))
    zh = np.zeros((seq, half))
    cos_h = np.concatenate([cos, cos, ones], axis=1)
    lo_h = np.concatenate([-sin, zh, zeros], axis=1)
    hi_h = np.concatenate([zh, sin, zeros], axis=1)
    two = lambda t: jnp.asarray(np.concatenate([t, t], axis=1), F32)
    return two(cos_h), two(lo_h), two(hi_h)


def _ev_front_kernel(x_ref, g_ref, w_ref, cos_ref, lo_ref, hi_ref,
                     q_ref, k_ref, v_ref, c_ref):
    hn = _rms(x_ref[...], g_ref[...]).astype(BF16)
    cos, lo, hi = cos_ref[...], lo_ref[...], hi_ref[...]
    qkv = _dot(hn, w_ref[:, 0:ATT_W + 2 * LANES])
    for j in range(ATT_W // LANES):
        t = qkv[:, j * LANES:(j + 1) * LANES]
        q_ref[:, j * LANES:(j + 1) * LANES] = (
            _rope(t, cos, lo, hi) * (HEAD_DIM ** -0.5)).astype(BF16)
    k_ref[...] = _rope(qkv[:, ATT_W:ATT_W + LANES], cos, lo, hi).astype(BF16)
    v_ref[...] = qkv[:, ATT_W + LANES:ATT_W + 2 * LANES].astype(BF16)
    base = ATT_W + 2 * LANES
    ga = _dot(hn, w_ref[:, base:base + CONV_CH])
    gb = _dot(hn, w_ref[:, base + CONV_CH:base + 2 * CONV_CH])
    c_ref[...] = ga * jax.nn.sigmoid(gb)


def _od_front_kernel(x_ref, g_ref, w_ref, cos_ref, lo_ref, hi_ref, lng_ref, lnb_ref,
                     qkv1_ref, qkv4_ref, qkv16_ref, u_ref, gate_ref, stage, stage4):
    tile = x_ref.shape[0]
    d4, d16 = DILATIONS[1], DILATIONS[2]
    hn = _rms(x_ref[...], g_ref[...]).astype(BF16)
    cos, lo, hi = cos_ref[...], lo_ref[...], hi_ref[...]
    zg = jax.nn.gelu(_dot(hn, w_ref[:, 3 * ATT_W + D_CH:3 * ATT_W + 2 * D_CH]))
    gate_ref[...] = _layer_norm(zg, lng_ref[...], lnb_ref[...]).astype(BF16)
    u_ref[...] = jax.nn.gelu(_dot(hn, w_ref[:, 3 * ATT_W:3 * ATT_W + D_CH]))
    for s in (2, 1, 0):
        t = _dot(hn, w_ref[:, s * ATT_W:(s + 1) * ATT_W])
        for j in range(ATT_W // LANES):
            sl = slice(j * LANES, (j + 1) * LANES)
            buf, buf4 = stage.at[s, j], stage4.at[s, j]
            if s == 0:
                buf[...] = _rope(t[:, sl], cos, lo, hi) * (HEAD_DIM ** -0.5)
            elif s == 1:
                buf[...] = _rope(t[:, sl], cos, lo, hi)
            else:
                buf[...] = t[:, sl]
            qkv1_ref[s, 0, :, sl] = buf[...].astype(BF16)
            for r in range(d4):
                rows4 = slice(r * (tile // d4), (r + 1) * (tile // d4))
                buf4[rows4, :] = buf[pl.ds(r, tile // d4, stride=d4), :]
                qkv4_ref[s, r, :, sl] = buf4[rows4, :].astype(BF16)
            for r in range(d16):
                r4, a = r % d4, r // d4
                qkv16_ref[s, r, :, sl] = buf4[
                    pl.ds(r4 * (tile // d4) + a, tile // d16, stride=d16 // d4), :].astype(BF16)


def _front_common(x2, seq):
    n_tok = x2.shape[0]
    nt = n_tok // TOK_TILE
    per_seq = seq // TOK_TILE
    x_spec = pl.BlockSpec((TOK_TILE, D_MODEL), lambda i: (i, 0))
    tab_spec = pl.BlockSpec((TOK_TILE, LANES), lambda i: (i % per_seq, 0))
    out = lambda w, dt: (jax.ShapeDtypeStruct((n_tok, w), dt),
                         pl.BlockSpec((TOK_TILE, w), lambda i: (i, 0)))
    return nt, x_spec, tab_spec, out


def _ev_front(x2, g, w, tabs, seq):
    nt, x_spec, tab_spec, out = _front_common(x2, seq)
    outs = [out(ATT_W, BF16), out(LANES, BF16), out(LANES, BF16), out(CONV_CH, F32)]
    return pl.pallas_call(
        _ev_front_kernel,
        grid=(nt,),
        in_specs=[x_spec, _const_spec(g.shape), _const_spec(w.shape),
                  tab_spec, tab_spec, tab_spec],
        out_specs=[o[1] for o in outs],
        out_shape=[o[0] for o in outs],
        compiler_params=_cparams(1),
        name="ev_front",
    )(x2, g, w, *tabs)


def _od_front(x2, g, w, tabs, lng, lnb, batch, seq):
    nt, x_spec, tab_spec, out = _front_common(x2, seq)
    per_seq = seq // TOK_TILE

    def folded(dil):
        return (jax.ShapeDtypeStruct((batch, 3, dil, seq // dil, ATT_W), BF16),
                pl.BlockSpec((None, 3, dil, TOK_TILE // dil, ATT_W),
                             lambda i: (i // per_seq, 0, 0, i % per_seq, 0)))

    outs = [folded(d) for d in DILATIONS] + [out(D_CH, F32), out(D_CH, BF16)]
    return pl.pallas_call(
        _od_front_kernel,
        grid=(nt,),
        in_specs=[x_spec, _const_spec(g.shape), _const_spec(w.shape),
                  tab_spec, tab_spec, tab_spec,
                  _const_spec(lng.shape), _const_spec(lnb.shape)],
        out_specs=[o[1] for o in outs],
        out_shape=[o[0] for o in outs],
        scratch_shapes=[pltpu.VMEM((3, ATT_W // LANES, TOK_TILE, LANES), F32)] * 2,
        compiler_params=_cparams(1),
        name="od_front",
    )(x2, g, w, *tabs, lng, lnb)


def _band_masks(min_dist):
    qi = lax.broadcasted_iota(jnp.int32, (BLOCK, 2 * BLOCK), 0)
    kj = lax.broadcasted_iota(jnp.int32, (BLOCK, 2 * BLOCK), 1)
    dist = qi + BLOCK - kj
    band = (dist >= 0) & (dist <= BLOCK - 1 + min_dist)
    return band, kj < BLOCK


def _softmax_rows(s, band, prev_keys, prev_penalty):
    s = jnp.where(band, s, NEG)
    if prev_penalty is not None:
        s = jnp.where(prev_keys, s + prev_penalty, s)
    m = jnp.max(s, axis=-1, keepdims=True)
    p = jnp.exp(s - m)
    l = jnp.sum(p, axis=-1, keepdims=True)
    return p, m, l


def _conv_skew(tile):
    skew = tile // SUBLANES + 1
    assert skew % 2 == 1
    return skew


def _ev_mix_kernel(sink_ref, x_ref, q_ref, k_ref, kh_ref, v_ref, vh_ref, c_ref, ch_ref,
                   convw_ref, convb_ref, lng_ref, lnb_ref, wout_ref, o_ref,
                   kbuf, vbuf, cbuf, cout, mixbuf):
    i = pl.program_id(1)
    tile = q_ref.shape[0]
    n_pairs = ATT_W // LANES

    kbuf[0:BLOCK, :] = kh_ref[...]
    kbuf[BLOCK:, :] = k_ref[...]
    vbuf[0:BLOCK, :] = vh_ref[...]
    vbuf[BLOCK:, :] = v_ref[...]

    row = lax.broadcasted_iota(jnp.int32, (BLOCK, BLOCK), 0)
    col = lax.broadcasted_iota(jnp.int32, (BLOCK, BLOCK), 1)
    own = col <= row
    first_penalty = jnp.where(i == 0, NEG, 0.0).astype(F32)
    low_lanes = lax.broadcasted_iota(jnp.int32, (BLOCK, LANES), 1) < HEAD_DIM

    for jb in range(tile // BLOCK):
        rows = slice(jb * BLOCK, (jb + 1) * BLOCK)
        kk = kbuf[jb * BLOCK:(jb + 2) * BLOCK, :]
        vv = vbuf[jb * BLOCK:(jb + 2) * BLOCK, :]
        parts = []
        for half in range(2):
            keep = low_lanes if half == 0 else jnp.logical_not(low_lanes)
            for p in range(n_pairs):
                qp = q_ref[rows, p * LANES:(p + 1) * LANES]
                parts.append(jnp.where(keep, qp, jnp.zeros_like(qp)))
        s_all = _dot_nt(jnp.concatenate(parts, axis=0), kk)
        probs, inv_l = [], []
        for idx in range(2 * n_pairs):
            s_prev = s_all[idx * BLOCK:(idx + 1) * BLOCK, 0:BLOCK]
            s_own = s_all[idx * BLOCK:(idx + 1) * BLOCK, BLOCK:]
            if jb == 0:
                s_prev = s_prev + first_penalty
            sc = jnp.where(own, s_own, s_prev)
            sink = sink_ref[idx]
            m = jnp.maximum(jnp.max(sc, axis=-1, keepdims=True), sink)
            pr = jnp.exp(sc - m)
            l = jnp.sum(pr, axis=-1, keepdims=True) + jnp.exp(sink - m)
            zero = jnp.zeros_like(pr)
            probs.append(jnp.concatenate([jnp.where(own, zero, pr), jnp.where(own, pr, zero)],
                                         axis=1).astype(BF16))
            inv_l.append(1.0 / l)
        o_all = _dot(jnp.concatenate(probs, axis=0), vv)
        for p in range(n_pairs):
            o_lo = o_all[p * BLOCK:(p + 1) * BLOCK] * inv_l[p]
            o_hi = o_all[(n_pairs + p) * BLOCK:(n_pairs + p + 1) * BLOCK] * inv_l[n_pairs + p]
            mixbuf[rows, p * LANES:(p + 1) * LANES] = jnp.where(low_lanes, o_lo, o_hi).astype(BF16)

    n_lg = CONV_CH // LANES
    skew = _conv_skew(tile)
    lead = CONV_HALO - (CONV_WIDTH - 1)
    for lg in range(n_lg):
        lanes = slice(lg * LANES, (lg + 1) * LANES)
        cbuf[lg, 0:CONV_HALO, :] = jnp.where(i > 0, ch_ref[:, lanes], 0.0)
        cbuf[lg, CONV_HALO:CONV_HALO + tile, :] = c_ref[:, lanes]
        cbuf[lg, CONV_HALO + tile:, :] = jnp.zeros((cbuf.shape[1] - CONV_HALO - tile, LANES), F32)
    skewed = lambda ref, lg, start: ref.at[lg, pl.ds(start, SUBLANES, stride=skew), :]
    one_trip = jnp.minimum(i, 0) + 1

    def conv_lane_group(lg):
        lanes = slice(lg * LANES, (lg + 1) * LANES)
        for t0 in range(0, CONV_WIDTH, CONV_TAPS):
            n_taps = min(CONV_TAPS, CONV_WIDTH - t0)
            w = [jnp.broadcast_to(convw_ref[t0 + tt:t0 + tt + 1, lanes], (SUBLANES, LANES))
                 for tt in range(n_taps)]
            acc = {}
            for o in range(skew + n_taps - 1):
                xs = skewed(cbuf, lg, o + lead + t0)[...]
                for tt in range(n_taps):
                    j = o - tt
                    if not 0 <= j < skew:
                        continue
                    if tt == 0:
                        acc[j] = (jnp.broadcast_to(convb_ref[:, lanes], (SUBLANES, LANES))
                                  if t0 == 0 else skewed(cout, lg, j)[...])
                    acc[j] = acc[j] + xs * w[tt]
                    if tt == n_taps - 1:
                        skewed(cout, lg, j)[...] = acc.pop(j)

    for lg in range(n_lg):
        lax.fori_loop(0, one_trip, lambda _, carry, lg=lg: (conv_lane_group(lg), carry)[1], 0)
    for j in range(skew):
        y = _layer_norm(jnp.concatenate([skewed(cout, lg, j)[...] for lg in range(n_lg)], axis=1),
                        lng_ref[...], lnb_ref[...])
        y = y * jax.nn.sigmoid(y)
        for lg in range(n_lg):
            skewed(cout, lg, j)[...] = y[:, lg * LANES:(lg + 1) * LANES]
    for lg in range(n_lg):
        mixbuf[:, ATT_W + lg * LANES:ATT_W + (lg + 1) * LANES] = cout[lg, 0:tile, :].astype(BF16)

    o_ref[...] = x_ref[...] + _dot(mixbuf[...], wout_ref[...])


def _ev_mix(x2, q, k, v, c, sinks, convw, convb, lng, lnb, wout, batch, seq):
    tile = TOK_TILE
    nt = seq // tile
    tok = lambda w: pl.BlockSpec((tile, w), lambda b, i: (b * nt + i, 0))
    halo = lambda rows, w: pl.BlockSpec(
        (rows, w), lambda b, i: (jnp.maximum((b * nt + i) * (tile // rows) - 1, 0), 0))
    consts = [convw, convb, lng, lnb, wout]
    skew_rows = SUBLANES * _conv_skew(tile)
    return pl.pallas_call(
        _ev_mix_kernel,
        grid=(batch, nt),
        in_specs=[pl.BlockSpec(memory_space=pltpu.SMEM),
                  tok(D_MODEL), tok(ATT_W), tok(LANES), halo(BLOCK, LANES),
                  tok(LANES), halo(BLOCK, LANES), tok(CONV_CH), halo(CONV_HALO, CONV_CH)]
                 + [_const_spec(t.shape) for t in consts],
        out_specs=tok(D_MODEL),
        out_shape=jax.ShapeDtypeStruct(x2.shape, F32),
        scratch_shapes=[pltpu.VMEM((tile + BLOCK, LANES), BF16),
                        pltpu.VMEM((tile + BLOCK, LANES), BF16),
                        pltpu.VMEM((CONV_CH // LANES, CONV_HALO + skew_rows, LANES), F32),
                        pltpu.VMEM((CONV_CH // LANES, skew_rows, LANES), F32),
                        pltpu.VMEM((tile, ATT_W + CONV_CH), BF16)],
        compiler_params=_cparams(2),
        name="ev_mix",
    )(sinks, x2, q, k, k, v, v, c, c, *consts)


def _branch_kernel(q_ref, k_ref, kh_ref, v_ref, vh_ref, o_ref, lse_ref, kbuf, vbuf):
    i = pl.program_id(2)
    tile = q_ref.shape[0]

    kbuf[0:BLOCK, :] = kh_ref[...]
    kbuf[BLOCK:, :] = k_ref[...]
    vbuf[0:BLOCK, :] = vh_ref[...]
    vbuf[BLOCK:, :] = v_ref[...]

    band, prev_keys = _band_masks(1)
    first_penalty = jnp.where(i == 0, NEG, 0.0).astype(F32)
    low_lanes = lax.broadcasted_iota(jnp.int32, (BLOCK, LANES), 1) < HEAD_DIM

    for jb in range(tile // BLOCK):
        rows = slice(jb * BLOCK, (jb + 1) * BLOCK)
        for p in range(ATT_W // LANES):
            lanes = slice(p * LANES, (p + 1) * LANES)
            kk = kbuf[jb * BLOCK:(jb + 2) * BLOCK, lanes]
            vv = vbuf[jb * BLOCK:(jb + 2) * BLOCK, lanes]
            qp = q_ref[rows, lanes]
            zero = jnp.zeros_like(qp)
            q2 = jnp.concatenate([jnp.where(low_lanes, qp, zero),
                                  jnp.where(low_lanes, zero, qp)], axis=0)
            s2 = _dot_nt(q2, kk)
            probs, inv_l, lse = [], [], []
            for half in range(2):
                pr, m, l = _softmax_rows(s2[half * BLOCK:(half + 1) * BLOCK], band, prev_keys,
                                         first_penalty if jb == 0 else None)
                probs.append(pr.astype(BF16))
                inv_l.append(1.0 / l)
                lse.append(m + jnp.log(l))
            o2 = _dot(jnp.concatenate(probs, axis=0), vv)
            o_ref[rows, lanes] = jnp.where(low_lanes, o2[0:BLOCK] * inv_l[0],
                                           o2[BLOCK:] * inv_l[1])
            lse_ref[rows, lanes] = jnp.where(low_lanes, lse[0], lse[1])


def _branch(qkv, batch, seq, dil):
    sub = seq // dil
    tile = min(TOK_TILE, sub)
    nt = sub // tile

    def cur(s):
        return pl.BlockSpec((None, None, None, tile, ATT_W), lambda b, r, i: (b, s, r, i, 0))

    def halo(s):
        return pl.BlockSpec(
            (None, None, None, BLOCK, ATT_W),
            lambda b, r, i: (b, s, r, jnp.maximum(i * (tile // BLOCK) - 1, 0), 0))

    out_spec = pl.BlockSpec((None, None, tile, ATT_W), lambda b, r, i: (b, r, i, 0))
    out_sds = jax.ShapeDtypeStruct((batch, dil, sub, ATT_W), F32)
    return pl.pallas_call(
        _branch_kernel,
        grid=(batch, dil, nt),
        in_specs=[cur(0), cur(1), halo(1), cur(2), halo(2)],
        out_specs=[out_spec, out_spec],
        out_shape=[out_sds, out_sds],
        scratch_shapes=[pltpu.VMEM((tile + BLOCK, ATT_W), BF16),
                        pltpu.VMEM((tile + BLOCK, ATT_W), BF16)],
        compiler_params=_cparams(3),
        name=f"od_branch_d{dil}",
    )(qkv, qkv, qkv, qkv, qkv)


def _od_mix_kernel(x_ref, o1_ref, o2_ref, o3_ref, l1_ref, l2_ref, l3_ref, u_ref, gate_ref,
                   sw_ref, sb_ref, wout_ref, out_ref, mixbuf, tok_o2, tok_l2, tok_o3, tok_l3):
    tile = x_ref.shape[0]
    for j in range(ATT_W // LANES):
        sl = slice(j * LANES, (j + 1) * LANES)
        for src, dst in ((o2_ref, tok_o2), (l2_ref, tok_l2), (o3_ref, tok_o3), (l3_ref, tok_l3)):
            dil = src.shape[0]
            for r in range(dil):
                dst[j, pl.ds(r, tile // dil, stride=dil), :] = src[r, :, sl]
        l1, l2, l3 = l1_ref[0, :, sl], tok_l2[j], tok_l3[j]
        m = jnp.maximum(jnp.maximum(l1, l2), l3)
        e1, e2, e3 = jnp.exp(l1 - m), jnp.exp(l2 - m), jnp.exp(l3 - m)
        mixed = (e1 * o1_ref[0, :, sl] + e2 * tok_o2[j] + e3 * tok_o3[j]) / (e1 + e2 + e3)
        mixbuf[:, sl] = mixed.astype(BF16)

    ti = lax.broadcasted_iota(jnp.int32, (CHUNK, CHUNK), 0)
    si = lax.broadcasted_iota(jnp.int32, (CHUNK, CHUNK), 1)
    causal = ti >= si
    low_lanes = lax.broadcasted_iota(jnp.int32, (CHUNK, LANES), 1) < HEAD_DIM
    for p in range(D_CH // LANES):
        lanes = slice(p * LANES, (p + 1) * LANES)
        w2 = jnp.concatenate(
            [jnp.where(causal, sw_ref[2 * p], 0.0), jnp.where(causal, sw_ref[2 * p + 1], 0.0)],
            axis=0).astype(BF16)
        bias = sb_ref[:, lanes]
        for ch in range(tile // CHUNK):
            rows = slice(ch * CHUNK, (ch + 1) * CHUNK)
            y2 = _dot(w2, gate_ref[rows, lanes])
            y = jnp.where(low_lanes, y2[0:CHUNK], y2[CHUNK:]) + bias
            mixbuf[rows, ATT_W + p * LANES:ATT_W + (p + 1) * LANES] = (
                u_ref[rows, lanes] * y).astype(BF16)

    out_ref[...] = x_ref[...] + _dot(mixbuf[...], wout_ref[...])


def _od_mix(x2, outs, lses, u, gate, sw, sb, wout, seq):
    tile = TOK_TILE
    per_seq = seq // tile
    tok = lambda w: pl.BlockSpec((tile, w), lambda i: (i, 0))
    folded = [pl.BlockSpec((None, d, tile // d, ATT_W),
                           lambda i: (i // per_seq, 0, i % per_seq, 0)) for d in DILATIONS]
    consts = [sw, sb, wout]
    return pl.pallas_call(
        _od_mix_kernel,
        grid=(x2.shape[0] // tile,),
        in_specs=[tok(D_MODEL)] + folded + folded + [tok(D_CH), tok(D_CH)]
                 + [_const_spec(t.shape) for t in consts],
        out_specs=tok(D_MODEL),
        out_shape=jax.ShapeDtypeStruct(x2.shape, F32),
        scratch_shapes=[pltpu.VMEM((tile, ATT_W + D_CH), BF16)]
                       + [pltpu.VMEM((ATT_W // LANES, tile, LANES), F32)] * 4,
        compiler_params=_cparams(1),
        name="od_mix",
    )(x2, *outs, *lses, u, gate, *consts)


def _ffn_kernel(x_ref, g_ref, wg_ref, wu_ref, wd_ref, fg_ref, o_ref, actbuf, *, final_norm):
    hn = _rms(x_ref[...], g_ref[...]).astype(BF16)
    for c in range(D_FF // FF_CHUNK):
        cols = slice(c * FF_CHUNK, (c + 1) * FF_CHUNK)
        gate = _dot(hn, wg_ref[:, cols])
        up = _dot(hn, wu_ref[:, cols])
        actbuf[:, cols] = (gate * jax.nn.sigmoid(gate) * up).astype(BF16)
    y = x_ref[...] + _dot(actbuf[...], wd_ref[...])
    if final_norm:
        y = _rms(y, fg_ref[...])
    o_ref[...] = y


def _ffn(x2, g, wg, wu, wd, fg, final_norm):
    tile = TOK_TILE
    tok = pl.BlockSpec((tile, D_MODEL), lambda i: (i, 0))
    consts = [g, wg, wu, wd, fg]
    return pl.pallas_call(
        functools.partial(_ffn_kernel, final_norm=final_norm),
        grid=(x2.shape[0] // tile,),
        in_specs=[tok] + [_const_spec(t.shape) for t in consts],
        out_specs=tok,
        out_shape=jax.ShapeDtypeStruct(x2.shape, F32),
        scratch_shapes=[pltpu.VMEM((tile, D_FF), BF16)],
        compiler_params=_cparams(1),
        name="ffn_final" if final_norm else "ffn",
    )(x2, *consts)


def kernel(x, ev_norm_g, ev_w_in, ev_sinks, ev_conv_w, ev_conv_b, ev_conv_ln_g, ev_conv_ln_b, ev_w_out, od_norm_g, od_w_in, od_sgu_ln_g, od_sgu_ln_b, od_spatial_w, od_spatial_b, od_w_out, ffn_norm_g, ffn_w_gate, ffn_w_up, ffn_w_down, final_norm_g):
    batch, seq, _ = x.shape
    assert seq % (max(DILATIONS) * BLOCK) == 0 and seq % TOK_TILE == 0
    row = lambda t: t.reshape(1, -1).astype(F32)
    x2 = x.reshape(batch * seq, D_MODEL)
    tabs = _rope_tables(seq)

    n_pairs = ATT_W // LANES
    head_order = [h for p in range(n_pairs) for h in (p, n_pairs + p)]
    q_cols = [h * HEAD_DIM + d for h in head_order for d in range(HEAD_DIM)]
    in_cols = np.asarray(q_cols + list(range(ATT_W, ev_w_in.shape[2])), np.int32)
    out_rows = np.asarray(q_cols + list(range(ATT_W, ev_w_out.shape[1])), np.int32)
    w_in0 = jnp.take(ev_w_in[0], in_cols, axis=1).astype(BF16)
    w_out0 = jnp.take(ev_w_out[0], out_rows, axis=0).astype(BF16)
    q, k, v, c = _ev_front(x2, row(ev_norm_g[0]), w_in0, tabs, seq)
    ffn_w = lambda l: (row(ffn_norm_g[l]), ffn_w_gate[l].astype(BF16), ffn_w_up[l].astype(BF16),
                       ffn_w_down[l].astype(BF16), row(final_norm_g))
    h = _ev_mix(x2, q, k, v, c, ev_sinks[0].astype(F32), ev_conv_w[0].astype(F32),
                row(ev_conv_b[0]), row(ev_conv_ln_g[0]), row(ev_conv_ln_b[0]), w_out0,
                batch, seq)
    h = _ffn(h, *ffn_w(0), final_norm=False)

    *qkvs, u, gate = _od_front(h, row(od_norm_g[0]), od_w_in[0].astype(BF16), tabs,
                               row(od_sgu_ln_g[0]), row(od_sgu_ln_b[0]), batch, seq)
    outs, lses = [], []
    for dil, qkv in zip(DILATIONS, qkvs):
        o_r, lse_r = _branch(qkv, batch, seq, dil)
        outs.append(o_r)
        lses.append(lse_r)
    sb = jnp.repeat(od_spatial_b[0].astype(F32).T, HEAD_DIM, axis=1)
    h = _od_mix(h, outs, lses, u, gate, od_spatial_w[0].astype(F32), sb,
                od_w_out[0].astype(BF16), seq)
    h = _ffn(h, *ffn_w(1), final_norm=True)
    return h.reshape(batch, seq, D_MODEL)
```

```python
import jax
import jax.numpy as jnp
import numpy as np
from jax import lax
from jax.experimental import pallas as pl
from jax.experimental.pallas import tpu as pltpu

F32 = jnp.float32
BF16 = jnp.bfloat16

D_MODEL = 1024
HEAD_DIM = 64
ROT_DIM = 16
ROPE_THETA = 500000.0
BLOCK = 128
RMS_EPS = 1e-6
LN_EPS = 1e-5
N_HEADS = 8
ATT_W = N_HEADS * HEAD_DIM
CONV_CH = 512
CONV_WIDTH = 31
CONV_HALO = 32
D_CH = 512
D_GROUPS = 8
CHUNK = 128
D_FF = 2816
DILATIONS = (1, 4, 16)
LANES = 128
SUBLANES = 8
NEG = -1e30

TOK_TILE = 512
FF_CHUNK = 256
CONV_TAPS = 16
CAST_SPLIT = 2
VMEM_LIMIT = 56 * 1024 * 1024


def _cparams(n_axes):
    return pltpu.CompilerParams(
        dimension_semantics=("arbitrary",) * n_axes,
        vmem_limit_bytes=VMEM_LIMIT)


def _const_spec(shape):
    nd = len(shape)
    return pl.BlockSpec(shape, lambda *_: (0,) * nd, pipeline_mode=pl.Buffered(1))


def _rms(x, g):
    ms = jnp.mean(x * x, axis=-1, keepdims=True)
    return x * lax.rsqrt(ms + RMS_EPS) * g


def _layer_norm(x, g, b):
    mu = jnp.mean(x, axis=-1, keepdims=True)
    xc = x - mu
    var = jnp.mean(xc * xc, axis=-1, keepdims=True)
    return xc * lax.rsqrt(var + LN_EPS) * g + b


def _dot(a, b):
    return jnp.dot(a, b, preferred_element_type=F32)


def _dot_nt(a, b):
    return lax.dot_general(a, b, (((1,), (1,)), ((), ())), preferred_element_type=F32)


def _rope(t, cos, sin_lo, sin_hi):
    up = pltpu.roll(t, LANES - ROT_DIM // 2, 1)
    down = pltpu.roll(t, ROT_DIM // 2, 1)
    return t * cos + up * sin_lo + down * sin_hi


def _rope_tables(seq):
    half = ROT_DIM // 2
    inv_freq = ROPE_THETA ** (-np.arange(half, dtype=np.float64) * (2.0 / ROT_DIM))
    ang = np.arange(seq, dtype=np.float64)[:, None] * inv_freq[None, :]
    cos, sin = np.cos(ang), np.sin(ang)
    ones = np.ones((seq, HEAD_DIM - ROT_DIM))
    zeros = np.zeros((seq, HEAD_DIM - ROT_DIM))
    zh = np.zeros((seq, half))
    cos_h = np.concatenate([cos, cos, ones], axis=1)
    lo_h = np.concatenate([-sin, zh, zeros], axis=1)
    hi_h = np.concatenate([zh, sin, zeros], axis=1)
    two = lambda t: jnp.asarray(np.concatenate([t, t], axis=1), F32)
    return two(cos_h), two(lo_h), two(hi_h)


def _ev_front_kernel(x_ref, g_ref, w_ref, cos_ref, lo_ref, hi_ref,
                     q_ref, k_ref, v_ref, c_ref):
    hn = _rms(x_ref[...], g_ref[...]).astype(BF16)
    cos, lo, hi = cos_ref[...], lo_ref[...], hi_ref[...]
    qkv = _dot(hn, w_ref[:, 0:ATT_W + 2 * LANES])
    for j in range(ATT_W // LANES):
        t = qkv[:, j * LANES:(j + 1) * LANES]
        q_ref[:, j * LANES:(j + 1) * LANES] = (
            _rope(t, cos, lo, hi) * (HEAD_DIM ** -0.5)).astype(BF16)
    k_ref[...] = _rope(qkv[:, ATT_W:ATT_W + LANES], cos, lo, hi).astype(BF16)
    v_ref[...] = qkv[:, ATT_W + LANES:ATT_W + 2 * LANES].astype(BF16)
    base = ATT_W + 2 * LANES
    ga = _dot(hn, w_ref[:, base:base + CONV_CH])
    gb = _dot(hn, w_ref[:, base + CONV_CH:base + 2 * CONV_CH])
    c_ref[...] = ga * jax.nn.sigmoid(gb)


def _od_front_kernel(x_ref, g_ref, w_ref, cos_ref, lo_ref, hi_ref, lng_ref, lnb_ref,
                     qkv1_ref, qkv4_ref, qkv16_ref, u_ref, gate_ref, stage, stage4):
    tile = x_ref.shape[0]
    d4, d16 = DILATIONS[1], DILATIONS[2]
    hn = _rms(x_ref[...], g_ref[...]).astype(BF16)
    cos, lo, hi = cos_ref[...], lo_ref[...], hi_ref[...]
    zg = jax.nn.gelu(_dot(hn, w_ref[:, 3 * ATT_W + D_CH:3 * ATT_W + 2 * D_CH]))
    gate_ref[...] = _layer_norm(zg, lng_ref[...], lnb_ref[...]).astype(BF16)
    u_ref[...] = jax.nn.gelu(_dot(hn, w_ref[:, 3 * ATT_W:3 * ATT_W + D_CH]))
    for s in (2, 1, 0):
        t = _dot(hn, w_ref[:, s * ATT_W:(s + 1) * ATT_W])
        for j in range(ATT_W // LANES):
            sl = slice(j * LANES, (j + 1) * LANES)
            buf, buf4 = stage.at[s, j], stage4.at[s, j]
            if s == 0:
                buf[...] = _rope(t[:, sl], cos, lo, hi) * (HEAD_DIM ** -0.5)
            elif s == 1:
                buf[...] = _rope(t[:, sl], cos, lo, hi)
            else:
                buf[...] = t[:, sl]
            qkv1_ref[s, 0, :, sl] = buf[...].astype(BF16)
            for r in range(d4):
                rows4 = slice(r * (tile // d4), (r + 1) * (tile // d4))
                buf4[rows4, :] = buf[pl.ds(r, tile // d4, stride=d4), :]
                qkv4_ref[s, r, :, sl] = buf4[rows4, :].astype(BF16)
            for r in range(d16):
                r4, a = r % d4, r // d4
                qkv16_ref[s, r, :, sl] = buf4[
                    pl.ds(r4 * (tile // d4) + a, tile // d16, stride=d16 // d4), :].astype(BF16)


def _front_common(x2, seq):
    n_tok = x2.shape[0]
    nt = n_tok // TOK_TILE
    per_seq = seq // TOK_TILE
    x_spec = pl.BlockSpec((TOK_TILE, D_MODEL), lambda i: (i, 0))
    tab_spec = pl.BlockSpec((TOK_TILE, LANES), lambda i: (i % per_seq, 0))
    out = lambda w, dt: (jax.ShapeDtypeStruct((n_tok, w), dt),
                         pl.BlockSpec((TOK_TILE, w), lambda i: (i, 0)))
    return nt, x_spec, tab_spec, out


def _ev_front(x2, g, w, tabs, seq):
    nt, x_spec, tab_spec, out = _front_common(x2, seq)
    outs = [out(ATT_W, BF16), out(LANES, BF16), out(LANES, BF16), out(CONV_CH, F32)]
    return pl.pallas_call(
        _ev_front_kernel,
        grid=(nt,),
        in_specs=[x_spec, _const_spec(g.shape), _const_spec(w.shape),
                  tab_spec, tab_spec, tab_spec],
        out_specs=[o[1] for o in outs],
        out_shape=[o[0] for o in outs],
        compiler_params=_cparams(1),
        name="ev_front",
    )(x2, g, w, *tabs)


def _od_front(x2, g, w, tabs, lng, lnb, batch, seq):
    nt, x_spec, tab_spec, out = _front_common(x2, seq)
    per_seq = seq // TOK_TILE

    def folded(dil):
        return (jax.ShapeDtypeStruct((batch, 3, dil, seq // dil, ATT_W), BF16),
                pl.BlockSpec((None, 3, dil, TOK_TILE // dil, ATT_W),
                             lambda i: (i // per_seq, 0, 0, i % per_seq, 0)))

    outs = [folded(d) for d in DILATIONS] + [out(D_CH, F32), out(D_CH, BF16)]
    return pl.pallas_call(
        _od_front_kernel,
        grid=(nt,),
        in_specs=[x_spec, _const_spec(g.shape), _const_spec(w.shape),
                  tab_spec, tab_spec, tab_spec,
                  _const_spec(lng.shape), _const_spec(lnb.shape)],
        out_specs=[o[1] for o in outs],
        out_shape=[o[0] for o in outs],
        scratch_shapes=[pltpu.VMEM((3, ATT_W // LANES, TOK_TILE, LANES), F32)] * 2,
        compiler_params=_cparams(1),
        name="od_front",
    )(x2, g, w, *tabs, lng, lnb)


def _band_masks(min_dist):
    qi = lax.broadcasted_iota(jnp.int32, (BLOCK, 2 * BLOCK), 0)
    kj = lax.broadcasted_iota(jnp.int32, (BLOCK, 2 * BLOCK), 1)
    dist = qi + BLOCK - kj
    band = (dist >= 0) & (dist <= BLOCK - 1 + min_dist)
    return band, kj < BLOCK


def _softmax_rows(s, band, prev_keys, prev_penalty):
    s = jnp.where(band, s, NEG)
    if prev_penalty is not None:
        s = jnp.where(prev_keys, s + prev_penalty, s)
    m = jnp.max(s, axis=-1, keepdims=True)
    p = jnp.exp(s - m)
    l = jnp.sum(p, axis=-1, keepdims=True)
    return p, m, l


def _conv_skew(tile):
    skew = tile // SUBLANES + 1
    assert skew % 2 == 1
    return skew


def _ev_mix_kernel(sink_ref, x_ref, q_ref, k_ref, kh_ref, v_ref, vh_ref, c_ref, ch_ref,
                   convw_ref, convb_ref, lng_ref, lnb_ref, wout_ref, o_ref,
                   kbuf, vbuf, cbuf, cout, mixbuf):
    i = pl.program_id(1)
    tile = q_ref.shape[0]
    n_pairs = ATT_W // LANES

    kbuf[0:BLOCK, :] = kh_ref[...]
    kbuf[BLOCK:, :] = k_ref[...]
    vbuf[0:BLOCK, :] = vh_ref[...]
    vbuf[BLOCK:, :] = v_ref[...]

    row = lax.broadcasted_iota(jnp.int32, (BLOCK, BLOCK), 0)
    col = lax.broadcasted_iota(jnp.int32, (BLOCK, BLOCK), 1)
    own = col <= row
    first_penalty = jnp.where(i == 0, NEG, 0.0).astype(F32)
    low_lanes = lax.broadcasted_iota(jnp.int32, (BLOCK, LANES), 1) < HEAD_DIM

    for jb in range(tile // BLOCK):
        rows = slice(jb * BLOCK, (jb + 1) * BLOCK)
        kk = kbuf[jb * BLOCK:(jb + 2) * BLOCK, :]
        vv = vbuf[jb * BLOCK:(jb + 2) * BLOCK, :]
        parts = []
        for half in range(2):
            keep = low_lanes if half == 0 else jnp.logical_not(low_lanes)
            for p in range(n_pairs):
                qp = q_ref[rows, p * LANES:(p + 1) * LANES]
                parts.append(jnp.where(keep, qp, jnp.zeros_like(qp)))
        s_all = _dot_nt(jnp.concatenate(parts, axis=0), kk)
        probs, inv_l = [], []
        for idx in range(2 * n_pairs):
            s_prev = s_all[idx * BLOCK:(idx + 1) * BLOCK, 0:BLOCK]
            s_own = s_all[idx * BLOCK:(idx + 1) * BLOCK, BLOCK:]
            if jb == 0:
                s_prev = s_prev + first_penalty
            sc = jnp.where(own, s_own, s_prev)
            sink = sink_ref[idx]
            m = jnp.maximum(jnp.max(sc, axis=-1, keepdims=True), sink)
            pr = jnp.exp(sc - m)
            l = jnp.sum(pr, axis=-1, keepdims=True) + jnp.exp(sink - m)
            zero = jnp.zeros_like(pr)
            probs.append(jnp.concatenate([jnp.where(own, zero, pr), jnp.where(own, pr, zero)],
                                         axis=1).astype(BF16))
            inv_l.append(1.0 / l)
        o_all = _dot(jnp.concatenate(probs, axis=0), vv)
        for p in range(n_pairs):
            o_lo = o_all[p * BLOCK:(p + 1) * BLOCK] * inv_l[p]
            o_hi = o_all[(n_pairs + p) * BLOCK:(n_pairs + p + 1) * BLOCK] * inv_l[n_pairs + p]
            mixbuf[rows, p * LANES:(p + 1) * LANES] = jnp.where(low_lanes, o_lo, o_hi).astype(BF16)

    n_lg = CONV_CH // LANES
    skew = _conv_skew(tile)
    lead = CONV_HALO - (CONV_WIDTH - 1)
    for lg in range(n_lg):
        lanes = slice(lg * LANES, (lg + 1) * LANES)
        cbuf[lg, 0:CONV_HALO, :] = jnp.where(i > 0, ch_ref[:, lanes], 0.0)
        cbuf[lg, CONV_HALO:CONV_HALO + tile, :] = c_ref[:, lanes]
        cbuf[lg, CONV_HALO + tile:, :] = jnp.zeros((cbuf.shape[1] - CONV_HALO - tile, LANES), F32)
    skewed = lambda ref, lg, start: ref.at[lg, pl.ds(start, SUBLANES, stride=skew), :]
    one_trip = jnp.minimum(i, 0) + 1

    def conv_lane_group(lg):
        lanes = slice(lg * LANES, (lg + 1) * LANES)
        for t0 in range(0, CONV_WIDTH, CONV_TAPS):
            n_taps = min(CONV_TAPS, CONV_WIDTH - t0)
            w = [jnp.broadcast_to(convw_ref[t0 + tt:t0 + tt + 1, lanes], (SUBLANES, LANES))
                 for tt in range(n_taps)]
            acc = {}
            for o in range(skew + n_taps - 1):
                xs = skewed(cbuf, lg, o + lead + t0)[...]
                for tt in range(n_taps):
                    j = o - tt
                    if not 0 <= j < skew:
                        continue
                    if tt == 0:
                        acc[j] = (jnp.broadcast_to(convb_ref[:, lanes], (SUBLANES, LANES))
                                  if t0 == 0 else skewed(cout, lg, j)[...])
                    acc[j] = acc[j] + xs * w[tt]
                    if tt == n_taps - 1:
                        skewed(cout, lg, j)[...] = acc.pop(j)

    for lg in range(n_lg):
        lax.fori_loop(0, one_trip, lambda _, carry, lg=lg: (conv_lane_group(lg), carry)[1], 0)
    for j in range(skew):
        y = _layer_norm(jnp.concatenate([skewed(cout, lg, j)[...] for lg in range(n_lg)], axis=1),
                        lng_ref[...], lnb_ref[...])
        y = y * jax.nn.sigmoid(y)
        for lg in range(n_lg):
            skewed(cout, lg, j)[...] = y[:, lg * LANES:(lg + 1) * LANES]
    for lg in range(n_lg):
        mixbuf[:, ATT_W + lg * LANES:ATT_W + (lg + 1) * LANES] = cout[lg, 0:tile, :].astype(BF16)

    o_ref[...] = x_ref[...] + _dot(mixbuf[...], wout_ref[...])


def _ev_tail_kernel(sink_ref, x_ref, q_ref, k_ref, kh_ref, v_ref, vh_ref, c_ref, ch_ref,
                    convw_ref, convb_ref, lng_ref, lnb_ref, wout_ref,
                    fng_ref, wg_ref, wu_ref, wd_ref, fg_ref, o_ref,
                    kbuf, vbuf, cbuf, cout, mixbuf, actbuf, h_mid):
    _ev_mix_kernel(sink_ref, x_ref, q_ref, k_ref, kh_ref, v_ref, vh_ref, c_ref, ch_ref,
                   convw_ref, convb_ref, lng_ref, lnb_ref, wout_ref, h_mid,
                   kbuf, vbuf, cbuf, cout, mixbuf)
    _ffn_kernel(h_mid, fng_ref, wg_ref, wu_ref, wd_ref, fg_ref, o_ref, actbuf, final_norm=False)


def _ev_tail(x2, q, k, v, c, sinks, convw, convb, lng, lnb, wout, fng, wg, wu, wd, fg,
             batch, seq):
    tile = TOK_TILE
    nt = seq // tile
    tok = lambda w: pl.BlockSpec((tile, w), lambda b, i: (b * nt + i, 0))
    halo = lambda rows, w: pl.BlockSpec(
        (rows, w), lambda b, i: (jnp.maximum((b * nt + i) * (tile // rows) - 1, 0), 0))
    consts = [convw, convb, lng, lnb, wout, fng, wg, wu, wd, fg]
    skew_rows = SUBLANES * _conv_skew(tile)
    return pl.pallas_call(
        _ev_tail_kernel,
        grid=(batch, nt),
        in_specs=[pl.BlockSpec(memory_space=pltpu.SMEM),
                  tok(D_MODEL), tok(ATT_W), tok(LANES), halo(BLOCK, LANES),
                  tok(LANES), halo(BLOCK, LANES), tok(CONV_CH), halo(CONV_HALO, CONV_CH)]
                 + [_const_spec(t.shape) for t in consts],
        out_specs=tok(D_MODEL),
        out_shape=jax.ShapeDtypeStruct(x2.shape, F32),
        scratch_shapes=[pltpu.VMEM((tile + BLOCK, LANES), BF16),
                        pltpu.VMEM((tile + BLOCK, LANES), BF16),
                        pltpu.VMEM((CONV_CH // LANES, CONV_HALO + skew_rows, LANES), F32),
                        pltpu.VMEM((CONV_CH // LANES, skew_rows, LANES), F32),
                        pltpu.VMEM((tile, ATT_W + CONV_CH), BF16),
                        pltpu.VMEM((tile, D_FF), BF16),
                        pltpu.VMEM((tile, D_MODEL), F32)],
        compiler_params=_cparams(2),
        name="ev_tail",
    )(sinks, x2, q, k, k, v, v, c, c, *consts)


def _branch_kernel(q_ref, k_ref, kh_ref, v_ref, vh_ref, o_ref, lse_ref, kbuf, vbuf):
    i = pl.program_id(2)
    tile = q_ref.shape[0]

    kbuf[0:BLOCK, :] = kh_ref[...]
    kbuf[BLOCK:, :] = k_ref[...]
    vbuf[0:BLOCK, :] = vh_ref[...]
    vbuf[BLOCK:, :] = v_ref[...]

    band, prev_keys = _band_masks(1)
    first_penalty = jnp.where(i == 0, NEG, 0.0).astype(F32)
    low_lanes = lax.broadcasted_iota(jnp.int32, (BLOCK, LANES), 1) < HEAD_DIM

    for jb in range(tile // BLOCK):
        rows = slice(jb * BLOCK, (jb + 1) * BLOCK)
        for p in range(ATT_W // LANES):
            lanes = slice(p * LANES, (p + 1) * LANES)
            kk = kbuf[jb * BLOCK:(jb + 2) * BLOCK, lanes]
            vv = vbuf[jb * BLOCK:(jb + 2) * BLOCK, lanes]
            qp = q_ref[rows, lanes]
            zero = jnp.zeros_like(qp)
            q2 = jnp.concatenate([jnp.where(low_lanes, qp, zero),
                                  jnp.where(low_lanes, zero, qp)], axis=0)
            s2 = _dot_nt(q2, kk)
            probs, inv_l, lse = [], [], []
            for half in range(2):
                pr, m, l = _softmax_rows(s2[half * BLOCK:(half + 1) * BLOCK], band, prev_keys,
                                         first_penalty if jb == 0 else None)
                probs.append(pr.astype(BF16))
                inv_l.append(1.0 / l)
                lse.append(m + jnp.log(l))
            o2 = _dot(jnp.concatenate(probs, axis=0), vv)
            o_ref[rows, lanes] = jnp.where(low_lanes, o2[0:BLOCK] * inv_l[0],
                                           o2[BLOCK:] * inv_l[1])
            lse_ref[rows, lanes] = jnp.where(low_lanes, lse[0], lse[1])


def _branch(qkv, batch, seq, dil):
    sub = seq // dil
    tile = min(TOK_TILE, sub)
    nt = sub // tile

    def cur(s):
        return pl.BlockSpec((None, None, None, tile, ATT_W), lambda b, r, i: (b, s, r, i, 0))

    def halo(s):
        return pl.BlockSpec(
            (None, None, None, BLOCK, ATT_W),
            lambda b, r, i: (b, s, r, jnp.maximum(i * (tile // BLOCK) - 1, 0), 0))

    out_spec = pl.BlockSpec((None, None, tile, ATT_W), lambda b, r, i: (b, r, i, 0))
    out_sds = jax.ShapeDtypeStruct((batch, dil, sub, ATT_W), F32)
    return pl.pallas_call(
        _branch_kernel,
        grid=(batch, dil, nt),
        in_specs=[cur(0), cur(1), halo(1), cur(2), halo(2)],
        out_specs=[out_spec, out_spec],
        out_shape=[out_sds, out_sds],
        scratch_shapes=[pltpu.VMEM((tile + BLOCK, ATT_W), BF16),
                        pltpu.VMEM((tile + BLOCK, ATT_W), BF16)],
        compiler_params=_cparams(3),
        name=f"od_branch_d{dil}",
    )(qkv, qkv, qkv, qkv, qkv)


def _od_mix_kernel(x_ref, o1_ref, o2_ref, o3_ref, l1_ref, l2_ref, l3_ref, u_ref, gate_ref,
                   sw_ref, sb_ref, wout_ref, out_ref, mixbuf, tok_o2, tok_l2, tok_o3, tok_l3):
    tile = x_ref.shape[0]
    for j in range(ATT_W // LANES):
        sl = slice(j * LANES, (j + 1) * LANES)
        for src, dst in ((o2_ref, tok_o2), (l2_ref, tok_l2), (o3_ref, tok_o3), (l3_ref, tok_l3)):
            dil = src.shape[0]
            for r in range(dil):
                dst[j, pl.ds(r, tile // dil, stride=dil), :] = src[r, :, sl]
        l1, l2, l3 = l1_ref[0, :, sl], tok_l2[j], tok_l3[j]
        m = jnp.maximum(jnp.maximum(l1, l2), l3)
        e1, e2, e3 = jnp.exp(l1 - m), jnp.exp(l2 - m), jnp.exp(l3 - m)
        mixed = (e1 * o1_ref[0, :, sl] + e2 * tok_o2[j] + e3 * tok_o3[j]) / (e1 + e2 + e3)
        mixbuf[:, sl] = mixed.astype(BF16)

    ti = lax.broadcasted_iota(jnp.int32, (CHUNK, CHUNK), 0)
    si = lax.broadcasted_iota(jnp.int32, (CHUNK, CHUNK), 1)
    causal = ti >= si
    low_lanes = lax.broadcasted_iota(jnp.int32, (CHUNK, LANES), 1) < HEAD_DIM
    for p in range(D_CH // LANES):
        lanes = slice(p * LANES, (p + 1) * LANES)
        w2 = jnp.concatenate(
            [jnp.where(causal, sw_ref[2 * p], 0.0), jnp.where(causal, sw_ref[2 * p + 1], 0.0)],
            axis=0).astype(BF16)
        bias = sb_ref[:, lanes]
        for ch in range(tile // CHUNK):
            rows = slice(ch * CHUNK, (ch + 1) * CHUNK)
            y2 = _dot(w2, gate_ref[rows, lanes])
            y = jnp.where(low_lanes, y2[0:CHUNK], y2[CHUNK:]) + bias
            mixbuf[rows, ATT_W + p * LANES:ATT_W + (p + 1) * LANES] = (
                u_ref[rows, lanes] * y).astype(BF16)

    out_ref[...] = x_ref[...] + _dot(mixbuf[...], wout_ref[...])


def _od_tail_kernel(x_ref, o1_ref, o2_ref, o3_ref, l1_ref, l2_ref, l3_ref, u_ref, gate_ref,
                    sw_ref, sb_ref, wout_ref, fng_ref, wg_ref, wu_ref, wd_ref, fg_ref, out_ref,
                    mixbuf, tok_o2, tok_l2, tok_o3, tok_l3, actbuf, h_mid):
    _od_mix_kernel(x_ref, o1_ref, o2_ref, o3_ref, l1_ref, l2_ref, l3_ref, u_ref, gate_ref,
                   sw_ref, sb_ref, wout_ref, h_mid, mixbuf, tok_o2, tok_l2, tok_o3, tok_l3)
    _ffn_kernel(h_mid, fng_ref, wg_ref, wu_ref, wd_ref, fg_ref, out_ref, actbuf, final_norm=True)


def _od_tail(x2, outs, lses, u, gate, sw, sb, wout, fng, wg, wu, wd, fg, seq):
    tile = TOK_TILE
    per_seq = seq // tile
    tok = lambda w: pl.BlockSpec((tile, w), lambda i: (i, 0))
    folded = [pl.BlockSpec((None, d, tile // d, ATT_W),
                           lambda i: (i // per_seq, 0, i % per_seq, 0)) for d in DILATIONS]
    consts = [sw, sb, wout, fng, wg, wu, wd, fg]
    return pl.pallas_call(
        _od_tail_kernel,
        grid=(x2.shape[0] // tile,),
        in_specs=[tok(D_MODEL)] + folded + folded + [tok(D_CH), tok(D_CH)]
                 + [_const_spec(t.shape) for t in consts],
        out_specs=tok(D_MODEL),
        out_shape=jax.ShapeDtypeStruct(x2.shape, F32),
        scratch_shapes=[pltpu.VMEM((tile, ATT_W + D_CH), BF16)]
                       + [pltpu.VMEM((ATT_W // LANES, tile, LANES), F32)] * 4
                       + [pltpu.VMEM((tile, D_FF), BF16), pltpu.VMEM((tile, D_MODEL), F32)],
        compiler_params=_cparams(1),
        name="od_tail",
    )(x2, *outs, *lses, u, gate, *consts)


def _ffn_kernel(x_ref, g_ref, wg_ref, wu_ref, wd_ref, fg_ref, o_ref, actbuf, *, final_norm):
    hn = _rms(x_ref[...], g_ref[...]).astype(BF16)
    for c in range(D_FF // FF_CHUNK):
        cols = slice(c * FF_CHUNK, (c + 1) * FF_CHUNK)
        gate = _dot(hn, wg_ref[:, cols])
        up = _dot(hn, wu_ref[:, cols])
        actbuf[:, cols] = (gate * jax.nn.sigmoid(gate) * up).astype(BF16)
    y = x_ref[...] + _dot(actbuf[...], wd_ref[...])
    if final_norm:
        y = _rms(y, fg_ref[...])
    o_ref[...] = y


def _cast_kernel(x_ref, o_ref):
    o_ref[...] = x_ref[...].astype(o_ref.dtype)


def _to_bf16(w):
    n_l, rows, cols = w.shape
    blk_rows = rows // CAST_SPLIT
    assert blk_rows * CAST_SPLIT == rows and blk_rows % (2 * SUBLANES) == 0
    blk = pl.BlockSpec((None, blk_rows, cols), lambda l, r: (l, r, 0))
    return pl.pallas_call(
        _cast_kernel,
        grid=(n_l, CAST_SPLIT),
        in_specs=[blk],
        out_specs=blk,
        out_shape=jax.ShapeDtypeStruct(w.shape, BF16),
        compiler_params=_cparams(2),
        name="to_bf16",
    )(w)


def kernel(x, ev_norm_g, ev_w_in, ev_sinks, ev_conv_w, ev_conv_b, ev_conv_ln_g, ev_conv_ln_b, ev_w_out, od_norm_g, od_w_in, od_sgu_ln_g, od_sgu_ln_b, od_spatial_w, od_spatial_b, od_w_out, ffn_norm_g, ffn_w_gate, ffn_w_up, ffn_w_down, final_norm_g):
    batch, seq, _ = x.shape
    assert seq % (max(DILATIONS) * BLOCK) == 0 and seq % TOK_TILE == 0
    row = lambda t: t.reshape(1, -1).astype(F32)
    x2 = x.reshape(batch * seq, D_MODEL)
    tabs = _rope_tables(seq)

    n_pairs = ATT_W // LANES
    head_order = [h for p in range(n_pairs) for h in (p, n_pairs + p)]
    q_cols = [h * HEAD_DIM + d for h in head_order for d in range(HEAD_DIM)]
    in_cols = np.asarray(q_cols + list(range(ATT_W, ev_w_in.shape[2])), np.int32)
    out_rows = np.asarray(q_cols + list(range(ATT_W, ev_w_out.shape[1])), np.int32)
    w_in0 = jnp.take(ev_w_in[0], in_cols, axis=1).astype(BF16)
    w_out0 = jnp.take(ev_w_out[0], out_rows, axis=0).astype(BF16)
    q, k, v, c = _ev_front(x2, row(ev_norm_g[0]), w_in0, tabs, seq)
    w_gate, w_up, w_down = _to_bf16(ffn_w_gate), _to_bf16(ffn_w_up), _to_bf16(ffn_w_down)
    ffn_w = lambda l: (row(ffn_norm_g[l]), w_gate[l], w_up[l], w_down[l], row(final_norm_g))
    h = _ev_tail(x2, q, k, v, c, ev_sinks[0].astype(F32), ev_conv_w[0].astype(F32),
                 row(ev_conv_b[0]), row(ev_conv_ln_g[0]), row(ev_conv_ln_b[0]), w_out0,
                 *ffn_w(0), batch, seq)

    *qkvs, u, gate = _od_front(h, row(od_norm_g[0]), od_w_in[0].astype(BF16), tabs,
                               row(od_sgu_ln_g[0]), row(od_sgu_ln_b[0]), batch, seq)
    outs, lses = [], []
    for dil, qkv in zip(DILATIONS, qkvs):
        o_r, lse_r = _branch(qkv, batch, seq, dil)
        outs.append(o_r)
        lses.append(lse_r)
    sb = jnp.repeat(od_spatial_b[0].astype(F32).T, HEAD_DIM, axis=1)
    h = _od_tail(h, outs, lses, u, gate, od_spatial_w[0].astype(F32), sb,
                 od_w_out[0].astype(BF16), *ffn_w(1), seq)
    return h.reshape(batch, seq, D_MODEL)
```

```python
import jax
import jax.numpy as jnp
import numpy as np
from jax import lax
from jax.experimental import pallas as pl
from jax.experimental.pallas import tpu as pltpu

F32 = jnp.float32
BF16 = jnp.bfloat16

D_MODEL = 1024
HEAD_DIM = 64
ROT_DIM = 16
ROPE_THETA = 500000.0
BLOCK = 128
RMS_EPS = 1e-6
LN_EPS = 1e-5
N_HEADS = 8
ATT_W = N_HEADS * HEAD_DIM
CONV_CH = 512
CONV_WIDTH = 31
CONV_HALO = 32
D_CH = 512
D_GROUPS = 8
CHUNK = 128
D_FF = 2816
DILATIONS = (1, 4, 16)
LANES = 128
SUBLANES = 8
NEG = -1e30

FRONT_TILE = 1024
TOK_TILE = 512
FF_CHUNK = 256
CONV_TAPS = 16
CAST_SPLIT = 2
VMEM_LIMIT = 56 * 1024 * 1024


def _cparams(n_axes):
    return pltpu.CompilerParams(
        dimension_semantics=("arbitrary",) * n_axes,
        vmem_limit_bytes=VMEM_LIMIT)


def _const_spec(shape, layer=None):
    nd = len(shape)
    if layer is None:
        return pl.BlockSpec(shape, lambda *_: (0,) * nd, pipeline_mode=pl.Buffered(1))
    return pl.BlockSpec((None,) + tuple(shape[1:]), lambda *_: (layer,) + (0,) * (nd - 1),
                        pipeline_mode=pl.Buffered(1))


def _ffn_specs(ffn, layer):
    fng, wg, wu, wd, fg = ffn
    return [_const_spec(fng.shape)] + [_const_spec(t.shape, layer) for t in (wg, wu, wd)] + [
        _const_spec(fg.shape)]


def _rms(x, g):
    ms = jnp.mean(x * x, axis=-1, keepdims=True)
    return x * lax.rsqrt(ms + RMS_EPS) * g


def _layer_norm(x, g, b):
    mu = jnp.mean(x, axis=-1, keepdims=True)
    xc = x - mu
    var = jnp.mean(xc * xc, axis=-1, keepdims=True)
    return xc * lax.rsqrt(var + LN_EPS) * g + b


def _dot(a, b):
    return jnp.dot(a, b, preferred_element_type=F32)


def _dot_nt(a, b):
    return lax.dot_general(a, b, (((1,), (1,)), ((), ())), preferred_element_type=F32)


def _rope(t, cos, sin_lo, sin_hi):
    up = pltpu.roll(t, LANES - ROT_DIM // 2, 1)
    down = pltpu.roll(t, ROT_DIM // 2, 1)
    return t * cos + up * sin_lo + down * sin_hi


def _rope_tables(seq):
    half = ROT_DIM // 2
    inv_freq = ROPE_THETA ** (-np.arange(half, dtype=np.float64) * (2.0 / ROT_DIM))
    ang = np.arange(seq, dtype=np.float64)[:, None] * inv_freq[None, :]
    cos, sin = np.cos(ang), np.sin(ang)
    ones = np.ones((seq, HEAD_DIM - ROT_DIM))
    zeros = np.zeros((seq, HEAD_DIM - ROT_DIM))
    zh = np.zeros((seq, half))
    cos_h = np.concatenate([cos, cos, ones], axis=1)
    lo_h = np.concatenate([-sin, zh, zeros], axis=1)
    hi_h = np.concatenate([zh, sin, zeros], axis=1)
    two = lambda t: jnp.asarray(np.concatenate([t, t], axis=1), F32)
    return two(cos_h), two(lo_h), two(hi_h)


def _ev_front_kernel(x_ref, g_ref, w_ref, cos_ref, lo_ref, hi_ref,
                     q_ref, k_ref, v_ref, c_ref):
    hn = _rms(x_ref[...], g_ref[...]).astype(BF16)
    cos, lo, hi = cos_ref[...], lo_ref[...], hi_ref[...]
    qkv = _dot(hn, w_ref[:, 0:ATT_W + 2 * LANES])
    for j in range(ATT_W // LANES):
        t = qkv[:, j * LANES:(j + 1) * LANES]
        q_ref[:, j * LANES:(j + 1) * LANES] = (
            _rope(t, cos, lo, hi) * (HEAD_DIM ** -0.5)).astype(BF16)
    k_ref[...] = _rope(qkv[:, ATT_W:ATT_W + LANES], cos, lo, hi).astype(BF16)
    v_ref[...] = qkv[:, ATT_W + LANES:ATT_W + 2 * LANES].astype(BF16)
    base = ATT_W + 2 * LANES
    ga = _dot(hn, w_ref[:, base:base + CONV_CH])
    gb = _dot(hn, w_ref[:, base + CONV_CH:base + 2 * CONV_CH])
    c_ref[...] = ga * jax.nn.sigmoid(gb)


def _od_front_kernel(x_ref, g_ref, w_ref, cos_ref, lo_ref, hi_ref, lng_ref, lnb_ref,
                     qkv1_ref, qkv4_ref, qkv16_ref, u_ref, gate_ref, stage, stage4):
    tile = x_ref.shape[0]
    d4, d16 = DILATIONS[1], DILATIONS[2]
    hn = _rms(x_ref[...], g_ref[...]).astype(BF16)
    cos, lo, hi = cos_ref[...], lo_ref[...], hi_ref[...]
    zg = jax.nn.gelu(_dot(hn, w_ref[:, 3 * ATT_W + D_CH:3 * ATT_W + 2 * D_CH]))
    gate_ref[...] = _layer_norm(zg, lng_ref[...], lnb_ref[...]).astype(BF16)
    u_ref[...] = jax.nn.gelu(_dot(hn, w_ref[:, 3 * ATT_W:3 * ATT_W + D_CH]))
    for s in (2, 1, 0):
        t = _dot(hn, w_ref[:, s * ATT_W:(s + 1) * ATT_W])
        for j in range(ATT_W // LANES):
            sl = slice(j * LANES, (j + 1) * LANES)
            buf, buf4 = stage.at[s, j], stage4.at[s, j]
            if s == 0:
                buf[...] = _rope(t[:, sl], cos, lo, hi) * (HEAD_DIM ** -0.5)
            elif s == 1:
                buf[...] = _rope(t[:, sl], cos, lo, hi)
            else:
                buf[...] = t[:, sl]
            qkv1_ref[s, 0, :, sl] = buf[...].astype(BF16)
            for r in range(d4):
                rows4 = slice(r * (tile // d4), (r + 1) * (tile // d4))
                buf4[rows4, :] = buf[pl.ds(r, tile // d4, stride=d4), :]
                qkv4_ref[s, r, :, sl] = buf4[rows4, :].astype(BF16)
            for r in range(d16):
                r4, a = r % d4, r // d4
                qkv16_ref[s, r, :, sl] = buf4[
                    pl.ds(r4 * (tile // d4) + a, tile // d16, stride=d16 // d4), :].astype(BF16)


def _front_common(x2, seq):
    n_tok = x2.shape[0]
    nt = n_tok // FRONT_TILE
    per_seq = seq // FRONT_TILE
    x_spec = pl.BlockSpec((FRONT_TILE, D_MODEL), lambda i: (i, 0))
    tab_spec = pl.BlockSpec((FRONT_TILE, LANES), lambda i: (i % per_seq, 0))
    out = lambda w, dt: (jax.ShapeDtypeStruct((n_tok, w), dt),
                         pl.BlockSpec((FRONT_TILE, w), lambda i: (i, 0)))
    return nt, x_spec, tab_spec, out


def _ev_front(x2, g, w, tabs, seq):
    nt, x_spec, tab_spec, out = _front_common(x2, seq)
    outs = [out(ATT_W, BF16), out(LANES, BF16), out(LANES, BF16), out(CONV_CH, F32)]
    return pl.pallas_call(
        _ev_front_kernel,
        grid=(nt,),
        in_specs=[x_spec, _const_spec(g.shape), _const_spec(w.shape),
                  tab_spec, tab_spec, tab_spec],
        out_specs=[o[1] for o in outs],
        out_shape=[o[0] for o in outs],
        compiler_params=_cparams(1),
        name="ev_front",
    )(x2, g, w, *tabs)


def _od_front(x2, g, w, tabs, lng, lnb, batch, seq):
    nt, x_spec, tab_spec, out = _front_common(x2, seq)
    per_seq = seq // FRONT_TILE

    def folded(dil):
        return (jax.ShapeDtypeStruct((batch, 3, dil, seq // dil, ATT_W), BF16),
                pl.BlockSpec((None, 3, dil, FRONT_TILE // dil, ATT_W),
                             lambda i: (i // per_seq, 0, 0, i % per_seq, 0)))

    outs = [folded(d) for d in DILATIONS] + [out(D_CH, F32), out(D_CH, BF16)]
    return pl.pallas_call(
        _od_front_kernel,
        grid=(nt,),
        in_specs=[x_spec, _const_spec(g.shape), _const_spec(w.shape),
                  tab_spec, tab_spec, tab_spec,
                  _const_spec(lng.shape), _const_spec(lnb.shape)],
        out_specs=[o[1] for o in outs],
        out_shape=[o[0] for o in outs],
        scratch_shapes=[pltpu.VMEM((3, ATT_W // LANES, FRONT_TILE, LANES), F32)] * 2,
        compiler_params=_cparams(1),
        name="od_front",
    )(x2, g, w, *tabs, lng, lnb)


def _band_masks(min_dist):
    qi = lax.broadcasted_iota(jnp.int32, (BLOCK, 2 * BLOCK), 0)
    kj = lax.broadcasted_iota(jnp.int32, (BLOCK, 2 * BLOCK), 1)
    dist = qi + BLOCK - kj
    band = (dist >= 0) & (dist <= BLOCK - 1 + min_dist)
    return band, kj < BLOCK


def _softmax_rows(s, band, prev_keys, prev_penalty):
    s = jnp.where(band, s, NEG)
    if prev_penalty is not None:
        s = jnp.where(prev_keys, s + prev_penalty, s)
    m = jnp.max(s, axis=-1, keepdims=True)
    p = jnp.exp(s - m)
    l = jnp.sum(p, axis=-1, keepdims=True)
    return p, m, l


def _conv_skew(tile):
    skew = tile // SUBLANES + 1
    assert skew % 2 == 1
    return skew


def _ev_mix_kernel(sink_ref, x_ref, q_ref, k_ref, kh_ref, v_ref, vh_ref, c_ref, ch_ref,
                   convw_ref, convb_ref, lng_ref, lnb_ref, wout_ref, o_ref,
                   kbuf, vbuf, cbuf, cout, mixbuf):
    i = pl.program_id(1)
    tile = q_ref.shape[0]
    n_pairs = ATT_W // LANES

    kbuf[0:BLOCK, :] = kh_ref[...]
    kbuf[BLOCK:, :] = k_ref[...]
    vbuf[0:BLOCK, :] = vh_ref[...]
    vbuf[BLOCK:, :] = v_ref[...]

    row = lax.broadcasted_iota(jnp.int32, (BLOCK, BLOCK), 0)
    col = lax.broadcasted_iota(jnp.int32, (BLOCK, BLOCK), 1)
    own = col <= row
    first_penalty = jnp.where(i == 0, NEG, 0.0).astype(F32)
    low_lanes = lax.broadcasted_iota(jnp.int32, (BLOCK, LANES), 1) < HEAD_DIM

    for jb in range(tile // BLOCK):
        rows = slice(jb * BLOCK, (jb + 1) * BLOCK)
        kk = kbuf[jb * BLOCK:(jb + 2) * BLOCK, :]
        vv = vbuf[jb * BLOCK:(jb + 2) * BLOCK, :]
        parts = []
        for half in range(2):
            keep = low_lanes if half == 0 else jnp.logical_not(low_lanes)
            for p in range(n_pairs):
                qp = q_ref[rows, p * LANES:(p + 1) * LANES]
                parts.append(jnp.where(keep, qp, jnp.zeros_like(qp)))
        s_all = _dot_nt(jnp.concatenate(parts, axis=0), kk)
        probs, inv_l = [], []
        for idx in range(2 * n_pairs):
            s_prev = s_all[idx * BLOCK:(idx + 1) * BLOCK, 0:BLOCK]
            s_own = s_all[idx * BLOCK:(idx + 1) * BLOCK, BLOCK:]
            if jb == 0:
                s_prev = s_prev + first_penalty
            sc = jnp.where(own, s_own, s_prev)
            sink = sink_ref[idx]
            m = jnp.maximum(jnp.max(sc, axis=-1, keepdims=True), sink)
            pr = jnp.exp(sc - m)
            l = jnp.sum(pr, axis=-1, keepdims=True) + jnp.exp(sink - m)
            zero = jnp.zeros_like(pr)
            probs.append(jnp.concatenate([jnp.where(own, zero, pr), jnp.where(own, pr, zero)],
                                         axis=1).astype(BF16))
            inv_l.append(1.0 / l)
        o_all = _dot(jnp.concatenate(probs, axis=0), vv)
        for p in range(n_pairs):
            o_lo = o_all[p * BLOCK:(p + 1) * BLOCK] * inv_l[p]
            o_hi = o_all[(n_pairs + p) * BLOCK:(n_pairs + p + 1) * BLOCK] * inv_l[n_pairs + p]
            mixbuf[rows, p * LANES:(p + 1) * LANES] = jnp.where(low_lanes, o_lo, o_hi).astype(BF16)

    n_lg = CONV_CH // LANES
    skew = _conv_skew(tile)
    lead = CONV_HALO - (CONV_WIDTH - 1)
    for lg in range(n_lg):
        lanes = slice(lg * LANES, (lg + 1) * LANES)
        cbuf[lg, 0:CONV_HALO, :] = jnp.where(i > 0, ch_ref[:, lanes], 0.0)
        cbuf[lg, CONV_HALO:CONV_HALO + tile, :] = c_ref[:, lanes]
        cbuf[lg, CONV_HALO + tile:, :] = jnp.zeros((cbuf.shape[1] - CONV_HALO - tile, LANES), F32)
    skewed = lambda ref, lg, start: ref.at[lg, pl.ds(start, SUBLANES, stride=skew), :]
    one_trip = jnp.minimum(i, 0) + 1

    def conv_lane_group(lg):
        lanes = slice(lg * LANES, (lg + 1) * LANES)
        for t0 in range(0, CONV_WIDTH, CONV_TAPS):
            n_taps = min(CONV_TAPS, CONV_WIDTH - t0)
            w = [jnp.broadcast_to(convw_ref[t0 + tt:t0 + tt + 1, lanes], (SUBLANES, LANES))
                 for tt in range(n_taps)]
            acc = {}
            for o in range(skew + n_taps - 1):
                xs = skewed(cbuf, lg, o + lead + t0)[...]
                for tt in range(n_taps):
                    j = o - tt
                    if not 0 <= j < skew:
                        continue
                    if tt == 0:
                        acc[j] = (jnp.broadcast_to(convb_ref[:, lanes], (SUBLANES, LANES))
                                  if t0 == 0 else skewed(cout, lg, j)[...])
                    acc[j] = acc[j] + xs * w[tt]
                    if tt == n_taps - 1:
                        skewed(cout, lg, j)[...] = acc.pop(j)

    for lg in range(n_lg):
        lax.fori_loop(0, one_trip, lambda _, carry, lg=lg: (conv_lane_group(lg), carry)[1], 0)
    for j in range(skew):
        y = _layer_norm(jnp.concatenate([skewed(cout, lg, j)[...] for lg in range(n_lg)], axis=1),
                        lng_ref[...], lnb_ref[...])
        y = y * jax.nn.sigmoid(y)
        for lg in range(n_lg):
            skewed(cout, lg, j)[...] = y[:, lg * LANES:(lg + 1) * LANES]
    for lg in range(n_lg):
        mixbuf[:, ATT_W + lg * LANES:ATT_W + (lg + 1) * LANES] = cout[lg, 0:tile, :].astype(BF16)

    o_ref[...] = x_ref[...] + _dot(mixbuf[...], wout_ref[...])


def _ev_tail_kernel(sink_ref, x_ref, q_ref, k_ref, kh_ref, v_ref, vh_ref, c_ref, ch_ref,
                    convw_ref, convb_ref, lng_ref, lnb_ref, wout_ref,
                    fng_ref, wg_ref, wu_ref, wd_ref, fg_ref, o_ref,
                    kbuf, vbuf, cbuf, cout, mixbuf, actbuf, h_mid):
    _ev_mix_kernel(sink_ref, x_ref, q_ref, k_ref, kh_ref, v_ref, vh_ref, c_ref, ch_ref,
                   convw_ref, convb_ref, lng_ref, lnb_ref, wout_ref, h_mid,
                   kbuf, vbuf, cbuf, cout, mixbuf)
    _ffn_kernel(h_mid, fng_ref, wg_ref, wu_ref, wd_ref, fg_ref, o_ref, actbuf, final_norm=False)


def _ev_tail(x2, q, k, v, c, sinks, convw, convb, lng, lnb, wout, ffn, layer, batch, seq):
    tile = TOK_TILE
    nt = seq // tile
    tok = lambda w: pl.BlockSpec((tile, w), lambda b, i: (b * nt + i, 0))
    halo = lambda rows, w: pl.BlockSpec(
        (rows, w), lambda b, i: (jnp.maximum((b * nt + i) * (tile // rows) - 1, 0), 0))
    consts = [convw, convb, lng, lnb, wout]
    skew_rows = SUBLANES * _conv_skew(tile)
    return pl.pallas_call(
        _ev_tail_kernel,
        grid=(batch, nt),
        in_specs=[pl.BlockSpec(memory_space=pltpu.SMEM),
                  tok(D_MODEL), tok(ATT_W), tok(LANES), halo(BLOCK, LANES),
                  tok(LANES), halo(BLOCK, LANES), tok(CONV_CH), halo(CONV_HALO, CONV_CH)]
                 + [_const_spec(t.shape) for t in consts] + _ffn_specs(ffn, layer),
        out_specs=tok(D_MODEL),
        out_shape=jax.ShapeDtypeStruct(x2.shape, F32),
        scratch_shapes=[pltpu.VMEM((tile + BLOCK, LANES), BF16),
                        pltpu.VMEM((tile + BLOCK, LANES), BF16),
                        pltpu.VMEM((CONV_CH // LANES, CONV_HALO + skew_rows, LANES), F32),
                        pltpu.VMEM((CONV_CH // LANES, skew_rows, LANES), F32),
                        pltpu.VMEM((tile, ATT_W + CONV_CH), BF16),
                        pltpu.VMEM((tile, D_FF), BF16),
                        pltpu.VMEM((tile, D_MODEL), F32)],
        compiler_params=_cparams(2),
        name="ev_tail",
    )(sinks, x2, q, k, k, v, v, c, c, *consts, *ffn)


def _branch_kernel(q_ref, k_ref, kh_ref, v_ref, vh_ref, o_ref, lse_ref, kbuf, vbuf):
    i = pl.program_id(2)
    tile = q_ref.shape[0]

    kbuf[0:BLOCK, :] = kh_ref[...]
    kbuf[BLOCK:, :] = k_ref[...]
    vbuf[0:BLOCK, :] = vh_ref[...]
    vbuf[BLOCK:, :] = v_ref[...]

    band, prev_keys = _band_masks(1)
    first_penalty = jnp.where(i == 0, NEG, 0.0).astype(F32)
    low_lanes = lax.broadcasted_iota(jnp.int32, (BLOCK, LANES), 1) < HEAD_DIM

    for jb in range(tile // BLOCK):
        rows = slice(jb * BLOCK, (jb + 1) * BLOCK)
        for p in range(ATT_W // LANES):
            lanes = slice(p * LANES, (p + 1) * LANES)
            kk = kbuf[jb * BLOCK:(jb + 2) * BLOCK, lanes]
            vv = vbuf[jb * BLOCK:(jb + 2) * BLOCK, lanes]
            qp = q_ref[rows, lanes]
            zero = jnp.zeros_like(qp)
            q2 = jnp.concatenate([jnp.where(low_lanes, qp, zero),
                                  jnp.where(low_lanes, zero, qp)], axis=0)
            s2 = _dot_nt(q2, kk)
            probs, inv_l, lse = [], [], []
            for half in range(2):
                pr, m, l = _softmax_rows(s2[half * BLOCK:(half + 1) * BLOCK], band, prev_keys,
                                         first_penalty if jb == 0 else None)
                probs.append(pr.astype(BF16))
                inv_l.append(1.0 / l)
                lse.append(m + jnp.log(l))
            o2 = _dot(jnp.concatenate(probs, axis=0), vv)
            o_ref[rows, lanes] = jnp.where(low_lanes, o2[0:BLOCK] * inv_l[0],
                                           o2[BLOCK:] * inv_l[1])
            lse_ref[rows, lanes] = jnp.where(low_lanes, lse[0], lse[1])


def _branch(qkv, batch, seq, dil):
    sub = seq // dil
    tile = min(TOK_TILE, sub)
    nt = sub // tile

    def cur(s):
        return pl.BlockSpec((None, None, None, tile, ATT_W), lambda b, r, i: (b, s, r, i, 0))

    def halo(s):
        return pl.BlockSpec(
            (None, None, None, BLOCK, ATT_W),
            lambda b, r, i: (b, s, r, jnp.maximum(i * (tile // BLOCK) - 1, 0), 0))

    out_spec = pl.BlockSpec((None, None, tile, ATT_W), lambda b, r, i: (b, r, i, 0))
    out_sds = jax.ShapeDtypeStruct((batch, dil, sub, ATT_W), F32)
    return pl.pallas_call(
        _branch_kernel,
        grid=(batch, dil, nt),
        in_specs=[cur(0), cur(1), halo(1), cur(2), halo(2)],
        out_specs=[out_spec, out_spec],
        out_shape=[out_sds, out_sds],
        scratch_shapes=[pltpu.VMEM((tile + BLOCK, ATT_W), BF16),
                        pltpu.VMEM((tile + BLOCK, ATT_W), BF16)],
        compiler_params=_cparams(3),
        name=f"od_branch_d{dil}",
    )(qkv, qkv, qkv, qkv, qkv)


def _od_mix_kernel(x_ref, o1_ref, o2_ref, o3_ref, l1_ref, l2_ref, l3_ref, u_ref, gate_ref,
                   sw_ref, sb_ref, wout_ref, out_ref, mixbuf, tok_o2, tok_l2, tok_o3, tok_l3):
    tile = x_ref.shape[0]
    for j in range(ATT_W // LANES):
        sl = slice(j * LANES, (j + 1) * LANES)
        for src, dst in ((o2_ref, tok_o2), (l2_ref, tok_l2), (o3_ref, tok_o3), (l3_ref, tok_l3)):
            dil = src.shape[0]
            for r in range(dil):
                dst[j, pl.ds(r, tile // dil, stride=dil), :] = src[r, :, sl]
        l1, l2, l3 = l1_ref[0, :, sl], tok_l2[j], tok_l3[j]
        m = jnp.maximum(jnp.maximum(l1, l2), l3)
        e1, e2, e3 = jnp.exp(l1 - m), jnp.exp(l2 - m), jnp.exp(l3 - m)
        mixed = (e1 * o1_ref[0, :, sl] + e2 * tok_o2[j] + e3 * tok_o3[j]) / (e1 + e2 + e3)
        mixbuf[:, sl] = mixed.astype(BF16)

    ti = lax.broadcasted_iota(jnp.int32, (CHUNK, CHUNK), 0)
    si = lax.broadcasted_iota(jnp.int32, (CHUNK, CHUNK), 1)
    causal = ti >= si
    low_lanes = lax.broadcasted_iota(jnp.int32, (CHUNK, LANES), 1) < HEAD_DIM
    for p in range(D_CH // LANES):
        lanes = slice(p * LANES, (p + 1) * LANES)
        w2 = jnp.concatenate(
            [jnp.where(causal, sw_ref[2 * p], 0.0), jnp.where(causal, sw_ref[2 * p + 1], 0.0)],
            axis=0).astype(BF16)
        bias = sb_ref[:, lanes]
        for ch in range(tile // CHUNK):
            rows = slice(ch * CHUNK, (ch + 1) * CHUNK)
            y2 = _dot(w2, gate_ref[rows, lanes])
            y = jnp.where(low_lanes, y2[0:CHUNK], y2[CHUNK:]) + bias
            mixbuf[rows, ATT_W + p * LANES:ATT_W + (p + 1) * LANES] = (
                u_ref[rows, lanes] * y).astype(BF16)

    out_ref[...] = x_ref[...] + _dot(mixbuf[...], wout_ref[...])


def _od_tail_kernel(x_ref, o1_ref, o2_ref, o3_ref, l1_ref, l2_ref, l3_ref, u_ref, gate_ref,
                    sw_ref, sb_ref, wout_ref, fng_ref, wg_ref, wu_ref, wd_ref, fg_ref, out_ref,
                    mixbuf, tok_o2, tok_l2, tok_o3, tok_l3, actbuf, h_mid):
    _od_mix_kernel(x_ref, o1_ref, o2_ref, o3_ref, l1_ref, l2_ref, l3_ref, u_ref, gate_ref,
                   sw_ref, sb_ref, wout_ref, h_mid, mixbuf, tok_o2, tok_l2, tok_o3, tok_l3)
    _ffn_kernel(h_mid, fng_ref, wg_ref, wu_ref, wd_ref, fg_ref, out_ref, actbuf, final_norm=True)


def _od_tail(x2, outs, lses, u, gate, sw, sb, wout, ffn, layer, seq):
    tile = TOK_TILE
    per_seq = seq // tile
    tok = lambda w: pl.BlockSpec((tile, w), lambda i: (i, 0))
    folded = [pl.BlockSpec((None, d, tile // d, ATT_W),
                           lambda i: (i // per_seq, 0, i % per_seq, 0)) for d in DILATIONS]
    consts = [sw, sb, wout]
    return pl.pallas_call(
        _od_tail_kernel,
        grid=(x2.shape[0] // tile,),
        in_specs=[tok(D_MODEL)] + folded + folded + [tok(D_CH), tok(D_CH)]
                 + [_const_spec(t.shape) for t in consts] + _ffn_specs(ffn, layer),
        out_specs=tok(D_MODEL),
        out_shape=jax.ShapeDtypeStruct(x2.shape, F32),
        scratch_shapes=[pltpu.VMEM((tile, ATT_W + D_CH), BF16)]
                       + [pltpu.VMEM((ATT_W // LANES, tile, LANES), F32)] * 4
                       + [pltpu.VMEM((tile, D_FF), BF16), pltpu.VMEM((tile, D_MODEL), F32)],
        compiler_params=_cparams(1),
        name="od_tail",
    )(x2, *outs, *lses, u, gate, *consts, *ffn)


def _ffn_kernel(x_ref, g_ref, wg_ref, wu_ref, wd_ref, fg_ref, o_ref, actbuf, *, final_norm):
    hn = _rms(x_ref[...], g_ref[...]).astype(BF16)
    for c in range(D_FF // FF_CHUNK):
        cols = slice(c * FF_CHUNK, (c + 1) * FF_CHUNK)
        gate = _dot(hn, wg_ref[:, cols])
        up = _dot(hn, wu_ref[:, cols])
        actbuf[:, cols] = (gate * jax.nn.sigmoid(gate) * up).astype(BF16)
    y = x_ref[...] + _dot(actbuf[...], wd_ref[...])
    if final_norm:
        y = _rms(y, fg_ref[...])
    o_ref[...] = y


def _cast_kernel(x_ref, o_ref):
    o_ref[...] = x_ref[...].astype(o_ref.dtype)


def _to_bf16(w):
    n_l, rows, cols = w.shape
    blk_rows = rows // CAST_SPLIT
    assert blk_rows * CAST_SPLIT == rows and blk_rows % (2 * SUBLANES) == 0
    blk = pl.BlockSpec((None, blk_rows, cols), lambda l, r: (l, r, 0))
    return pl.pallas_call(
        _cast_kernel,
        grid=(n_l, CAST_SPLIT),
        in_specs=[blk],
        out_specs=blk,
        out_shape=jax.ShapeDtypeStruct(w.shape, BF16),
        compiler_params=_cparams(2),
        name="to_bf16",
    )(w)


def kernel(x, ev_norm_g, ev_w_in, ev_sinks, ev_conv_w, ev_conv_b, ev_conv_ln_g, ev_conv_ln_b, ev_w_out, od_norm_g, od_w_in, od_sgu_ln_g, od_sgu_ln_b, od_spatial_w, od_spatial_b, od_w_out, ffn_norm_g, ffn_w_gate, ffn_w_up, ffn_w_down, final_norm_g):
    batch, seq, _ = x.shape
    assert seq % (max(DILATIONS) * BLOCK) == 0 and seq % FRONT_TILE == 0 and seq % TOK_TILE == 0
    row = lambda t: t.reshape(1, -1).astype(F32)
    x2 = x.reshape(batch * seq, D_MODEL)
    tabs = _rope_tables(seq)

    n_pairs = ATT_W // LANES
    head_order = [h for p in range(n_pairs) for h in (p, n_pairs + p)]
    q_cols = [h * HEAD_DIM + d for h in head_order for d in range(HEAD_DIM)]
    in_cols = np.asarray(q_cols + list(range(ATT_W, ev_w_in.shape[2])), np.int32)
    out_rows = np.asarray(q_cols + list(range(ATT_W, ev_w_out.shape[1])), np.int32)
    w_in0 = jnp.take(ev_w_in[0], in_cols, axis=1).astype(BF16)
    w_out0 = jnp.take(ev_w_out[0], out_rows, axis=0).astype(BF16)
    q, k, v, c = _ev_front(x2, row(ev_norm_g[0]), w_in0, tabs, seq)
    w_gate, w_up, w_down = _to_bf16(ffn_w_gate), _to_bf16(ffn_w_up), _to_bf16(ffn_w_down)
    ffn_w = lambda l: (row(ffn_norm_g[l]), w_gate, w_up, w_down, row(final_norm_g))
    h = _ev_tail(x2, q, k, v, c, ev_sinks[0].astype(F32), ev_conv_w[0].astype(F32),
                 row(ev_conv_b[0]), row(ev_conv_ln_g[0]), row(ev_conv_ln_b[0]), w_out0,
                 ffn_w(0), 0, batch, seq)

    *qkvs, u, gate = _od_front(h, row(od_norm_g[0]), od_w_in[0].astype(BF16), tabs,
                               row(od_sgu_ln_g[0]), row(od_sgu_ln_b[0]), batch, seq)
    outs, lses = [], []
    for dil, qkv in zip(DILATIONS, qkvs):
        o_r, lse_r = _branch(qkv, batch, seq, dil)
        outs.append(o_r)
        lses.append(lse_r)
    sb = jnp.repeat(od_spatial_b[0].astype(F32).T, HEAD_DIM, axis=1)
    h = _od_tail(h, outs, lses, u, gate, od_spatial_w[0].astype(F32), sb,
                 od_w_out[0].astype(BF16), ffn_w(1), 1, seq)
    return h.reshape(batch, seq, D_MODEL)
```

```python
import functools

import jax
import jax.numpy as jnp
import numpy as np
from jax import lax
from jax.experimental import pallas as pl
from jax.experimental.pallas import tpu as pltpu

F32 = jnp.float32
BF16 = jnp.bfloat16

D_MODEL = 1024
HEAD_DIM = 64
ROT_DIM = 16
ROPE_THETA = 500000.0
BLOCK = 128
RMS_EPS = 1e-6
LN_EPS = 1e-5
N_HEADS = 8
ATT_W = N_HEADS * HEAD_DIM
CONV_CH = 512
CONV_WIDTH = 31
CONV_HALO = 32
D_CH = 512
D_GROUPS = 8
CHUNK = 128
D_FF = 2816
DILATIONS = (1, 4, 16)
LANES = 128
SUBLANES = 8
NEG = -1e30

FRONT_TILE = 1024
TOK_TILE = 512
FF_CHUNK = 256
CONV_TAPS = 16
CAST_SPLIT = 2
VMEM_LIMIT = 56 * 1024 * 1024


def _cparams(n_axes):
    return pltpu.CompilerParams(
        dimension_semantics=("arbitrary",) * n_axes,
        vmem_limit_bytes=VMEM_LIMIT)


def _const_spec(shape, layer=None):
    nd = len(shape)
    if layer is None:
        return pl.BlockSpec(shape, lambda *_: (0,) * nd, pipeline_mode=pl.Buffered(1))
    return pl.BlockSpec((None,) + tuple(shape[1:]), lambda *_: (layer,) + (0,) * (nd - 1),
                        pipeline_mode=pl.Buffered(1))


def _ffn_specs(ffn, layer):
    fng, wg, wu, wd, fg = ffn
    return [_const_spec(fng.shape)] + [_const_spec(t.shape, layer) for t in (wg, wu, wd)] + [
        _const_spec(fg.shape)]


def _rms(x, g):
    ms = jnp.mean(x * x, axis=-1, keepdims=True)
    return x * lax.rsqrt(ms + RMS_EPS) * g


def _layer_norm(x, g, b):
    mu = jnp.mean(x, axis=-1, keepdims=True)
    xc = x - mu
    var = jnp.mean(xc * xc, axis=-1, keepdims=True)
    return xc * lax.rsqrt(var + LN_EPS) * g + b


def _dot(a, b):
    return jnp.dot(a, b, preferred_element_type=F32)


def _dot_nt(a, b):
    return lax.dot_general(a, b, (((1,), (1,)), ((), ())), preferred_element_type=F32)


def _rope(t, cos, sin_lo, sin_hi):
    up = pltpu.roll(t, LANES - ROT_DIM // 2, 1)
    down = pltpu.roll(t, ROT_DIM // 2, 1)
    return t * cos + up * sin_lo + down * sin_hi


def _rope_tables(seq):
    half = ROT_DIM // 2
    inv_freq = ROPE_THETA ** (-np.arange(half, dtype=np.float64) * (2.0 / ROT_DIM))
    ang = np.arange(seq, dtype=np.float64)[:, None] * inv_freq[None, :]
    cos, sin = np.cos(ang), np.sin(ang)
    ones = np.ones((seq, HEAD_DIM - ROT_DIM))
    zeros = np.zeros((seq, HEAD_DIM - ROT_DIM))
    zh = np.zeros((seq, half))
    cos_h = np.concatenate([cos, cos, ones], axis=1)
    lo_h = np.concatenate([-sin, zh, zeros], axis=1)
    hi_h = np.concatenate([zh, sin, zeros], axis=1)
    two = lambda t: jnp.asarray(np.concatenate([t, t], axis=1), F32)
    return two(cos_h), two(lo_h), two(hi_h)


def _ev_front_kernel(x_ref, g_ref, w_ref, cos_ref, lo_ref, hi_ref,
                     q_ref, k_ref, v_ref, c_ref):
    hn = _rms(x_ref[...], g_ref[...]).astype(BF16)
    cos, lo, hi = cos_ref[...], lo_ref[...], hi_ref[...]
    qkv = _dot(hn, w_ref[:, 0:ATT_W + 2 * LANES])
    for j in range(ATT_W // LANES):
        t = qkv[:, j * LANES:(j + 1) * LANES]
        q_ref[:, j * LANES:(j + 1) * LANES] = (
            _rope(t, cos, lo, hi) * (HEAD_DIM ** -0.5)).astype(BF16)
    k_ref[...] = _rope(qkv[:, ATT_W:ATT_W + LANES], cos, lo, hi).astype(BF16)
    v_ref[...] = qkv[:, ATT_W + LANES:ATT_W + 2 * LANES].astype(BF16)
    base = ATT_W + 2 * LANES
    ga = _dot(hn, w_ref[:, base:base + CONV_CH])
    gb = _dot(hn, w_ref[:, base + CONV_CH:base + 2 * CONV_CH])
    c_ref[...] = ga * jax.nn.sigmoid(gb)


def _od_front_kernel(x_ref, g_ref, w_ref, cos_ref, lo_ref, hi_ref, lng_ref, lnb_ref,
                     qkv1_ref, qkv4_ref, qkv16_ref, u_ref, gate_ref, stage, stage4):
    tile = x_ref.shape[0]
    d4, d16 = DILATIONS[1], DILATIONS[2]
    hn = _rms(x_ref[...], g_ref[...]).astype(BF16)
    cos, lo, hi = cos_ref[...], lo_ref[...], hi_ref[...]
    zg = jax.nn.gelu(_dot(hn, w_ref[:, 3 * ATT_W + D_CH:3 * ATT_W + 2 * D_CH]))
    gate_ref[...] = _layer_norm(zg, lng_ref[...], lnb_ref[...]).astype(BF16)
    u_ref[...] = jax.nn.gelu(_dot(hn, w_ref[:, 3 * ATT_W:3 * ATT_W + D_CH]))
    for s in (2, 1, 0):
        t = _dot(hn, w_ref[:, s * ATT_W:(s + 1) * ATT_W])
        for j in range(ATT_W // LANES):
            sl = slice(j * LANES, (j + 1) * LANES)
            buf, buf4 = stage.at[s, j], stage4.at[s, j]
            if s == 0:
                buf[...] = _rope(t[:, sl], cos, lo, hi) * (HEAD_DIM ** -0.5)
            elif s == 1:
                buf[...] = _rope(t[:, sl], cos, lo, hi)
            else:
                buf[...] = t[:, sl]
            qkv1_ref[s, 0, :, sl] = buf[...].astype(BF16)
            for r in range(d4):
                rows4 = slice(r * (tile // d4), (r + 1) * (tile // d4))
                buf4[rows4, :] = buf[pl.ds(r, tile // d4, stride=d4), :]
                qkv4_ref[s, r, :, sl] = buf4[rows4, :].astype(BF16)
            for r in range(d16):
                r4, a = r % d4, r // d4
                qkv16_ref[s, r, :, sl] = buf4[
                    pl.ds(r4 * (tile // d4) + a, tile // d16, stride=d16 // d4), :].astype(BF16)


def _front_common(x2, seq):
    n_tok = x2.shape[0]
    nt = n_tok // FRONT_TILE
    per_seq = seq // FRONT_TILE
    x_spec = pl.BlockSpec((FRONT_TILE, D_MODEL), lambda i: (i, 0))
    tab_spec = pl.BlockSpec((FRONT_TILE, LANES), lambda i: (i % per_seq, 0))
    out = lambda w, dt: (jax.ShapeDtypeStruct((n_tok, w), dt),
                         pl.BlockSpec((FRONT_TILE, w), lambda i: (i, 0)))
    return nt, x_spec, tab_spec, out


def _ev_front(x2, g, w, tabs, seq):
    nt, x_spec, tab_spec, out = _front_common(x2, seq)
    outs = [out(ATT_W, BF16), out(LANES, BF16), out(LANES, BF16), out(CONV_CH, F32)]
    return pl.pallas_call(
        _ev_front_kernel,
        grid=(nt,),
        in_specs=[x_spec, _const_spec(g.shape), _const_spec(w.shape),
                  tab_spec, tab_spec, tab_spec],
        out_specs=[o[1] for o in outs],
        out_shape=[o[0] for o in outs],
        compiler_params=_cparams(1),
        name="ev_front",
    )(x2, g, w, *tabs)


def _od_front(x2, g, w, tabs, lng, lnb, batch, seq):
    nt, x_spec, tab_spec, out = _front_common(x2, seq)
    per_seq = seq // FRONT_TILE

    def folded(dil):
        return (jax.ShapeDtypeStruct((batch, 3, dil, seq // dil, ATT_W), BF16),
                pl.BlockSpec((None, 3, dil, FRONT_TILE // dil, ATT_W),
                             lambda i: (i // per_seq, 0, 0, i % per_seq, 0)))

    outs = [folded(d) for d in DILATIONS] + [out(D_CH, F32), out(D_CH, BF16)]
    return pl.pallas_call(
        _od_front_kernel,
        grid=(nt,),
        in_specs=[x_spec, _const_spec(g.shape), _const_spec(w.shape),
                  tab_spec, tab_spec, tab_spec,
                  _const_spec(lng.shape), _const_spec(lnb.shape)],
        out_specs=[o[1] for o in outs],
        out_shape=[o[0] for o in outs],
        scratch_shapes=[pltpu.VMEM((3, ATT_W // LANES, FRONT_TILE, LANES), F32)] * 2,
        compiler_params=_cparams(1),
        name="od_front",
    )(x2, g, w, *tabs, lng, lnb)


def _band_masks(min_dist):
    qi = lax.broadcasted_iota(jnp.int32, (BLOCK, 2 * BLOCK), 0)
    kj = lax.broadcasted_iota(jnp.int32, (BLOCK, 2 * BLOCK), 1)
    dist = qi + BLOCK - kj
    band = (dist >= 0) & (dist <= BLOCK - 1 + min_dist)
    return band, kj < BLOCK


def _softmax_rows(s, band, prev_keys, prev_penalty):
    s = jnp.where(band, s, NEG)
    if prev_penalty is not None:
        s = jnp.where(prev_keys, s + prev_penalty, s)
    m = jnp.max(s, axis=-1, keepdims=True)
    p = jnp.exp(s - m)
    l = jnp.sum(p, axis=-1, keepdims=True)
    return p, m, l


def _conv_skew(tile):
    skew = tile // SUBLANES + 1
    assert skew % 2 == 1
    return skew


def _ev_mix_items(i, sink_ref, x_ref, q_ref, k_ref, kh_ref, v_ref, vh_ref, c_ref, ch_ref,
                  convw_ref, convb_ref, lng_ref, lnb_ref, wout_ref, o_ref,
                  kbuf, vbuf, cbuf, cout, mixbuf):
    tile = q_ref.shape[0]
    n_pairs = ATT_W // LANES
    n_lg = CONV_CH // LANES
    skew = _conv_skew(tile)
    lead = CONV_HALO - (CONV_WIDTH - 1)
    skewed = lambda ref, lg, start: ref.at[lg, pl.ds(start, SUBLANES, stride=skew), :]

    def stage():
        kbuf[0:BLOCK, :] = kh_ref[...]
        kbuf[BLOCK:, :] = k_ref[...]
        vbuf[0:BLOCK, :] = vh_ref[...]
        vbuf[BLOCK:, :] = v_ref[...]
        for lg in range(n_lg):
            lanes = slice(lg * LANES, (lg + 1) * LANES)
            cbuf[lg, 0:CONV_HALO, :] = jnp.where(i > 0, ch_ref[:, lanes], 0.0)
            cbuf[lg, CONV_HALO:CONV_HALO + tile, :] = c_ref[:, lanes]
            cbuf[lg, CONV_HALO + tile:, :] = jnp.zeros(
                (cbuf.shape[1] - CONV_HALO - tile, LANES), F32)

    def attend(jb):
        row = lax.broadcasted_iota(jnp.int32, (BLOCK, BLOCK), 0)
        col = lax.broadcasted_iota(jnp.int32, (BLOCK, BLOCK), 1)
        own = col <= row
        first_penalty = jnp.where(i == 0, NEG, 0.0).astype(F32)
        low_lanes = lax.broadcasted_iota(jnp.int32, (BLOCK, LANES), 1) < HEAD_DIM
        rows = slice(jb * BLOCK, (jb + 1) * BLOCK)
        kk = kbuf[jb * BLOCK:(jb + 2) * BLOCK, :]
        vv = vbuf[jb * BLOCK:(jb + 2) * BLOCK, :]
        parts = []
        for half in range(2):
            keep = low_lanes if half == 0 else jnp.logical_not(low_lanes)
            for p in range(n_pairs):
                qp = q_ref[rows, p * LANES:(p + 1) * LANES]
                parts.append(jnp.where(keep, qp, jnp.zeros_like(qp)))
        s_all = _dot_nt(jnp.concatenate(parts, axis=0), kk)
        probs, inv_l = [], []
        for idx in range(2 * n_pairs):
            s_prev = s_all[idx * BLOCK:(idx + 1) * BLOCK, 0:BLOCK]
            s_own = s_all[idx * BLOCK:(idx + 1) * BLOCK, BLOCK:]
            if jb == 0:
                s_prev = s_prev + first_penalty
            sc = jnp.where(own, s_own, s_prev)
            sink = sink_ref[idx]
            m = jnp.maximum(jnp.max(sc, axis=-1, keepdims=True), sink)
            pr = jnp.exp(sc - m)
            l = jnp.sum(pr, axis=-1, keepdims=True) + jnp.exp(sink - m)
            zero = jnp.zeros_like(pr)
            probs.append(jnp.concatenate([jnp.where(own, zero, pr), jnp.where(own, pr, zero)],
                                         axis=1).astype(BF16))
            inv_l.append(1.0 / l)
        o_all = _dot(jnp.concatenate(probs, axis=0), vv)
        for p in range(n_pairs):
            o_lo = o_all[p * BLOCK:(p + 1) * BLOCK] * inv_l[p]
            o_hi = o_all[(n_pairs + p) * BLOCK:(n_pairs + p + 1) * BLOCK] * inv_l[n_pairs + p]
            mixbuf[rows, p * LANES:(p + 1) * LANES] = jnp.where(low_lanes, o_lo, o_hi).astype(BF16)

    def conv(lg, t0):
        lanes = slice(lg * LANES, (lg + 1) * LANES)
        n_taps = min(CONV_TAPS, CONV_WIDTH - t0)
        w = [jnp.broadcast_to(convw_ref[t0 + tt:t0 + tt + 1, lanes], (SUBLANES, LANES))
             for tt in range(n_taps)]
        acc = {}
        for o in range(skew + n_taps - 1):
            xs = skewed(cbuf, lg, o + lead + t0)[...]
            for tt in range(n_taps):
                j = o - tt
                if not 0 <= j < skew:
                    continue
                if tt == 0:
                    acc[j] = (jnp.broadcast_to(convb_ref[:, lanes], (SUBLANES, LANES))
                              if t0 == 0 else skewed(cout, lg, j)[...])
                acc[j] = acc[j] + xs * w[tt]
                if tt == n_taps - 1:
                    skewed(cout, lg, j)[...] = acc.pop(j)

    def norm_conv(j0, j1):
        for j in range(j0, j1):
            y = _layer_norm(
                jnp.concatenate([skewed(cout, lg, j)[...] for lg in range(n_lg)], axis=1),
                lng_ref[...], lnb_ref[...])
            y = y * jax.nn.sigmoid(y)
            for lg in range(n_lg):
                skewed(cout, lg, j)[...] = y[:, lg * LANES:(lg + 1) * LANES]

    def pack():
        for lg in range(n_lg):
            mixbuf[:, ATT_W + lg * LANES:ATT_W + (lg + 1) * LANES] = (
                cout[lg, 0:tile, :].astype(BF16))

    def project():
        o_ref[...] = x_ref[...] + _dot(mixbuf[...], wout_ref[...])

    P = functools.partial
    convs = [P(conv, lg, t0) for lg in range(n_lg) for t0 in range(0, CONV_WIDTH, CONV_TAPS)]
    cuts = [skew * n // 4 for n in range(5)]
    norms = [P(norm_conv, cuts[n], cuts[n + 1]) for n in range(4)]
    return dict(stage=stage, attend=[P(attend, jb) for jb in range(tile // BLOCK)],
                conv=convs, norm=norms, pack=pack, project=project)


def _regions(one_trip, groups):
    for group in groups:
        def body(_, carry, group=group):
            for item in group:
                item()
            return carry
        lax.fori_loop(0, one_trip, body, 0)


def _ffn_items(x_ref, g_ref, wg_ref, wu_ref, wd_ref, fg_ref, hnbuf, actbuf, o_ref, final_norm):
    def norm():
        hnbuf[...] = _rms(x_ref[...], g_ref[...]).astype(BF16)

    def gate_up(c):
        cols = slice(c * FF_CHUNK, (c + 1) * FF_CHUNK)
        gate = _dot(hnbuf[...], wg_ref[:, cols])
        up = _dot(hnbuf[...], wu_ref[:, cols])
        actbuf[:, cols] = (gate * jax.nn.sigmoid(gate) * up).astype(BF16)

    def down():
        y = x_ref[...] + _dot(actbuf[...], wd_ref[...])
        if final_norm:
            y = _rms(y, fg_ref[...])
        o_ref[...] = y

    P = functools.partial
    return dict(norm=norm, gate_up=[P(gate_up, c) for c in range(D_FF // FF_CHUNK)], down=down)


def _ev_layer_kernel(sink_ref, x_ref, q_ref, k_ref, kh_ref, v_ref, vh_ref, c_ref, ch_ref,
                     convw_ref, convb_ref, lng_ref, lnb_ref, wout_ref,
                     fng_ref, wg_ref, wu_ref, wd_ref, fg_ref, o_ref,
                     kbuf, vbuf, cbuf, cout, mixbuf, hnbuf, actbuf, h_mid,
                     *, n_tiles, tiles_per_seq):
    g = pl.program_id(0)

    @pl.when(g == 0)
    def _():
        h_mid[...] = jnp.zeros_like(h_mid)

    i = jnp.minimum(g, n_tiles - 1) % tiles_per_seq
    one_trip = jnp.minimum(g, 0) + 1
    f = _ffn_items(h_mid, fng_ref, wg_ref, wu_ref, wd_ref, fg_ref, hnbuf, actbuf, o_ref, False)
    m = _ev_mix_items(i, sink_ref, x_ref, q_ref, k_ref, kh_ref, v_ref, vh_ref, c_ref, ch_ref,
                      convw_ref, convb_ref, lng_ref, lnb_ref, wout_ref, h_mid,
                      kbuf, vbuf, cbuf, cout, mixbuf)
    paired = []
    for n, gu in enumerate(f["gate_up"]):
        paired += ([m["conv"][n]] if n < len(m["conv"]) else []) + [gu]
    groups = [[f["norm"], m["stage"]], m["attend"], paired,
              m["norm"] + [m["pack"], f["down"]], [m["project"]]]
    _regions(one_trip, groups)


def _pipelined_specs(n_tiles, tile):
    cur = lambda g: jnp.minimum(g, n_tiles - 1)
    tok = lambda w: pl.BlockSpec((tile, w), lambda g: (cur(g), 0))
    halo = lambda rows, w: pl.BlockSpec(
        (rows, w), lambda g: (jnp.maximum(cur(g) * (tile // rows) - 1, 0), 0))
    out = pl.BlockSpec((tile, D_MODEL), lambda g: (jnp.maximum(g - 1, 0), 0))
    return tok, halo, out


def _ev_layer(x2, q, k, v, c, sinks, convw, convb, lng, lnb, wout, ffn, layer, seq):
    tile = TOK_TILE
    n_tiles = x2.shape[0] // tile
    tok, halo, out = _pipelined_specs(n_tiles, tile)
    consts = [convw, convb, lng, lnb, wout]
    skew_rows = SUBLANES * _conv_skew(tile)
    return pl.pallas_call(
        functools.partial(_ev_layer_kernel, n_tiles=n_tiles, tiles_per_seq=seq // tile),
        grid=(n_tiles + 1,),
        in_specs=[pl.BlockSpec(memory_space=pltpu.SMEM),
                  tok(D_MODEL), tok(ATT_W), tok(LANES), halo(BLOCK, LANES),
                  tok(LANES), halo(BLOCK, LANES), tok(CONV_CH), halo(CONV_HALO, CONV_CH)]
                 + [_const_spec(t.shape) for t in consts] + _ffn_specs(ffn, layer),
        out_specs=out,
        out_shape=jax.ShapeDtypeStruct(x2.shape, F32),
        scratch_shapes=[pltpu.VMEM((tile + BLOCK, LANES), BF16),
                        pltpu.VMEM((tile + BLOCK, LANES), BF16),
                        pltpu.VMEM((CONV_CH // LANES, CONV_HALO + skew_rows, LANES), F32),
                        pltpu.VMEM((CONV_CH // LANES, skew_rows, LANES), F32),
                        pltpu.VMEM((tile, ATT_W + CONV_CH), BF16),
                        pltpu.VMEM((tile, D_MODEL), BF16),
                        pltpu.VMEM((tile, D_FF), BF16),
                        pltpu.VMEM((tile, D_MODEL), F32)],
        compiler_params=_cparams(1),
        name="ev_layer",
    )(sinks, x2, q, k, k, v, v, c, c, *consts, *ffn)


def _branch_kernel(q_ref, k_ref, kh_ref, v_ref, vh_ref, o_ref, lse_ref, kbuf, vbuf):
    i = pl.program_id(2)
    tile = q_ref.shape[0]

    kbuf[0:BLOCK, :] = kh_ref[...]
    kbuf[BLOCK:, :] = k_ref[...]
    vbuf[0:BLOCK, :] = vh_ref[...]
    vbuf[BLOCK:, :] = v_ref[...]

    band, prev_keys = _band_masks(1)
    first_penalty = jnp.where(i == 0, NEG, 0.0).astype(F32)
    low_lanes = lax.broadcasted_iota(jnp.int32, (BLOCK, LANES), 1) < HEAD_DIM

    for jb in range(tile // BLOCK):
        rows = slice(jb * BLOCK, (jb + 1) * BLOCK)
        for p in range(ATT_W // LANES):
            lanes = slice(p * LANES, (p + 1) * LANES)
            kk = kbuf[jb * BLOCK:(jb + 2) * BLOCK, lanes]
            vv = vbuf[jb * BLOCK:(jb + 2) * BLOCK, lanes]
            qp = q_ref[rows, lanes]
            zero = jnp.zeros_like(qp)
            q2 = jnp.concatenate([jnp.where(low_lanes, qp, zero),
                                  jnp.where(low_lanes, zero, qp)], axis=0)
            s2 = _dot_nt(q2, kk)
            probs, inv_l, lse = [], [], []
            for half in range(2):
                pr, m, l = _softmax_rows(s2[half * BLOCK:(half + 1) * BLOCK], band, prev_keys,
                                         first_penalty if jb == 0 else None)
                probs.append(pr.astype(BF16))
                inv_l.append(1.0 / l)
                lse.append(m + jnp.log(l))
            o2 = _dot(jnp.concatenate(probs, axis=0), vv)
            o_ref[rows, lanes] = jnp.where(low_lanes, o2[0:BLOCK] * inv_l[0],
                                           o2[BLOCK:] * inv_l[1])
            lse_ref[rows, lanes] = jnp.where(low_lanes, lse[0], lse[1])


def _branch(qkv, batch, seq, dil):
    sub = seq // dil
    tile = min(TOK_TILE, sub)
    nt = sub // tile

    def cur(s):
        return pl.BlockSpec((None, None, None, tile, ATT_W), lambda b, r, i: (b, s, r, i, 0))

    def halo(s):
        return pl.BlockSpec(
            (None, None, None, BLOCK, ATT_W),
            lambda b, r, i: (b, s, r, jnp.maximum(i * (tile // BLOCK) - 1, 0), 0))

    out_spec = pl.BlockSpec((None, None, tile, ATT_W), lambda b, r, i: (b, r, i, 0))
    out_sds = jax.ShapeDtypeStruct((batch, dil, sub, ATT_W), F32)
    return pl.pallas_call(
        _branch_kernel,
        grid=(batch, dil, nt),
        in_specs=[cur(0), cur(1), halo(1), cur(2), halo(2)],
        out_specs=[out_spec, out_spec],
        out_shape=[out_sds, out_sds],
        scratch_shapes=[pltpu.VMEM((tile + BLOCK, ATT_W), BF16),
                        pltpu.VMEM((tile + BLOCK, ATT_W), BF16)],
        compiler_params=_cparams(3),
        name=f"od_branch_d{dil}",
    )(qkv, qkv, qkv, qkv, qkv)


def _od_mix_kernel(x_ref, o1_ref, o2_ref, o3_ref, l1_ref, l2_ref, l3_ref, u_ref, gate_ref,
                   sw_ref, sb_ref, wout_ref, out_ref, mixbuf, tok_o2, tok_l2, tok_o3, tok_l3):
    tile = x_ref.shape[0]
    for j in range(ATT_W // LANES):
        sl = slice(j * LANES, (j + 1) * LANES)
        for src, dst in ((o2_ref, tok_o2), (l2_ref, tok_l2), (o3_ref, tok_o3), (l3_ref, tok_l3)):
            dil = src.shape[0]
            for r in range(dil):
                dst[j, pl.ds(r, tile // dil, stride=dil), :] = src[r, :, sl]
        l1, l2, l3 = l1_ref[0, :, sl], tok_l2[j], tok_l3[j]
        m = jnp.maximum(jnp.maximum(l1, l2), l3)
        e1, e2, e3 = jnp.exp(l1 - m), jnp.exp(l2 - m), jnp.exp(l3 - m)
        mixed = (e1 * o1_ref[0, :, sl] + e2 * tok_o2[j] + e3 * tok_o3[j]) / (e1 + e2 + e3)
        mixbuf[:, sl] = mixed.astype(BF16)

    ti = lax.broadcasted_iota(jnp.int32, (CHUNK, CHUNK), 0)
    si = lax.broadcasted_iota(jnp.int32, (CHUNK, CHUNK), 1)
    causal = ti >= si
    low_lanes = lax.broadcasted_iota(jnp.int32, (CHUNK, LANES), 1) < HEAD_DIM
    for p in range(D_CH // LANES):
        lanes = slice(p * LANES, (p + 1) * LANES)
        w2 = jnp.concatenate(
            [jnp.where(causal, sw_ref[2 * p], 0.0), jnp.where(causal, sw_ref[2 * p + 1], 0.0)],
            axis=0).astype(BF16)
        bias = sb_ref[:, lanes]
        for ch in range(tile // CHUNK):
            rows = slice(ch * CHUNK, (ch + 1) * CHUNK)
            y2 = _dot(w2, gate_ref[rows, lanes])
            y = jnp.where(low_lanes, y2[0:CHUNK], y2[CHUNK:]) + bias
            mixbuf[rows, ATT_W + p * LANES:ATT_W + (p + 1) * LANES] = (
                u_ref[rows, lanes] * y).astype(BF16)

    out_ref[...] = x_ref[...] + _dot(mixbuf[...], wout_ref[...])


def _od_tail_kernel(x_ref, o1_ref, o2_ref, o3_ref, l1_ref, l2_ref, l3_ref, u_ref, gate_ref,
                    sw_ref, sb_ref, wout_ref, fng_ref, wg_ref, wu_ref, wd_ref, fg_ref, out_ref,
                    mixbuf, tok_o2, tok_l2, tok_o3, tok_l3, actbuf, h_mid):
    _od_mix_kernel(x_ref, o1_ref, o2_ref, o3_ref, l1_ref, l2_ref, l3_ref, u_ref, gate_ref,
                   sw_ref, sb_ref, wout_ref, h_mid, mixbuf, tok_o2, tok_l2, tok_o3, tok_l3)
    _ffn_kernel(h_mid, fng_ref, wg_ref, wu_ref, wd_ref, fg_ref, out_ref, actbuf, final_norm=True)


def _od_tail(x2, outs, lses, u, gate, sw, sb, wout, ffn, layer, seq):
    tile = TOK_TILE
    per_seq = seq // tile
    tok = lambda w: pl.BlockSpec((tile, w), lambda i: (i, 0))
    folded = [pl.BlockSpec((None, d, tile // d, ATT_W),
                           lambda i: (i // per_seq, 0, i % per_seq, 0)) for d in DILATIONS]
    consts = [sw, sb, wout]
    return pl.pallas_call(
        _od_tail_kernel,
        grid=(x2.shape[0] // tile,),
        in_specs=[tok(D_MODEL)] + folded + folded + [tok(D_CH), tok(D_CH)]
                 + [_const_spec(t.shape) for t in consts] + _ffn_specs(ffn, layer),
        out_specs=tok(D_MODEL),
        out_shape=jax.ShapeDtypeStruct(x2.shape, F32),
        scratch_shapes=[pltpu.VMEM((tile, ATT_W + D_CH), BF16)]
                       + [pltpu.VMEM((ATT_W // LANES, tile, LANES), F32)] * 4
                       + [pltpu.VMEM((tile, D_FF), BF16), pltpu.VMEM((tile, D_MODEL), F32)],
        compiler_params=_cparams(1),
        name="od_tail",
    )(x2, *outs, *lses, u, gate, *consts, *ffn)


def _ffn_kernel(x_ref, g_ref, wg_ref, wu_ref, wd_ref, fg_ref, o_ref, actbuf, *, final_norm):
    hn = _rms(x_ref[...], g_ref[...]).astype(BF16)
    for c in range(D_FF // FF_CHUNK):
        cols = slice(c * FF_CHUNK, (c + 1) * FF_CHUNK)
        gate = _dot(hn, wg_ref[:, cols])
        up = _dot(hn, wu_ref[:, cols])
        actbuf[:, cols] = (gate * jax.nn.sigmoid(gate) * up).astype(BF16)
    y = x_ref[...] + _dot(actbuf[...], wd_ref[...])
    if final_norm:
        y = _rms(y, fg_ref[...])
    o_ref[...] = y


def _cast_kernel(x_ref, o_ref):
    o_ref[...] = x_ref[...].astype(o_ref.dtype)


def _to_bf16(w):
    n_l, rows, cols = w.shape
    blk_rows = rows // CAST_SPLIT
    assert blk_rows * CAST_SPLIT == rows and blk_rows % (2 * SUBLANES) == 0
    blk = pl.BlockSpec((None, blk_rows, cols), lambda l, r: (l, r, 0))
    return pl.pallas_call(
        _cast_kernel,
        grid=(n_l, CAST_SPLIT),
        in_specs=[blk],
        out_specs=blk,
        out_shape=jax.ShapeDtypeStruct(w.shape, BF16),
        compiler_params=_cparams(2),
        name="to_bf16",
    )(w)


def kernel(x, ev_norm_g, ev_w_in, ev_sinks, ev_conv_w, ev_conv_b, ev_conv_ln_g, ev_conv_ln_b, ev_w_out, od_norm_g, od_w_in, od_sgu_ln_g, od_sgu_ln_b, od_spatial_w, od_spatial_b, od_w_out, ffn_norm_g, ffn_w_gate, ffn_w_up, ffn_w_down, final_norm_g):
    batch, seq, _ = x.shape
    assert seq % (max(DILATIONS) * BLOCK) == 0 and seq % FRONT_TILE == 0 and seq % TOK_TILE == 0
    row = lambda t: t.reshape(1, -1).astype(F32)
    x2 = x.reshape(batch * seq, D_MODEL)
    tabs = _rope_tables(seq)

    n_pairs = ATT_W // LANES
    head_order = [h for p in range(n_pairs) for h in (p, n_pairs + p)]
    q_cols = [h * HEAD_DIM + d for h in head_order for d in range(HEAD_DIM)]
    in_cols = np.asarray(q_cols + list(range(ATT_W, ev_w_in.shape[2])), np.int32)
    out_rows = np.asarray(q_cols + list(range(ATT_W, ev_w_out.shape[1])), np.int32)
    w_in0 = jnp.take(ev_w_in[0], in_cols, axis=1).astype(BF16)
    w_out0 = jnp.take(ev_w_out[0], out_rows, axis=0).astype(BF16)
    q, k, v, c = _ev_front(x2, row(ev_norm_g[0]), w_in0, tabs, seq)
    w_gate, w_up, w_down = _to_bf16(ffn_w_gate), _to_bf16(ffn_w_up), _to_bf16(ffn_w_down)
    ffn_w = lambda l: (row(ffn_norm_g[l]), w_gate, w_up, w_down, row(final_norm_g))
    h = _ev_layer(x2, q, k, v, c, ev_sinks[0].astype(F32), ev_conv_w[0].astype(F32),
                  row(ev_conv_b[0]), row(ev_conv_ln_g[0]), row(ev_conv_ln_b[0]), w_out0,
                  ffn_w(0), 0, seq)

    *qkvs, u, gate = _od_front(h, row(od_norm_g[0]), od_w_in[0].astype(BF16), tabs,
                               row(od_sgu_ln_g[0]), row(od_sgu_ln_b[0]), batch, seq)
    outs, lses = [], []
    for dil, qkv in zip(DILATIONS, qkvs):
        o_r, lse_r = _branch(qkv, batch, seq, dil)
        outs.append(o_r)
        lses.append(lse_r)
    sb = jnp.repeat(od_spatial_b[0].astype(F32).T, HEAD_DIM, axis=1)
    h = _od_tail(h, outs, lses, u, gate, od_spatial_w[0].astype(F32), sb,
                 od_w_out[0].astype(BF16), ffn_w(1), 1, seq)
    return h.reshape(batch, seq, D_MODEL)
```

```python
import jax
import jax.numpy as jnp
import numpy as np
from jax import lax
from jax.experimental import pallas as pl
from jax.experimental.pallas import tpu as pltpu

F32 = jnp.float32
BF16 = jnp.bfloat16

D_MODEL = 1024
HEAD_DIM = 64
ROT_DIM = 16
ROPE_THETA = 500000.0
BLOCK = 128
RMS_EPS = 1e-6
LN_EPS = 1e-5
N_HEADS = 8
ATT_W = N_HEADS * HEAD_DIM
CONV_CH = 512
CONV_WIDTH = 31
CONV_HALO = 32
D_CH = 512
D_GROUPS = 8
CHUNK = 128
D_FF = 2816
DILATIONS = (1, 4, 16)
LANES = 128
SUBLANES = 8
NEG = -1e30

FRONT_TILE = 1024
TOK_TILE = 512
BRANCH_ROWS = 2048
FF_CHUNK = 256
CONV_TAPS = 16
CAST_SPLIT = 2
VMEM_LIMIT = 56 * 1024 * 1024


def _cparams(n_axes):
    return pltpu.CompilerParams(
        dimension_semantics=("arbitrary",) * n_axes,
        vmem_limit_bytes=VMEM_LIMIT)


def _const_spec(shape, layer=None):
    nd = len(shape)
    if layer is None:
        return pl.BlockSpec(shape, lambda *_: (0,) * nd, pipeline_mode=pl.Buffered(1))
    return pl.BlockSpec((None,) + tuple(shape[1:]), lambda *_: (layer,) + (0,) * (nd - 1),
                        pipeline_mode=pl.Buffered(1))


def _ffn_specs(ffn, layer):
    fng, wg, wu, wd, fg = ffn
    return [_const_spec(fng.shape)] + [_const_spec(t.shape, layer) for t in (wg, wu, wd)] + [
        _const_spec(fg.shape)]


def _rms(x, g):
    ms = jnp.mean(x * x, axis=-1, keepdims=True)
    return x * lax.rsqrt(ms + RMS_EPS) * g


def _layer_norm(x, g, b):
    mu = jnp.mean(x, axis=-1, keepdims=True)
    xc = x - mu
    var = jnp.mean(xc * xc, axis=-1, keepdims=True)
    return xc * lax.rsqrt(var + LN_EPS) * g + b


def _dot(a, b):
    return jnp.dot(a, b, preferred_element_type=F32)


def _dot_nt(a, b):
    return lax.dot_general(a, b, (((1,), (1,)), ((), ())), preferred_element_type=F32)


def _rope(t, cos, sin_lo, sin_hi):
    up = pltpu.roll(t, LANES - ROT_DIM // 2, 1)
    down = pltpu.roll(t, ROT_DIM // 2, 1)
    return t * cos + up * sin_lo + down * sin_hi


def _rope_tables(seq):
    half = ROT_DIM // 2
    inv_freq = ROPE_THETA ** (-np.arange(half, dtype=np.float64) * (2.0 / ROT_DIM))
    ang = np.arange(seq, dtype=np.float64)[:, None] * inv_freq[None, :]
    cos, sin = np.cos(ang), np.sin(ang)
    ones = np.ones((seq, HEAD_DIM - ROT_DIM))
    zeros = np.zeros((seq, HEAD_DIM - ROT_DIM))
    zh = np.zeros((seq, half))
    cos_h = np.concatenate([cos, cos, ones], axis=1)
    lo_h = np.concatenate([-sin, zh, zeros], axis=1)
    hi_h = np.concatenate([zh, sin, zeros], axis=1)
    two = lambda t: jnp.asarray(np.concatenate([t, t], axis=1), F32)
    return two(cos_h), two(lo_h), two(hi_h)


def _ev_front_kernel(x_ref, g_ref, w_ref, cos_ref, lo_ref, hi_ref,
                     q_ref, k_ref, v_ref, c_ref):
    hn = _rms(x_ref[...], g_ref[...]).astype(BF16)
    cos, lo, hi = cos_ref[...], lo_ref[...], hi_ref[...]
    qkv = _dot(hn, w_ref[:, 0:ATT_W + 2 * LANES])
    for j in range(ATT_W // LANES):
        t = qkv[:, j * LANES:(j + 1) * LANES]
        q_ref[:, j * LANES:(j + 1) * LANES] = (
            _rope(t, cos, lo, hi) * (HEAD_DIM ** -0.5)).astype(BF16)
    k_ref[...] = _rope(qkv[:, ATT_W:ATT_W + LANES], cos, lo, hi).astype(BF16)
    v_ref[...] = qkv[:, ATT_W + LANES:ATT_W + 2 * LANES].astype(BF16)
    base = ATT_W + 2 * LANES
    ga = _dot(hn, w_ref[:, base:base + CONV_CH])
    gb = _dot(hn, w_ref[:, base + CONV_CH:base + 2 * CONV_CH])
    c_ref[...] = ga * jax.nn.sigmoid(gb)


def _od_front_kernel(x_ref, g_ref, w_ref, cos_ref, lo_ref, hi_ref, lng_ref, lnb_ref,
                     qkv1_ref, qkv4_ref, qkv16_ref, u_ref, gate_ref, stage, stage4):
    tile = x_ref.shape[0]
    d4, d16 = DILATIONS[1], DILATIONS[2]
    hn = _rms(x_ref[...], g_ref[...]).astype(BF16)
    cos, lo, hi = cos_ref[...], lo_ref[...], hi_ref[...]
    zg = jax.nn.gelu(_dot(hn, w_ref[:, 3 * ATT_W + D_CH:3 * ATT_W + 2 * D_CH]))
    gate_ref[...] = _layer_norm(zg, lng_ref[...], lnb_ref[...]).astype(BF16)
    u_ref[...] = jax.nn.gelu(_dot(hn, w_ref[:, 3 * ATT_W:3 * ATT_W + D_CH]))
    for s in (2, 1, 0):
        t = _dot(hn, w_ref[:, s * ATT_W:(s + 1) * ATT_W])
        for j in range(ATT_W // LANES):
            sl = slice(j * LANES, (j + 1) * LANES)
            buf, buf4 = stage.at[s, j], stage4.at[s, j]
            if s == 0:
                buf[...] = _rope(t[:, sl], cos, lo, hi) * (HEAD_DIM ** -0.5)
            elif s == 1:
                buf[...] = _rope(t[:, sl], cos, lo, hi)
            else:
                buf[...] = t[:, sl]
            qkv1_ref[s, 0, :, sl] = buf[...].astype(BF16)
            for r in range(d4):
                rows4 = slice(r * (tile // d4), (r + 1) * (tile // d4))
                buf4[rows4, :] = buf[pl.ds(r, tile // d4, stride=d4), :]
                qkv4_ref[s, r, :, sl] = buf4[rows4, :].astype(BF16)
            for r in range(d16):
                r4, a = r % d4, r // d4
                qkv16_ref[s, r, :, sl] = buf4[
                    pl.ds(r4 * (tile // d4) + a, tile // d16, stride=d16 // d4), :].astype(BF16)


def _front_common(x2, seq):
    n_tok = x2.shape[0]
    nt = n_tok // FRONT_TILE
    per_seq = seq // FRONT_TILE
    x_spec = pl.BlockSpec((FRONT_TILE, D_MODEL), lambda i: (i, 0))
    tab_spec = pl.BlockSpec((FRONT_TILE, LANES), lambda i: (i % per_seq, 0))
    out = lambda w, dt: (jax.ShapeDtypeStruct((n_tok, w), dt),
                         pl.BlockSpec((FRONT_TILE, w), lambda i: (i, 0)))
    return nt, x_spec, tab_spec, out


def _ev_front(x2, g, w, tabs, seq):
    nt, x_spec, tab_spec, out = _front_common(x2, seq)
    outs = [out(ATT_W, BF16), out(LANES, BF16), out(LANES, BF16), out(CONV_CH, F32)]
    return pl.pallas_call(
        _ev_front_kernel,
        grid=(nt,),
        in_specs=[x_spec, _const_spec(g.shape), _const_spec(w.shape),
                  tab_spec, tab_spec, tab_spec],
        out_specs=[o[1] for o in outs],
        out_shape=[o[0] for o in outs],
        compiler_params=_cparams(1),
        name="ev_front",
    )(x2, g, w, *tabs)


def _od_front(x2, g, w, tabs, lng, lnb, batch, seq):
    nt, x_spec, tab_spec, out = _front_common(x2, seq)
    per_seq = seq // FRONT_TILE

    def folded(dil):
        return (jax.ShapeDtypeStruct((batch, 3, dil, seq // dil, ATT_W), BF16),
                pl.BlockSpec((None, 3, dil, FRONT_TILE // dil, ATT_W),
                             lambda i: (i // per_seq, 0, 0, i % per_seq, 0)))

    outs = [folded(d) for d in DILATIONS] + [out(D_CH, F32), out(D_CH, BF16)]
    return pl.pallas_call(
        _od_front_kernel,
        grid=(nt,),
        in_specs=[x_spec, _const_spec(g.shape), _const_spec(w.shape),
                  tab_spec, tab_spec, tab_spec,
                  _const_spec(lng.shape), _const_spec(lnb.shape)],
        out_specs=[o[1] for o in outs],
        out_shape=[o[0] for o in outs],
        scratch_shapes=[pltpu.VMEM((3, ATT_W // LANES, FRONT_TILE, LANES), F32)] * 2,
        compiler_params=_cparams(1),
        name="od_front",
    )(x2, g, w, *tabs, lng, lnb)


def _band_masks(min_dist):
    qi = lax.broadcasted_iota(jnp.int32, (BLOCK, 2 * BLOCK), 0)
    kj = lax.broadcasted_iota(jnp.int32, (BLOCK, 2 * BLOCK), 1)
    dist = qi + BLOCK - kj
    band = (dist >= 0) & (dist <= BLOCK - 1 + min_dist)
    return band, kj < BLOCK


def _softmax_rows(s, band, prev_keys, prev_penalty):
    s = jnp.where(band, s, NEG)
    if prev_penalty is not None:
        s = jnp.where(prev_keys, s + prev_penalty, s)
    m = jnp.max(s, axis=-1, keepdims=True)
    p = jnp.exp(s - m)
    l = jnp.sum(p, axis=-1, keepdims=True)
    return p, m, l


def _conv_skew(tile):
    skew = tile // SUBLANES + 1
    assert skew % 2 == 1
    return skew


def _ev_mix_kernel(sink_ref, x_ref, q_ref, k_ref, kh_ref, v_ref, vh_ref, c_ref, ch_ref,
                   convw_ref, convb_ref, lng_ref, lnb_ref, wout_ref, o_ref,
                   kbuf, vbuf, cbuf, cout, mixbuf):
    i = pl.program_id(1)
    tile = q_ref.shape[0]
    n_pairs = ATT_W // LANES

    kbuf[0:BLOCK, :] = kh_ref[...]
    kbuf[BLOCK:, :] = k_ref[...]
    vbuf[0:BLOCK, :] = vh_ref[...]
    vbuf[BLOCK:, :] = v_ref[...]

    row = lax.broadcasted_iota(jnp.int32, (BLOCK, BLOCK), 0)
    col = lax.broadcasted_iota(jnp.int32, (BLOCK, BLOCK), 1)
    own = col <= row
    first_penalty = jnp.where(i == 0, NEG, 0.0).astype(F32)
    low_lanes = lax.broadcasted_iota(jnp.int32, (BLOCK, LANES), 1) < HEAD_DIM

    for jb in range(tile // BLOCK):
        rows = slice(jb * BLOCK, (jb + 1) * BLOCK)
        kk = kbuf[jb * BLOCK:(jb + 2) * BLOCK, :]
        vv = vbuf[jb * BLOCK:(jb + 2) * BLOCK, :]
        parts = []
        for half in range(2):
            keep = low_lanes if half == 0 else jnp.logical_not(low_lanes)
            for p in range(n_pairs):
                qp = q_ref[rows, p * LANES:(p + 1) * LANES]
                parts.append(jnp.where(keep, qp, jnp.zeros_like(qp)))
        s_all = _dot_nt(jnp.concatenate(parts, axis=0), kk)
        probs, inv_l = [], []
        for idx in range(2 * n_pairs):
            s_prev = s_all[idx * BLOCK:(idx + 1) * BLOCK, 0:BLOCK]
            s_own = s_all[idx * BLOCK:(idx + 1) * BLOCK, BLOCK:]
            if jb == 0:
                s_prev = s_prev + first_penalty
            sc = jnp.where(own, s_own, s_prev)
            sink = sink_ref[idx]
            m = jnp.maximum(jnp.max(sc, axis=-1, keepdims=True), sink)
            pr = jnp.exp(sc - m)
            l = jnp.sum(pr, axis=-1, keepdims=True) + jnp.exp(sink - m)
            zero = jnp.zeros_like(pr)
            probs.append(jnp.concatenate([jnp.where(own, zero, pr), jnp.where(own, pr, zero)],
                                         axis=1).astype(BF16))
            inv_l.append(1.0 / l)
        o_all = _dot(jnp.concatenate(probs, axis=0), vv)
        for p in range(n_pairs):
            o_lo = o_all[p * BLOCK:(p + 1) * BLOCK] * inv_l[p]
            o_hi = o_all[(n_pairs + p) * BLOCK:(n_pairs + p + 1) * BLOCK] * inv_l[n_pairs + p]
            mixbuf[rows, p * LANES:(p + 1) * LANES] = jnp.where(low_lanes, o_lo, o_hi).astype(BF16)

    n_lg = CONV_CH // LANES
    skew = _conv_skew(tile)
    lead = CONV_HALO - (CONV_WIDTH - 1)
    for lg in range(n_lg):
        lanes = slice(lg * LANES, (lg + 1) * LANES)
        cbuf[lg, 0:CONV_HALO, :] = jnp.where(i > 0, ch_ref[:, lanes], 0.0)
        cbuf[lg, CONV_HALO:CONV_HALO + tile, :] = c_ref[:, lanes]
        cbuf[lg, CONV_HALO + tile:, :] = jnp.zeros((cbuf.shape[1] - CONV_HALO - tile, LANES), F32)
    skewed = lambda ref, lg, start: ref.at[lg, pl.ds(start, SUBLANES, stride=skew), :]
    one_trip = jnp.minimum(i, 0) + 1

    def conv_lane_group(lg):
        lanes = slice(lg * LANES, (lg + 1) * LANES)
        for t0 in range(0, CONV_WIDTH, CONV_TAPS):
            n_taps = min(CONV_TAPS, CONV_WIDTH - t0)
            w = [jnp.broadcast_to(convw_ref[t0 + tt:t0 + tt + 1, lanes], (SUBLANES, LANES))
                 for tt in range(n_taps)]
            acc = {}
            for o in range(skew + n_taps - 1):
                xs = skewed(cbuf, lg, o + lead + t0)[...]
                for tt in range(n_taps):
                    j = o - tt
                    if not 0 <= j < skew:
                        continue
                    if tt == 0:
                        acc[j] = (jnp.broadcast_to(convb_ref[:, lanes], (SUBLANES, LANES))
                                  if t0 == 0 else skewed(cout, lg, j)[...])
                    acc[j] = acc[j] + xs * w[tt]
                    if tt == n_taps - 1:
                        skewed(cout, lg, j)[...] = acc.pop(j)

    for lg in range(n_lg):
        lax.fori_loop(0, one_trip, lambda _, carry, lg=lg: (conv_lane_group(lg), carry)[1], 0)
    for j in range(skew):
        y = _layer_norm(jnp.concatenate([skewed(cout, lg, j)[...] for lg in range(n_lg)], axis=1),
                        lng_ref[...], lnb_ref[...])
        y = y * jax.nn.sigmoid(y)
        for lg in range(n_lg):
            skewed(cout, lg, j)[...] = y[:, lg * LANES:(lg + 1) * LANES]
    for lg in range(n_lg):
        mixbuf[:, ATT_W + lg * LANES:ATT_W + (lg + 1) * LANES] = cout[lg, 0:tile, :].astype(BF16)

    o_ref[...] = x_ref[...] + _dot(mixbuf[...], wout_ref[...])


def _ev_tail_kernel(sink_ref, x_ref, q_ref, k_ref, kh_ref, v_ref, vh_ref, c_ref, ch_ref,
                    convw_ref, convb_ref, lng_ref, lnb_ref, wout_ref,
                    fng_ref, wg_ref, wu_ref, wd_ref, fg_ref, o_ref,
                    kbuf, vbuf, cbuf, cout, mixbuf, actbuf, h_mid):
    _ev_mix_kernel(sink_ref, x_ref, q_ref, k_ref, kh_ref, v_ref, vh_ref, c_ref, ch_ref,
                   convw_ref, convb_ref, lng_ref, lnb_ref, wout_ref, h_mid,
                   kbuf, vbuf, cbuf, cout, mixbuf)
    _ffn_kernel(h_mid, fng_ref, wg_ref, wu_ref, wd_ref, fg_ref, o_ref, actbuf, final_norm=False)


def _ev_tail(x2, q, k, v, c, sinks, convw, convb, lng, lnb, wout, ffn, layer, batch, seq):
    tile = TOK_TILE
    nt = seq // tile
    tok = lambda w: pl.BlockSpec((tile, w), lambda b, i: (b * nt + i, 0))
    halo = lambda rows, w: pl.BlockSpec(
        (rows, w), lambda b, i: (jnp.maximum((b * nt + i) * (tile // rows) - 1, 0), 0))
    consts = [convw, convb, lng, lnb, wout]
    skew_rows = SUBLANES * _conv_skew(tile)
    return pl.pallas_call(
        _ev_tail_kernel,
        grid=(batch, nt),
        in_specs=[pl.BlockSpec(memory_space=pltpu.SMEM),
                  tok(D_MODEL), tok(ATT_W), tok(LANES), halo(BLOCK, LANES),
                  tok(LANES), halo(BLOCK, LANES), tok(CONV_CH), halo(CONV_HALO, CONV_CH)]
                 + [_const_spec(t.shape) for t in consts] + _ffn_specs(ffn, layer),
        out_specs=tok(D_MODEL),
        out_shape=jax.ShapeDtypeStruct(x2.shape, F32),
        scratch_shapes=[pltpu.VMEM((tile + BLOCK, LANES), BF16),
                        pltpu.VMEM((tile + BLOCK, LANES), BF16),
                        pltpu.VMEM((CONV_CH // LANES, CONV_HALO + skew_rows, LANES), F32),
                        pltpu.VMEM((CONV_CH // LANES, skew_rows, LANES), F32),
                        pltpu.VMEM((tile, ATT_W + CONV_CH), BF16),
                        pltpu.VMEM((tile, D_FF), BF16),
                        pltpu.VMEM((tile, D_MODEL), F32)],
        compiler_params=_cparams(2),
        name="ev_tail",
    )(sinks, x2, q, k, k, v, v, c, c, *consts, *ffn)


def _branch_kernel(q_ref, k_ref, kh_ref, v_ref, vh_ref, o_ref, lse_ref, kbuf, vbuf):
    i = pl.program_id(2)
    n_res, tile = q_ref.shape[0], q_ref.shape[1]

    band, prev_keys = _band_masks(1)
    first_penalty = jnp.where(i == 0, NEG, 0.0).astype(F32)
    low_lanes = lax.broadcasted_iota(jnp.int32, (BLOCK, LANES), 1) < HEAD_DIM

    for r in range(n_res):
        kbuf[r, 0:BLOCK, :] = kh_ref[r]
        kbuf[r, BLOCK:, :] = k_ref[r]
        vbuf[r, 0:BLOCK, :] = vh_ref[r]
        vbuf[r, BLOCK:, :] = v_ref[r]
        for jb in range(tile // BLOCK):
            rows = slice(jb * BLOCK, (jb + 1) * BLOCK)
            for p in range(ATT_W // LANES):
                lanes = slice(p * LANES, (p + 1) * LANES)
                kk = kbuf[r, jb * BLOCK:(jb + 2) * BLOCK, lanes]
                vv = vbuf[r, jb * BLOCK:(jb + 2) * BLOCK, lanes]
                qp = q_ref[r, rows, lanes]
                zero = jnp.zeros_like(qp)
                q2 = jnp.concatenate([jnp.where(low_lanes, qp, zero),
                                      jnp.where(low_lanes, zero, qp)], axis=0)
                s2 = _dot_nt(q2, kk)
                probs, inv_l, lse = [], [], []
                for half in range(2):
                    pr, m, l = _softmax_rows(s2[half * BLOCK:(half + 1) * BLOCK], band,
                                             prev_keys, first_penalty if jb == 0 else None)
                    probs.append(pr.astype(BF16))
                    inv_l.append(1.0 / l)
                    lse.append(m + jnp.log(l))
                o2 = _dot(jnp.concatenate(probs, axis=0), vv)
                o_ref[r, rows, lanes] = jnp.where(low_lanes, o2[0:BLOCK] * inv_l[0],
                                                  o2[BLOCK:] * inv_l[1])
                lse_ref[r, rows, lanes] = jnp.where(low_lanes, lse[0], lse[1])


def _branch(qkv, batch, seq, dil):
    sub = seq // dil
    tile = min(BRANCH_ROWS, sub)
    n_res = BRANCH_ROWS // tile
    assert dil % n_res == 0
    nt = sub // tile

    def cur(s):
        return pl.BlockSpec((None, None, n_res, tile, ATT_W), lambda b, r, i: (b, s, r, i, 0))

    def halo(s):
        return pl.BlockSpec(
            (None, None, n_res, BLOCK, ATT_W),
            lambda b, r, i: (b, s, r, jnp.maximum(i * (tile // BLOCK) - 1, 0), 0))

    out_spec = pl.BlockSpec((None, n_res, tile, ATT_W), lambda b, r, i: (b, r, i, 0))
    out_sds = jax.ShapeDtypeStruct((batch, dil, sub, ATT_W), F32)
    return pl.pallas_call(
        _branch_kernel,
        grid=(batch, dil // n_res, nt),
        in_specs=[cur(0), cur(1), halo(1), cur(2), halo(2)],
        out_specs=[out_spec, out_spec],
        out_shape=[out_sds, out_sds],
        scratch_shapes=[pltpu.VMEM((n_res, tile + BLOCK, ATT_W), BF16),
                        pltpu.VMEM((n_res, tile + BLOCK, ATT_W), BF16)],
        compiler_params=_cparams(3),
        name=f"od_branch_d{dil}",
    )(qkv, qkv, qkv, qkv, qkv)


def _od_mix_kernel(x_ref, o1_ref, o2_ref, o3_ref, l1_ref, l2_ref, l3_ref, u_ref, gate_ref,
                   sw_ref, sb_ref, wout_ref, out_ref, mixbuf, tok_o2, tok_l2, tok_o3, tok_l3):
    tile = x_ref.shape[0]
    for j in range(ATT_W // LANES):
        sl = slice(j * LANES, (j + 1) * LANES)
        for src, dst in ((o2_ref, tok_o2), (l2_ref, tok_l2), (o3_ref, tok_o3), (l3_ref, tok_l3)):
            dil = src.shape[0]
            for r in range(dil):
                dst[j, pl.ds(r, tile // dil, stride=dil), :] = src[r, :, sl]
        l1, l2, l3 = l1_ref[0, :, sl], tok_l2[j], tok_l3[j]
        m = jnp.maximum(jnp.maximum(l1, l2), l3)
        e1, e2, e3 = jnp.exp(l1 - m), jnp.exp(l2 - m), jnp.exp(l3 - m)
        mixed = (e1 * o1_ref[0, :, sl] + e2 * tok_o2[j] + e3 * tok_o3[j]) / (e1 + e2 + e3)
        mixbuf[:, sl] = mixed.astype(BF16)

    ti = lax.broadcasted_iota(jnp.int32, (CHUNK, CHUNK), 0)
    si = lax.broadcasted_iota(jnp.int32, (CHUNK, CHUNK), 1)
    causal = ti >= si
    low_lanes = lax.broadcasted_iota(jnp.int32, (CHUNK, LANES), 1) < HEAD_DIM
    for p in range(D_CH // LANES):
        lanes = slice(p * LANES, (p + 1) * LANES)
        w2 = jnp.concatenate(
            [jnp.where(causal, sw_ref[2 * p], 0.0), jnp.where(causal, sw_ref[2 * p + 1], 0.0)],
            axis=0).astype(BF16)
        bias = sb_ref[:, lanes]
        for ch in range(tile // CHUNK):
            rows = slice(ch * CHUNK, (ch + 1) * CHUNK)
            y2 = _dot(w2, gate_ref[rows, lanes])
            y = jnp.where(low_lanes, y2[0:CHUNK], y2[CHUNK:]) + bias
            mixbuf[rows, ATT_W + p * LANES:ATT_W + (p + 1) * LANES] = (
                u_ref[rows, lanes] * y).astype(BF16)

    out_ref[...] = x_ref[...] + _dot(mixbuf[...], wout_ref[...])


def _od_tail_kernel(x_ref, o1_ref, o2_ref, o3_ref, l1_ref, l2_ref, l3_ref, u_ref, gate_ref,
                    sw_ref, sb_ref, wout_ref, fng_ref, wg_ref, wu_ref, wd_ref, fg_ref, out_ref,
                    mixbuf, tok_o2, tok_l2, tok_o3, tok_l3, actbuf, h_mid):
    _od_mix_kernel(x_ref, o1_ref, o2_ref, o3_ref, l1_ref, l2_ref, l3_ref, u_ref, gate_ref,
                   sw_ref, sb_ref, wout_ref, h_mid, mixbuf, tok_o2, tok_l2, tok_o3, tok_l3)
    _ffn_kernel(h_mid, fng_ref, wg_ref, wu_ref, wd_ref, fg_ref, out_ref, actbuf, final_norm=True)


def _od_tail(x2, outs, lses, u, gate, sw, sb, wout, ffn, layer, seq):
    tile = TOK_TILE
    per_seq = seq // tile
    tok = lambda w: pl.BlockSpec((tile, w), lambda i: (i, 0))
    folded = [pl.BlockSpec((None, d, tile // d, ATT_W),
                           lambda i: (i // per_seq, 0, i % per_seq, 0)) for d in DILATIONS]
    consts = [sw, sb, wout]
    return pl.pallas_call(
        _od_tail_kernel,
        grid=(x2.shape[0] // tile,),
        in_specs=[tok(D_MODEL)] + folded + folded + [tok(D_CH), tok(D_CH)]
                 + [_const_spec(t.shape) for t in consts] + _ffn_specs(ffn, layer),
        out_specs=tok(D_MODEL),
        out_shape=jax.ShapeDtypeStruct(x2.shape, F32),
        scratch_shapes=[pltpu.VMEM((tile, ATT_W + D_CH), BF16)]
                       + [pltpu.VMEM((ATT_W // LANES, tile, LANES), F32)] * 4
                       + [pltpu.VMEM((tile, D_FF), BF16), pltpu.VMEM((tile, D_MODEL), F32)],
        compiler_params=_cparams(1),
        name="od_tail",
    )(x2, *outs, *lses, u, gate, *consts, *ffn)


def _ffn_kernel(x_ref, g_ref, wg_ref, wu_ref, wd_ref, fg_ref, o_ref, actbuf, *, final_norm):
    hn = _rms(x_ref[...], g_ref[...]).astype(BF16)
    for c in range(D_FF // FF_CHUNK):
        cols = slice(c * FF_CHUNK, (c + 1) * FF_CHUNK)
        gate = _dot(hn, wg_ref[:, cols])
        up = _dot(hn, wu_ref[:, cols])
        actbuf[:, cols] = (gate * jax.nn.sigmoid(gate) * up).astype(BF16)
    y = x_ref[...] + _dot(actbuf[...], wd_ref[...])
    if final_norm:
        y = _rms(y, fg_ref[...])
    o_ref[...] = y


def _cast_kernel(x_ref, o_ref):
    o_ref[...] = x_ref[...].astype(o_ref.dtype)


def _to_bf16(w):
    n_l, rows, cols = w.shape
    blk_rows = rows // CAST_SPLIT
    assert blk_rows * CAST_SPLIT == rows and blk_rows % (2 * SUBLANES) == 0
    blk = pl.BlockSpec((None, blk_rows, cols), lambda l, r: (l, r, 0))
    return pl.pallas_call(
        _cast_kernel,
        grid=(n_l, CAST_SPLIT),
        in_specs=[blk],
        out_specs=blk,
        out_shape=jax.ShapeDtypeStruct(w.shape, BF16),
        compiler_params=_cparams(2),
        name="to_bf16",
    )(w)


def kernel(x, ev_norm_g, ev_w_in, ev_sinks, ev_conv_w, ev_conv_b, ev_conv_ln_g, ev_conv_ln_b, ev_w_out, od_norm_g, od_w_in, od_sgu_ln_g, od_sgu_ln_b, od_spatial_w, od_spatial_b, od_w_out, ffn_norm_g, ffn_w_gate, ffn_w_up, ffn_w_down, final_norm_g):
    batch, seq, _ = x.shape
    assert seq % (max(DILATIONS) * BLOCK) == 0 and seq % FRONT_TILE == 0 and seq % TOK_TILE == 0
    row = lambda t: t.reshape(1, -1).astype(F32)
    x2 = x.reshape(batch * seq, D_MODEL)
    tabs = _rope_tables(seq)

    n_pairs = ATT_W // LANES
    head_order = [h for p in range(n_pairs) for h in (p, n_pairs + p)]
    q_cols = [h * HEAD_DIM + d for h in head_order for d in range(HEAD_DIM)]
    in_cols = np.asarray(q_cols + list(range(ATT_W, ev_w_in.shape[2])), np.int32)
    out_rows = np.asarray(q_cols + list(range(ATT_W, ev_w_out.shape[1])), np.int32)
    w_in0 = jnp.take(ev_w_in[0], in_cols, axis=1).astype(BF16)
    w_out0 = jnp.take(ev_w_out[0], out_rows, axis=0).astype(BF16)
    q, k, v, c = _ev_front(x2, row(ev_norm_g[0]), w_in0, tabs, seq)
    w_gate, w_up, w_down = _to_bf16(ffn_w_gate), _to_bf16(ffn_w_up), _to_bf16(ffn_w_down)
    ffn_w = lambda l: (row(ffn_norm_g[l]), w_gate, w_up, w_down, row(final_norm_g))
    h = _ev_tail(x2, q, k, v, c, ev_sinks[0].astype(F32), ev_conv_w[0].astype(F32),
                 row(ev_conv_b[0]), row(ev_conv_ln_g[0]), row(ev_conv_ln_b[0]), w_out0,
                 ffn_w(0), 0, batch, seq)

    *qkvs, u, gate = _od_front(h, row(od_norm_g[0]), od_w_in[0].astype(BF16), tabs,
                               row(od_sgu_ln_g[0]), row(od_sgu_ln_b[0]), batch, seq)
    outs, lses = [], []
    for dil, qkv in zip(DILATIONS, qkvs):
        o_r, lse_r = _branch(qkv, batch, seq, dil)
        outs.append(o_r)
        lses.append(lse_r)
    sb = jnp.repeat(od_spatial_b[0].astype(F32).T, HEAD_DIM, axis=1)
    h = _od_tail(h, outs, lses, u, gate, od_spatial_w[0].astype(F32), sb,
                 od_w_out[0].astype(BF16), ffn_w(1), 1, seq)
    return h.reshape(batch, seq, D_MODEL)
```

```python
import jax
import jax.numpy as jnp
import numpy as np
from jax import lax
from jax.experimental import pallas as pl
from jax.experimental.pallas import tpu as pltpu

F32 = jnp.float32
BF16 = jnp.bfloat16

D_MODEL = 1024
HEAD_DIM = 64
ROT_DIM = 16
ROPE_THETA = 500000.0
BLOCK = 128
RMS_EPS = 1e-6
LN_EPS = 1e-5
N_HEADS = 8
ATT_W = N_HEADS * HEAD_DIM
CONV_CH = 512
CONV_WIDTH = 31
CONV_HALO = 32
D_CH = 512
D_GROUPS = 8
CHUNK = 128
D_FF = 2816
DILATIONS = (1, 4, 16)
LANES = 128
SUBLANES = 8
NEG = -1e30

FRONT_TILE = 1024
TOK_TILE = 512
BRANCH_ROWS = 2048
FF_CHUNK = 256
CONV_TAPS = 16
CAST_SPLIT = 2
VMEM_LIMIT = 56 * 1024 * 1024


def _cparams(n_axes):
    return pltpu.CompilerParams(
        dimension_semantics=("arbitrary",) * n_axes,
        vmem_limit_bytes=VMEM_LIMIT)


def _const_spec(shape, layer=None):
    nd = len(shape)
    if layer is None:
        return pl.BlockSpec(shape, lambda *_: (0,) * nd, pipeline_mode=pl.Buffered(1))
    return pl.BlockSpec((None,) + tuple(shape[1:]), lambda *_: (layer,) + (0,) * (nd - 1),
                        pipeline_mode=pl.Buffered(1))


def _ffn_specs(ffn, layer):
    fng, wg, wu, wd, fg = ffn
    return [_const_spec(fng.shape)] + [_const_spec(t.shape, layer) for t in (wg, wu, wd)] + [
        _const_spec(fg.shape)]


def _rms(x, g):
    ms = jnp.mean(x * x, axis=-1, keepdims=True)
    return x * lax.rsqrt(ms + RMS_EPS) * g


def _layer_norm(x, g, b):
    mu = jnp.mean(x, axis=-1, keepdims=True)
    xc = x - mu
    var = jnp.mean(xc * xc, axis=-1, keepdims=True)
    return xc * lax.rsqrt(var + LN_EPS) * g + b


def _dot(a, b):
    return jnp.dot(a, b, preferred_element_type=F32)


def _dot_nt(a, b):
    return lax.dot_general(a, b, (((1,), (1,)), ((), ())), preferred_element_type=F32)


def _rope(t, cos, sin_lo, sin_hi):
    up = pltpu.roll(t, LANES - ROT_DIM // 2, 1)
    down = pltpu.roll(t, ROT_DIM // 2, 1)
    return t * cos + up * sin_lo + down * sin_hi


def _rope_tables(seq):
    half = ROT_DIM // 2
    inv_freq = ROPE_THETA ** (-np.arange(half, dtype=np.float64) * (2.0 / ROT_DIM))
    ang = np.arange(seq, dtype=np.float64)[:, None] * inv_freq[None, :]
    cos, sin = np.cos(ang), np.sin(ang)
    ones = np.ones((seq, HEAD_DIM - ROT_DIM))
    zeros = np.zeros((seq, HEAD_DIM - ROT_DIM))
    zh = np.zeros((seq, half))
    cos_h = np.concatenate([cos, cos, ones], axis=1)
    lo_h = np.concatenate([-sin, zh, zeros], axis=1)
    hi_h = np.concatenate([zh, sin, zeros], axis=1)
    two = lambda t: jnp.asarray(np.concatenate([t, t], axis=1), F32)
    return two(cos_h), two(lo_h), two(hi_h)


def _ev_front_kernel(x_ref, g_ref, w_ref, cos_ref, lo_ref, hi_ref,
                     q_ref, k_ref, v_ref, c_ref):
    hn = _rms(x_ref[...], g_ref[...]).astype(BF16)
    cos, lo, hi = cos_ref[...], lo_ref[...], hi_ref[...]
    qkv = _dot(hn, w_ref[:, 0:ATT_W + 2 * LANES])
    for j in range(ATT_W // LANES):
        t = qkv[:, j * LANES:(j + 1) * LANES]
        q_ref[:, j * LANES:(j + 1) * LANES] = (
            _rope(t, cos, lo, hi) * (HEAD_DIM ** -0.5)).astype(BF16)
    k_ref[...] = _rope(qkv[:, ATT_W:ATT_W + LANES], cos, lo, hi).astype(BF16)
    v_ref[...] = qkv[:, ATT_W + LANES:ATT_W + 2 * LANES].astype(BF16)
    base = ATT_W + 2 * LANES
    ga = _dot(hn, w_ref[:, base:base + CONV_CH])
    gb = _dot(hn, w_ref[:, base + CONV_CH:base + 2 * CONV_CH])
    c_ref[...] = ga * jax.nn.sigmoid(gb)


def _od_front_kernel(x_ref, g_ref, w_ref, cos_ref, lo_ref, hi_ref, lng_ref, lnb_ref,
                     qkv1_ref, qkv4_ref, qkv16_ref, u_ref, gate_ref, stage, stage4, raw):
    tile = x_ref.shape[0]
    half = tile // 2
    d4, d16 = DILATIONS[1], DILATIONS[2]
    n_out = w_ref.shape[1]

    def project(h):
        rows = slice(h * half, (h + 1) * half)
        hn = _rms(x_ref[rows, :], g_ref[...]).astype(BF16)
        for c in range(n_out // ATT_W):
            cols = slice(c * ATT_W, (c + 1) * ATT_W)
            raw[h, :, cols] = _dot(hn, w_ref[:, cols])

    def finish(h):
        rows = slice(h * half, (h + 1) * half)
        cos, lo, hi = cos_ref[rows, :], lo_ref[rows, :], hi_ref[rows, :]
        zg = jax.nn.gelu(raw[h, :, 3 * ATT_W + D_CH:3 * ATT_W + 2 * D_CH])
        gate_ref[rows, :] = _layer_norm(zg, lng_ref[...], lnb_ref[...]).astype(BF16)
        u_ref[rows, :] = jax.nn.gelu(raw[h, :, 3 * ATT_W:3 * ATT_W + D_CH])
        for s in (2, 1, 0):
            for j in range(ATT_W // LANES):
                sl = slice(j * LANES, (j + 1) * LANES)
                t = raw[h, :, s * ATT_W + j * LANES:s * ATT_W + (j + 1) * LANES]
                buf, buf4 = stage.at[s, j], stage4.at[s, j]
                if s == 0:
                    buf[...] = _rope(t, cos, lo, hi) * (HEAD_DIM ** -0.5)
                elif s == 1:
                    buf[...] = _rope(t, cos, lo, hi)
                else:
                    buf[...] = t
                qkv1_ref[s, 0, rows, sl] = buf[...].astype(BF16)
                for r in range(d4):
                    rows4 = slice(r * (half // d4), (r + 1) * (half // d4))
                    buf4[rows4, :] = buf[pl.ds(r, half // d4, stride=d4), :]
                    qkv4_ref[s, r, h * (half // d4):(h + 1) * (half // d4), sl] = (
                        buf4[rows4, :].astype(BF16))
                for r in range(d16):
                    r4, a = r % d4, r // d4
                    qkv16_ref[s, r, h * (half // d16):(h + 1) * (half // d16), sl] = buf4[
                        pl.ds(r4 * (half // d4) + a, half // d16, stride=d16 // d4), :].astype(BF16)

    project(0)
    finish(0)
    project(1)
    finish(1)


def _front_common(x2, seq):
    n_tok = x2.shape[0]
    nt = n_tok // FRONT_TILE
    per_seq = seq // FRONT_TILE
    x_spec = pl.BlockSpec((FRONT_TILE, D_MODEL), lambda i: (i, 0))
    tab_spec = pl.BlockSpec((FRONT_TILE, LANES), lambda i: (i % per_seq, 0))
    out = lambda w, dt: (jax.ShapeDtypeStruct((n_tok, w), dt),
                         pl.BlockSpec((FRONT_TILE, w), lambda i: (i, 0)))
    return nt, x_spec, tab_spec, out


def _ev_front(x2, g, w, tabs, seq):
    nt, x_spec, tab_spec, out = _front_common(x2, seq)
    outs = [out(ATT_W, BF16), out(LANES, BF16), out(LANES, BF16), out(CONV_CH, F32)]
    return pl.pallas_call(
        _ev_front_kernel,
        grid=(nt,),
        in_specs=[x_spec, _const_spec(g.shape), _const_spec(w.shape),
                  tab_spec, tab_spec, tab_spec],
        out_specs=[o[1] for o in outs],
        out_shape=[o[0] for o in outs],
        compiler_params=_cparams(1),
        name="ev_front",
    )(x2, g, w, *tabs)


def _od_front(x2, g, w, tabs, lng, lnb, batch, seq):
    nt, x_spec, tab_spec, out = _front_common(x2, seq)
    per_seq = seq // FRONT_TILE

    def folded(dil):
        return (jax.ShapeDtypeStruct((batch, 3, dil, seq // dil, ATT_W), BF16),
                pl.BlockSpec((None, 3, dil, FRONT_TILE // dil, ATT_W),
                             lambda i: (i // per_seq, 0, 0, i % per_seq, 0)))

    outs = [folded(d) for d in DILATIONS] + [out(D_CH, F32), out(D_CH, BF16)]
    return pl.pallas_call(
        _od_front_kernel,
        grid=(nt,),
        in_specs=[x_spec, _const_spec(g.shape), _const_spec(w.shape),
                  tab_spec, tab_spec, tab_spec,
                  _const_spec(lng.shape), _const_spec(lnb.shape)],
        out_specs=[o[1] for o in outs],
        out_shape=[o[0] for o in outs],
        scratch_shapes=[pltpu.VMEM((3, ATT_W // LANES, FRONT_TILE // 2, LANES), F32)] * 2
                       + [pltpu.VMEM((2, FRONT_TILE // 2, w.shape[1]), F32)],
        compiler_params=_cparams(1),
        name="od_front",
    )(x2, g, w, *tabs, lng, lnb)


def _band_masks(min_dist):
    qi = lax.broadcasted_iota(jnp.int32, (BLOCK, 2 * BLOCK), 0)
    kj = lax.broadcasted_iota(jnp.int32, (BLOCK, 2 * BLOCK), 1)
    dist = qi + BLOCK - kj
    band = (dist >= 0) & (dist <= BLOCK - 1 + min_dist)
    return band, kj < BLOCK


def _softmax_rows(s, band, prev_keys, prev_penalty):
    s = jnp.where(band, s, NEG)
    if prev_penalty is not None:
        s = jnp.where(prev_keys, s + prev_penalty, s)
    m = jnp.max(s, axis=-1, keepdims=True)
    p = jnp.exp(s - m)
    l = jnp.sum(p, axis=-1, keepdims=True)
    return p, m, l


def _conv_skew(tile):
    skew = tile // SUBLANES + 1
    assert skew % 2 == 1
    return skew


def _ev_mix_kernel(sink_ref, x_ref, q_ref, k_ref, kh_ref, v_ref, vh_ref, c_ref, ch_ref,
                   convw_ref, convb_ref, lng_ref, lnb_ref, wout_ref, o_ref,
                   kbuf, vbuf, cbuf, cout, mixbuf):
    i = pl.program_id(1)
    tile = q_ref.shape[0]
    n_pairs = ATT_W // LANES

    kbuf[0:BLOCK, :] = kh_ref[...]
    kbuf[BLOCK:, :] = k_ref[...]
    vbuf[0:BLOCK, :] = vh_ref[...]
    vbuf[BLOCK:, :] = v_ref[...]

    row = lax.broadcasted_iota(jnp.int32, (BLOCK, BLOCK), 0)
    col = lax.broadcasted_iota(jnp.int32, (BLOCK, BLOCK), 1)
    own = col <= row
    first_penalty = jnp.where(i == 0, NEG, 0.0).astype(F32)
    low_lanes = lax.broadcasted_iota(jnp.int32, (BLOCK, LANES), 1) < HEAD_DIM

    for jb in range(tile // BLOCK):
        rows = slice(jb * BLOCK, (jb + 1) * BLOCK)
        kk = kbuf[jb * BLOCK:(jb + 2) * BLOCK, :]
        vv = vbuf[jb * BLOCK:(jb + 2) * BLOCK, :]
        parts = []
        for half in range(2):
            keep = low_lanes if half == 0 else jnp.logical_not(low_lanes)
            for p in range(n_pairs):
                qp = q_ref[rows, p * LANES:(p + 1) * LANES]
                parts.append(jnp.where(keep, qp, jnp.zeros_like(qp)))
        s_all = _dot_nt(jnp.concatenate(parts, axis=0), kk)
        probs, inv_l = [], []
        for idx in range(2 * n_pairs):
            s_prev = s_all[idx * BLOCK:(idx + 1) * BLOCK, 0:BLOCK]
            s_own = s_all[idx * BLOCK:(idx + 1) * BLOCK, BLOCK:]
            if jb == 0:
                s_prev = s_prev + first_penalty
            sc = jnp.where(own, s_own, s_prev)
            sink = sink_ref[idx]
            m = jnp.maximum(jnp.max(sc, axis=-1, keepdims=True), sink)
            pr = jnp.exp(sc - m)
            l = jnp.sum(pr, axis=-1, keepdims=True) + jnp.exp(sink - m)
            zero = jnp.zeros_like(pr)
            probs.append(jnp.concatenate([jnp.where(own, zero, pr), jnp.where(own, pr, zero)],
                                         axis=1).astype(BF16))
            inv_l.append(1.0 / l)
        o_all = _dot(jnp.concatenate(probs, axis=0), vv)
        for p in range(n_pairs):
            o_lo = o_all[p * BLOCK:(p + 1) * BLOCK] * inv_l[p]
            o_hi = o_all[(n_pairs + p) * BLOCK:(n_pairs + p + 1) * BLOCK] * inv_l[n_pairs + p]
            mixbuf[rows, p * LANES:(p + 1) * LANES] = jnp.where(low_lanes, o_lo, o_hi).astype(BF16)

    n_lg = CONV_CH // LANES
    skew = _conv_skew(tile)
    lead = CONV_HALO - (CONV_WIDTH - 1)
    for lg in range(n_lg):
        lanes = slice(lg * LANES, (lg + 1) * LANES)
        cbuf[lg, 0:CONV_HALO, :] = jnp.where(i > 0, ch_ref[:, lanes], 0.0)
        cbuf[lg, CONV_HALO:CONV_HALO + tile, :] = c_ref[:, lanes]
        cbuf[lg, CONV_HALO + tile:, :] = jnp.zeros((cbuf.shape[1] - CONV_HALO - tile, LANES), F32)
    skewed = lambda ref, lg, start: ref.at[lg, pl.ds(start, SUBLANES, stride=skew), :]
    one_trip = jnp.minimum(i, 0) + 1

    def conv_lane_group(lg):
        lanes = slice(lg * LANES, (lg + 1) * LANES)
        for t0 in range(0, CONV_WIDTH, CONV_TAPS):
            n_taps = min(CONV_TAPS, CONV_WIDTH - t0)
            w = [jnp.broadcast_to(convw_ref[t0 + tt:t0 + tt + 1, lanes], (SUBLANES, LANES))
                 for tt in range(n_taps)]
            acc = {}
            for o in range(skew + n_taps - 1):
                xs = skewed(cbuf, lg, o + lead + t0)[...]
                for tt in range(n_taps):
                    j = o - tt
                    if not 0 <= j < skew:
                        continue
                    if tt == 0:
                        acc[j] = (jnp.broadcast_to(convb_ref[:, lanes], (SUBLANES, LANES))
                                  if t0 == 0 else skewed(cout, lg, j)[...])
                    acc[j] = acc[j] + xs * w[tt]
                    if tt == n_taps - 1:
                        skewed(cout, lg, j)[...] = acc.pop(j)

    for lg in range(n_lg):
        lax.fori_loop(0, one_trip, lambda _, carry, lg=lg: (conv_lane_group(lg), carry)[1], 0)
    for j in range(skew):
        y = _layer_norm(jnp.concatenate([skewed(cout, lg, j)[...] for lg in range(n_lg)], axis=1),
                        lng_ref[...], lnb_ref[...])
        y = y * jax.nn.sigmoid(y)
        for lg in range(n_lg):
            skewed(cout, lg, j)[...] = y[:, lg * LANES:(lg + 1) * LANES]
    for lg in range(n_lg):
        mixbuf[:, ATT_W + lg * LANES:ATT_W + (lg + 1) * LANES] = cout[lg, 0:tile, :].astype(BF16)

    o_ref[...] = x_ref[...] + _dot(mixbuf[...], wout_ref[...])


def _ev_tail_kernel(sink_ref, x_ref, q_ref, k_ref, kh_ref, v_ref, vh_ref, c_ref, ch_ref,
                    convw_ref, convb_ref, lng_ref, lnb_ref, wout_ref,
                    fng_ref, wg_ref, wu_ref, wd_ref, fg_ref, o_ref,
                    kbuf, vbuf, cbuf, cout, mixbuf, actbuf, h_mid):
    _ev_mix_kernel(sink_ref, x_ref, q_ref, k_ref, kh_ref, v_ref, vh_ref, c_ref, ch_ref,
                   convw_ref, convb_ref, lng_ref, lnb_ref, wout_ref, h_mid,
                   kbuf, vbuf, cbuf, cout, mixbuf)
    _ffn_kernel(h_mid, fng_ref, wg_ref, wu_ref, wd_ref, fg_ref, o_ref, actbuf, final_norm=False)


def _ev_tail(x2, q, k, v, c, sinks, convw, convb, lng, lnb, wout, ffn, layer, batch, seq):
    tile = TOK_TILE
    nt = seq // tile
    tok = lambda w: pl.BlockSpec((tile, w), lambda b, i: (b * nt + i, 0))
    halo = lambda rows, w: pl.BlockSpec(
        (rows, w), lambda b, i: (jnp.maximum((b * nt + i) * (tile // rows) - 1, 0), 0))
    consts = [convw, convb, lng, lnb, wout]
    skew_rows = SUBLANES * _conv_skew(tile)
    return pl.pallas_call(
        _ev_tail_kernel,
        grid=(batch, nt),
        in_specs=[pl.BlockSpec(memory_space=pltpu.SMEM),
                  tok(D_MODEL), tok(ATT_W), tok(LANES), halo(BLOCK, LANES),
                  tok(LANES), halo(BLOCK, LANES), tok(CONV_CH), halo(CONV_HALO, CONV_CH)]
                 + [_const_spec(t.shape) for t in consts] + _ffn_specs(ffn, layer),
        out_specs=tok(D_MODEL),
        out_shape=jax.ShapeDtypeStruct(x2.shape, F32),
        scratch_shapes=[pltpu.VMEM((tile + BLOCK, LANES), BF16),
                        pltpu.VMEM((tile + BLOCK, LANES), BF16),
                        pltpu.VMEM((CONV_CH // LANES, CONV_HALO + skew_rows, LANES), F32),
                        pltpu.VMEM((CONV_CH // LANES, skew_rows, LANES), F32),
                        pltpu.VMEM((tile, ATT_W + CONV_CH), BF16),
                        pltpu.VMEM((tile, D_FF), BF16),
                        pltpu.VMEM((tile, D_MODEL), F32)],
        compiler_params=_cparams(2),
        name="ev_tail",
    )(sinks, x2, q, k, k, v, v, c, c, *consts, *ffn)


def _branch_kernel(q_ref, k_ref, kh_ref, v_ref, vh_ref, o_ref, lse_ref, kbuf, vbuf):
    i = pl.program_id(2)
    n_res, tile = q_ref.shape[0], q_ref.shape[1]

    band, prev_keys = _band_masks(1)
    first_penalty = jnp.where(i == 0, NEG, 0.0).astype(F32)
    low_lanes = lax.broadcasted_iota(jnp.int32, (BLOCK, LANES), 1) < HEAD_DIM

    for r in range(n_res):
        kbuf[r, 0:BLOCK, :] = kh_ref[r]
        kbuf[r, BLOCK:, :] = k_ref[r]
        vbuf[r, 0:BLOCK, :] = vh_ref[r]
        vbuf[r, BLOCK:, :] = v_ref[r]
        for jb in range(tile // BLOCK):
            rows = slice(jb * BLOCK, (jb + 1) * BLOCK)
            for p in range(ATT_W // LANES):
                lanes = slice(p * LANES, (p + 1) * LANES)
                kk = kbuf[r, jb * BLOCK:(jb + 2) * BLOCK, lanes]
                vv = vbuf[r, jb * BLOCK:(jb + 2) * BLOCK, lanes]
                qp = q_ref[r, rows, lanes]
                zero = jnp.zeros_like(qp)
                q2 = jnp.concatenate([jnp.where(low_lanes, qp, zero),
                                      jnp.where(low_lanes, zero, qp)], axis=0)
                s2 = _dot_nt(q2, kk)
                probs, inv_l, lse = [], [], []
                for half in range(2):
                    pr, m, l = _softmax_rows(s2[half * BLOCK:(half + 1) * BLOCK], band,
                                             prev_keys, first_penalty if jb == 0 else None)
                    probs.append(pr.astype(BF16))
                    inv_l.append(1.0 / l)
                    lse.append(m + jnp.log(l))
                o2 = _dot(jnp.concatenate(probs, axis=0), vv)
                o_ref[r, rows, lanes] = jnp.where(low_lanes, o2[0:BLOCK] * inv_l[0],
                                                  o2[BLOCK:] * inv_l[1])
                lse_ref[r, rows, lanes] = jnp.where(low_lanes, lse[0], lse[1])


def _branch(qkv, batch, seq, dil):
    sub = seq // dil
    tile = min(BRANCH_ROWS, sub)
    n_res = BRANCH_ROWS // tile
    assert dil % n_res == 0
    nt = sub // tile

    def cur(s):
        return pl.BlockSpec((None, None, n_res, tile, ATT_W), lambda b, r, i: (b, s, r, i, 0))

    def halo(s):
        return pl.BlockSpec(
            (None, None, n_res, BLOCK, ATT_W),
            lambda b, r, i: (b, s, r, jnp.maximum(i * (tile // BLOCK) - 1, 0), 0))

    out_spec = pl.BlockSpec((None, n_res, tile, ATT_W), lambda b, r, i: (b, r, i, 0))
    out_sds = jax.ShapeDtypeStruct((batch, dil, sub, ATT_W), F32)
    return pl.pallas_call(
        _branch_kernel,
        grid=(batch, dil // n_res, nt),
        in_specs=[cur(0), cur(1), halo(1), cur(2), halo(2)],
        out_specs=[out_spec, out_spec],
        out_shape=[out_sds, out_sds],
        scratch_shapes=[pltpu.VMEM((n_res, tile + BLOCK, ATT_W), BF16),
                        pltpu.VMEM((n_res, tile + BLOCK, ATT_W), BF16)],
        compiler_params=_cparams(3),
        name=f"od_branch_d{dil}",
    )(qkv, qkv, qkv, qkv, qkv)


def _od_mix_kernel(x_ref, o1_ref, o2_ref, o3_ref, l1_ref, l2_ref, l3_ref, u_ref, gate_ref,
                   sw_ref, sb_ref, wout_ref, out_ref, mixbuf, tok_o2, tok_l2, tok_o3, tok_l3):
    tile = x_ref.shape[0]
    for j in range(ATT_W // LANES):
        sl = slice(j * LANES, (j + 1) * LANES)
        for src, dst in ((o2_ref, tok_o2), (l2_ref, tok_l2), (o3_ref, tok_o3), (l3_ref, tok_l3)):
            dil = src.shape[0]
            for r in range(dil):
                dst[j, pl.ds(r, tile // dil, stride=dil), :] = src[r, :, sl]
        l1, l2, l3 = l1_ref[0, :, sl], tok_l2[j], tok_l3[j]
        m = jnp.maximum(jnp.maximum(l1, l2), l3)
        e1, e2, e3 = jnp.exp(l1 - m), jnp.exp(l2 - m), jnp.exp(l3 - m)
        mixed = (e1 * o1_ref[0, :, sl] + e2 * tok_o2[j] + e3 * tok_o3[j]) / (e1 + e2 + e3)
        mixbuf[:, sl] = mixed.astype(BF16)

    ti = lax.broadcasted_iota(jnp.int32, (CHUNK, CHUNK), 0)
    si = lax.broadcasted_iota(jnp.int32, (CHUNK, CHUNK), 1)
    causal = ti >= si
    low_lanes = lax.broadcasted_iota(jnp.int32, (CHUNK, LANES), 1) < HEAD_DIM
    for p in range(D_CH // LANES):
        lanes = slice(p * LANES, (p + 1) * LANES)
        w2 = jnp.concatenate(
            [jnp.where(causal, sw_ref[2 * p], 0.0), jnp.where(causal, sw_ref[2 * p + 1], 0.0)],
            axis=0).astype(BF16)
        bias = sb_ref[:, lanes]
        for ch in range(tile // CHUNK):
            rows = slice(ch * CHUNK, (ch + 1) * CHUNK)
            y2 = _dot(w2, gate_ref[rows, lanes])
            y = jnp.where(low_lanes, y2[0:CHUNK], y2[CHUNK:]) + bias
            mixbuf[rows, ATT_W + p * LANES:ATT_W + (p + 1) * LANES] = (
                u_ref[rows, lanes] * y).astype(BF16)

    out_ref[...] = x_ref[...] + _dot(mixbuf[...], wout_ref[...])


def _od_tail_kernel(x_ref, o1_ref, o2_ref, o3_ref, l1_ref, l2_ref, l3_ref, u_ref, gate_ref,
                    sw_ref, sb_ref, wout_ref, fng_ref, wg_ref, wu_ref, wd_ref, fg_ref, out_ref,
                    mixbuf, tok_o2, tok_l2, tok_o3, tok_l3, actbuf, h_mid):
    _od_mix_kernel(x_ref, o1_ref, o2_ref, o3_ref, l1_ref, l2_ref, l3_ref, u_ref, gate_ref,
                   sw_ref, sb_ref, wout_ref, h_mid, mixbuf, tok_o2, tok_l2, tok_o3, tok_l3)
    _ffn_kernel(h_mid, fng_ref, wg_ref, wu_ref, wd_ref, fg_ref, out_ref, actbuf, final_norm=True)


def _od_tail(x2, outs, lses, u, gate, sw, sb, wout, ffn, layer, seq):
    tile = TOK_TILE
    per_seq = seq // tile
    tok = lambda w: pl.BlockSpec((tile, w), lambda i: (i, 0))
    folded = [pl.BlockSpec((None, d, tile // d, ATT_W),
                           lambda i: (i // per_seq, 0, i % per_seq, 0)) for d in DILATIONS]
    consts = [sw, sb, wout]
    return pl.pallas_call(
        _od_tail_kernel,
        grid=(x2.shape[0] // tile,),
        in_specs=[tok(D_MODEL)] + folded + folded + [tok(D_CH), tok(D_CH)]
                 + [_const_spec(t.shape) for t in consts] + _ffn_specs(ffn, layer),
        out_specs=tok(D_MODEL),
        out_shape=jax.ShapeDtypeStruct(x2.shape, F32),
        scratch_shapes=[pltpu.VMEM((tile, ATT_W + D_CH), BF16)]
                       + [pltpu.VMEM((ATT_W // LANES, tile, LANES), F32)] * 4
                       + [pltpu.VMEM((tile, D_FF), BF16), pltpu.VMEM((tile, D_MODEL), F32)],
        compiler_params=_cparams(1),
        name="od_tail",
    )(x2, *outs, *lses, u, gate, *consts, *ffn)


def _ffn_kernel(x_ref, g_ref, wg_ref, wu_ref, wd_ref, fg_ref, o_ref, actbuf, *, final_norm):
    hn = _rms(x_ref[...], g_ref[...]).astype(BF16)
    for c in range(D_FF // FF_CHUNK):
        cols = slice(c * FF_CHUNK, (c + 1) * FF_CHUNK)
        gate = _dot(hn, wg_ref[:, cols])
        up = _dot(hn, wu_ref[:, cols])
        actbuf[:, cols] = (gate * jax.nn.sigmoid(gate) * up).astype(BF16)
    y = x_ref[...] + _dot(actbuf[...], wd_ref[...])
    if final_norm:
        y = _rms(y, fg_ref[...])
    o_ref[...] = y


def _cast_kernel(x_ref, o_ref):
    o_ref[...] = x_ref[...].astype(o_ref.dtype)


def _to_bf16(w):
    n_l, rows, cols = w.shape
    blk_rows = rows // CAST_SPLIT
    assert blk_rows * CAST_SPLIT == rows and blk_rows % (2 * SUBLANES) == 0
    blk = pl.BlockSpec((None, blk_rows, cols), lambda l, r: (l, r, 0))
    return pl.pallas_call(
        _cast_kernel,
        grid=(n_l, CAST_SPLIT),
        in_specs=[blk],
        out_specs=blk,
        out_shape=jax.ShapeDtypeStruct(w.shape, BF16),
        compiler_params=_cparams(2),
        name="to_bf16",
    )(w)


def kernel(x, ev_norm_g, ev_w_in, ev_sinks, ev_conv_w, ev_conv_b, ev_conv_ln_g, ev_conv_ln_b, ev_w_out, od_norm_g, od_w_in, od_sgu_ln_g, od_sgu_ln_b, od_spatial_w, od_spatial_b, od_w_out, ffn_norm_g, ffn_w_gate, ffn_w_up, ffn_w_down, final_norm_g):
    batch, seq, _ = x.shape
    assert seq % (max(DILATIONS) * BLOCK) == 0 and seq % FRONT_TILE == 0 and seq % TOK_TILE == 0
    row = lambda t: t.reshape(1, -1).astype(F32)
    x2 = x.reshape(batch * seq, D_MODEL)
    tabs = _rope_tables(seq)

    n_pairs = ATT_W // LANES
    head_order = [h for p in range(n_pairs) for h in (p, n_pairs + p)]
    q_cols = [h * HEAD_DIM + d for h in head_order for d in range(HEAD_DIM)]
    in_cols = np.asarray(q_cols + list(range(ATT_W, ev_w_in.shape[2])), np.int32)
    out_rows = np.asarray(q_cols + list(range(ATT_W, ev_w_out.shape[1])), np.int32)
    w_in0 = jnp.take(ev_w_in[0], in_cols, axis=1).astype(BF16)
    w_out0 = jnp.take(ev_w_out[0], out_rows, axis=0).astype(BF16)
    q, k, v, c = _ev_front(x2, row(ev_norm_g[0]), w_in0, tabs, seq)
    w_gate, w_up, w_down = _to_bf16(ffn_w_gate), _to_bf16(ffn_w_up), _to_bf16(ffn_w_down)
    ffn_w = lambda l: (row(ffn_norm_g[l]), w_gate, w_up, w_down, row(final_norm_g))
    h = _ev_tail(x2, q, k, v, c, ev_sinks[0].astype(F32), ev_conv_w[0].astype(F32),
                 row(ev_conv_b[0]), row(ev_conv_ln_g[0]), row(ev_conv_ln_b[0]), w_out0,
                 ffn_w(0), 0, batch, seq)

    *qkvs, u, gate = _od_front(h, row(od_norm_g[0]), od_w_in[0].astype(BF16), tabs,
                               row(od_sgu_ln_g[0]), row(od_sgu_ln_b[0]), batch, seq)
    outs, lses = [], []
    for dil, qkv in zip(DILATIONS, qkvs):
        o_r, lse_r = _branch(qkv, batch, seq, dil)
        outs.append(o_r)
        lses.append(lse_r)
    sb = jnp.repeat(od_spatial_b[0].astype(F32).T, HEAD_DIM, axis=1)
    h = _od_tail(h, outs, lses, u, gate, od_spatial_w[0].astype(F32), sb,
                 od_w_out[0].astype(BF16), ffn_w(1), 1, seq)
    return h.reshape(batch, seq, D_MODEL)
```

```python
import jax
import jax.numpy as jnp
import numpy as np
from jax import lax
from jax.experimental import pallas as pl
from jax.experimental.pallas import tpu as pltpu

F32 = jnp.float32
BF16 = jnp.bfloat16

D_MODEL = 1024
HEAD_DIM = 64
ROT_DIM = 16
ROPE_THETA = 500000.0
BLOCK = 128
RMS_EPS = 1e-6
LN_EPS = 1e-5
N_HEADS = 8
ATT_W = N_HEADS * HEAD_DIM
CONV_CH = 512
CONV_WIDTH = 31
CONV_HALO = 32
D_CH = 512
D_GROUPS = 8
CHUNK = 128
D_FF = 2816
DILATIONS = (1, 4, 16)
LANES = 128
SUBLANES = 8
NEG = -1e30

FRONT_TILE = 1024
FRONT_PARTS = 4
TOK_TILE = 512
BRANCH_ROWS = 2048
FF_CHUNK = 256
CONV_TAPS = 16
CAST_SPLIT = 2
VMEM_LIMIT = 56 * 1024 * 1024


def _cparams(n_axes):
    return pltpu.CompilerParams(
        dimension_semantics=("arbitrary",) * n_axes,
        vmem_limit_bytes=VMEM_LIMIT)


def _const_spec(shape, layer=None):
    nd = len(shape)
    if layer is None:
        return pl.BlockSpec(shape, lambda *_: (0,) * nd, pipeline_mode=pl.Buffered(1))
    return pl.BlockSpec((None,) + tuple(shape[1:]), lambda *_: (layer,) + (0,) * (nd - 1),
                        pipeline_mode=pl.Buffered(1))


def _ffn_specs(ffn, layer):
    fng, wg, wu, wd, fg = ffn
    return [_const_spec(fng.shape)] + [_const_spec(t.shape, layer) for t in (wg, wu, wd)] + [
        _const_spec(fg.shape)]


def _rms(x, g):
    ms = jnp.mean(x * x, axis=-1, keepdims=True)
    return x * lax.rsqrt(ms + RMS_EPS) * g


def _layer_norm(x, g, b):
    mu = jnp.mean(x, axis=-1, keepdims=True)
    xc = x - mu
    var = jnp.mean(xc * xc, axis=-1, keepdims=True)
    return xc * lax.rsqrt(var + LN_EPS) * g + b


def _dot(a, b):
    return jnp.dot(a, b, preferred_element_type=F32)


def _dot_nt(a, b):
    return lax.dot_general(a, b, (((1,), (1,)), ((), ())), preferred_element_type=F32)


def _rope(t, cos, sin_lo, sin_hi):
    up = pltpu.roll(t, LANES - ROT_DIM // 2, 1)
    down = pltpu.roll(t, ROT_DIM // 2, 1)
    return t * cos + up * sin_lo + down * sin_hi


def _rope_tables(seq):
    half = ROT_DIM // 2
    inv_freq = ROPE_THETA ** (-np.arange(half, dtype=np.float64) * (2.0 / ROT_DIM))
    ang = np.arange(seq, dtype=np.float64)[:, None] * inv_freq[None, :]
    cos, sin = np.cos(ang), np.sin(ang)
    ones = np.ones((seq, HEAD_DIM - ROT_DIM))
    zeros = np.zeros((seq, HEAD_DIM - ROT_DIM))
    zh = np.zeros((seq, half))
    cos_h = np.concatenate([cos, cos, ones], axis=1)
    lo_h = np.concatenate([-sin, zh, zeros], axis=1)
    hi_h = np.concatenate([zh, sin, zeros], axis=1)
    two = lambda t: jnp.asarray(np.concatenate([t, t], axis=1), F32)
    return two(cos_h), two(lo_h), two(hi_h)


def _ev_front_kernel(x_ref, g_ref, w_ref, cos_ref, lo_ref, hi_ref,
                     q_ref, k_ref, v_ref, c_ref, raw):
    n_parts = raw.shape[0]
    part = x_ref.shape[0] // n_parts
    base = ATT_W + 2 * LANES

    def project(h):
        rows = slice(h * part, (h + 1) * part)
        hn = _rms(x_ref[rows, :], g_ref[...]).astype(BF16)
        raw[h, :, 0:base] = _dot(hn, w_ref[:, 0:base])
        raw[h, :, base:base + CONV_CH] = _dot(hn, w_ref[:, base:base + CONV_CH])
        raw[h, :, base + CONV_CH:] = _dot(hn, w_ref[:, base + CONV_CH:base + 2 * CONV_CH])

    def finish(h):
        rows = slice(h * part, (h + 1) * part)
        cos, lo, hi = cos_ref[rows, :], lo_ref[rows, :], hi_ref[rows, :]
        for j in range(ATT_W // LANES):
            t = raw[h, :, j * LANES:(j + 1) * LANES]
            q_ref[rows, j * LANES:(j + 1) * LANES] = (
                _rope(t, cos, lo, hi) * (HEAD_DIM ** -0.5)).astype(BF16)
        k_ref[rows, :] = _rope(raw[h, :, ATT_W:ATT_W + LANES], cos, lo, hi).astype(BF16)
        v_ref[rows, :] = raw[h, :, ATT_W + LANES:base].astype(BF16)
        c_ref[rows, :] = raw[h, :, base:base + CONV_CH] * jax.nn.sigmoid(raw[h, :, base + CONV_CH:])

    for h in range(n_parts):
        project(h)
        finish(h)


def _od_front_kernel(x_ref, g_ref, w_ref, cos_ref, lo_ref, hi_ref, lng_ref, lnb_ref,
                     qkv1_ref, qkv4_ref, qkv16_ref, u_ref, gate_ref, stage, stage4, raw):
    tile = x_ref.shape[0]
    n_parts = raw.shape[0]
    half = tile // n_parts
    d4, d16 = DILATIONS[1], DILATIONS[2]
    n_out = w_ref.shape[1]

    def project(h):
        rows = slice(h * half, (h + 1) * half)
        hn = _rms(x_ref[rows, :], g_ref[...]).astype(BF16)
        for c in range(n_out // ATT_W):
            cols = slice(c * ATT_W, (c + 1) * ATT_W)
            raw[h, :, cols] = _dot(hn, w_ref[:, cols])

    def finish(h):
        rows = slice(h * half, (h + 1) * half)
        cos, lo, hi = cos_ref[rows, :], lo_ref[rows, :], hi_ref[rows, :]
        zg = jax.nn.gelu(raw[h, :, 3 * ATT_W + D_CH:3 * ATT_W + 2 * D_CH])
        gate_ref[rows, :] = _layer_norm(zg, lng_ref[...], lnb_ref[...]).astype(BF16)
        u_ref[rows, :] = jax.nn.gelu(raw[h, :, 3 * ATT_W:3 * ATT_W + D_CH])
        for s in (2, 1, 0):
            for j in range(ATT_W // LANES):
                sl = slice(j * LANES, (j + 1) * LANES)
                t = raw[h, :, s * ATT_W + j * LANES:s * ATT_W + (j + 1) * LANES]
                buf, buf4 = stage.at[s, j], stage4.at[s, j]
                if s == 0:
                    buf[...] = _rope(t, cos, lo, hi) * (HEAD_DIM ** -0.5)
                elif s == 1:
                    buf[...] = _rope(t, cos, lo, hi)
                else:
                    buf[...] = t
                qkv1_ref[s, 0, rows, sl] = buf[...].astype(BF16)
                for r in range(d4):
                    rows4 = slice(r * (half // d4), (r + 1) * (half // d4))
                    buf4[rows4, :] = buf[pl.ds(r, half // d4, stride=d4), :]
                    qkv4_ref[s, r, h * (half // d4):(h + 1) * (half // d4), sl] = (
                        buf4[rows4, :].astype(BF16))
                for r in range(d16):
                    r4, a = r % d4, r // d4
                    qkv16_ref[s, r, h * (half // d16):(h + 1) * (half // d16), sl] = buf4[
                        pl.ds(r4 * (half // d4) + a, half // d16, stride=d16 // d4), :].astype(BF16)

    for h in range(n_parts):
        project(h)
        finish(h)


def _front_common(x2, seq):
    n_tok = x2.shape[0]
    nt = n_tok // FRONT_TILE
    per_seq = seq // FRONT_TILE
    x_spec = pl.BlockSpec((FRONT_TILE, D_MODEL), lambda i: (i, 0))
    tab_spec = pl.BlockSpec((FRONT_TILE, LANES), lambda i: (i % per_seq, 0))
    out = lambda w, dt: (jax.ShapeDtypeStruct((n_tok, w), dt),
                         pl.BlockSpec((FRONT_TILE, w), lambda i: (i, 0)))
    return nt, x_spec, tab_spec, out


def _ev_front(x2, g, w, tabs, seq):
    nt, x_spec, tab_spec, out = _front_common(x2, seq)
    outs = [out(ATT_W, BF16), out(LANES, BF16), out(LANES, BF16), out(CONV_CH, F32)]
    return pl.pallas_call(
        _ev_front_kernel,
        grid=(nt,),
        in_specs=[x_spec, _const_spec(g.shape), _const_spec(w.shape),
                  tab_spec, tab_spec, tab_spec],
        out_specs=[o[1] for o in outs],
        out_shape=[o[0] for o in outs],
        scratch_shapes=[pltpu.VMEM((FRONT_PARTS, FRONT_TILE // FRONT_PARTS, w.shape[1]), F32)],
        compiler_params=_cparams(1),
        name="ev_front",
    )(x2, g, w, *tabs)


def _od_front(x2, g, w, tabs, lng, lnb, batch, seq):
    nt, x_spec, tab_spec, out = _front_common(x2, seq)
    per_seq = seq // FRONT_TILE

    def folded(dil):
        return (jax.ShapeDtypeStruct((batch, 3, dil, seq // dil, ATT_W), BF16),
                pl.BlockSpec((None, 3, dil, FRONT_TILE // dil, ATT_W),
                             lambda i: (i // per_seq, 0, 0, i % per_seq, 0)))

    outs = [folded(d) for d in DILATIONS] + [out(D_CH, F32), out(D_CH, BF16)]
    return pl.pallas_call(
        _od_front_kernel,
        grid=(nt,),
        in_specs=[x_spec, _const_spec(g.shape), _const_spec(w.shape),
                  tab_spec, tab_spec, tab_spec,
                  _const_spec(lng.shape), _const_spec(lnb.shape)],
        out_specs=[o[1] for o in outs],
        out_shape=[o[0] for o in outs],
        scratch_shapes=[pltpu.VMEM((3, ATT_W // LANES, FRONT_TILE // FRONT_PARTS, LANES), F32)] * 2
                       + [pltpu.VMEM((FRONT_PARTS, FRONT_TILE // FRONT_PARTS, w.shape[1]), F32)],
        compiler_params=_cparams(1),
        name="od_front",
    )(x2, g, w, *tabs, lng, lnb)


def _band_masks(min_dist):
    qi = lax.broadcasted_iota(jnp.int32, (BLOCK, 2 * BLOCK), 0)
    kj = lax.broadcasted_iota(jnp.int32, (BLOCK, 2 * BLOCK), 1)
    dist = qi + BLOCK - kj
    band = (dist >= 0) & (dist <= BLOCK - 1 + min_dist)
    return band, kj < BLOCK


def _softmax_rows(s, band, prev_keys, prev_penalty):
    s = jnp.where(band, s, NEG)
    if prev_penalty is not None:
        s = jnp.where(prev_keys, s + prev_penalty, s)
    m = jnp.max(s, axis=-1, keepdims=True)
    p = jnp.exp(s - m)
    l = jnp.sum(p, axis=-1, keepdims=True)
    return p, m, l


def _conv_skew(tile):
    skew = tile // SUBLANES + 1
    assert skew % 2 == 1
    return skew


def _ev_mix_kernel(sink_ref, x_ref, q_ref, k_ref, kh_ref, v_ref, vh_ref, c_ref, ch_ref,
                   convw_ref, convb_ref, lng_ref, lnb_ref, wout_ref, o_ref,
                   kbuf, vbuf, cbuf, cout, mixbuf):
    i = pl.program_id(1)
    tile = q_ref.shape[0]
    n_pairs = ATT_W // LANES

    kbuf[0:BLOCK, :] = kh_ref[...]
    kbuf[BLOCK:, :] = k_ref[...]
    vbuf[0:BLOCK, :] = vh_ref[...]
    vbuf[BLOCK:, :] = v_ref[...]

    row = lax.broadcasted_iota(jnp.int32, (BLOCK, BLOCK), 0)
    col = lax.broadcasted_iota(jnp.int32, (BLOCK, BLOCK), 1)
    own = col <= row
    first_penalty = jnp.where(i == 0, NEG, 0.0).astype(F32)
    low_lanes = lax.broadcasted_iota(jnp.int32, (BLOCK, LANES), 1) < HEAD_DIM

    for jb in range(tile // BLOCK):
        rows = slice(jb * BLOCK, (jb + 1) * BLOCK)
        kk = kbuf[jb * BLOCK:(jb + 2) * BLOCK, :]
        vv = vbuf[jb * BLOCK:(jb + 2) * BLOCK, :]
        parts = []
        for half in range(2):
            keep = low_lanes if half == 0 else jnp.logical_not(low_lanes)
            for p in range(n_pairs):
                qp = q_ref[rows, p * LANES:(p + 1) * LANES]
                parts.append(jnp.where(keep, qp, jnp.zeros_like(qp)))
        s_all = _dot_nt(jnp.concatenate(parts, axis=0), kk)
        probs, inv_l = [], []
        for idx in range(2 * n_pairs):
            s_prev = s_all[idx * BLOCK:(idx + 1) * BLOCK, 0:BLOCK]
            s_own = s_all[idx * BLOCK:(idx + 1) * BLOCK, BLOCK:]
            if jb == 0:
                s_prev = s_prev + first_penalty
            sc = jnp.where(own, s_own, s_prev)
            sink = sink_ref[idx]
            m = jnp.maximum(jnp.max(sc, axis=-1, keepdims=True), sink)
            pr = jnp.exp(sc - m)
            l = jnp.sum(pr, axis=-1, keepdims=True) + jnp.exp(sink - m)
            zero = jnp.zeros_like(pr)
            probs.append(jnp.concatenate([jnp.where(own, zero, pr), jnp.where(own, pr, zero)],
                                         axis=1).astype(BF16))
            inv_l.append(1.0 / l)
        o_all = _dot(jnp.concatenate(probs, axis=0), vv)
        for p in range(n_pairs):
            o_lo = o_all[p * BLOCK:(p + 1) * BLOCK] * inv_l[p]
            o_hi = o_all[(n_pairs + p) * BLOCK:(n_pairs + p + 1) * BLOCK] * inv_l[n_pairs + p]
            mixbuf[rows, p * LANES:(p + 1) * LANES] = jnp.where(low_lanes, o_lo, o_hi).astype(BF16)

    n_lg = CONV_CH // LANES
    skew = _conv_skew(tile)
    lead = CONV_HALO - (CONV_WIDTH - 1)
    for lg in range(n_lg):
        lanes = slice(lg * LANES, (lg + 1) * LANES)
        cbuf[lg, 0:CONV_HALO, :] = jnp.where(i > 0, ch_ref[:, lanes], 0.0)
        cbuf[lg, CONV_HALO:CONV_HALO + tile, :] = c_ref[:, lanes]
        cbuf[lg, CONV_HALO + tile:, :] = jnp.zeros((cbuf.shape[1] - CONV_HALO - tile, LANES), F32)
    skewed = lambda ref, lg, start: ref.at[lg, pl.ds(start, SUBLANES, stride=skew), :]
    one_trip = jnp.minimum(i, 0) + 1

    def conv_lane_group(lg):
        lanes = slice(lg * LANES, (lg + 1) * LANES)
        for t0 in range(0, CONV_WIDTH, CONV_TAPS):
            n_taps = min(CONV_TAPS, CONV_WIDTH - t0)
            w = [jnp.broadcast_to(convw_ref[t0 + tt:t0 + tt + 1, lanes], (SUBLANES, LANES))
                 for tt in range(n_taps)]
            acc = {}
            for o in range(skew + n_taps - 1):
                xs = skewed(cbuf, lg, o + lead + t0)[...]
                for tt in range(n_taps):
                    j = o - tt
                    if not 0 <= j < skew:
                        continue
                    if tt == 0:
                        acc[j] = (jnp.broadcast_to(convb_ref[:, lanes], (SUBLANES, LANES))
                                  if t0 == 0 else skewed(cout, lg, j)[...])
                    acc[j] = acc[j] + xs * w[tt]
                    if tt == n_taps - 1:
                        skewed(cout, lg, j)[...] = acc.pop(j)

    for lg in range(n_lg):
        lax.fori_loop(0, one_trip, lambda _, carry, lg=lg: (conv_lane_group(lg), carry)[1], 0)
    for j in range(skew):
        y = _layer_norm(jnp.concatenate([skewed(cout, lg, j)[...] for lg in range(n_lg)], axis=1),
                        lng_ref[...], lnb_ref[...])
        y = y * jax.nn.sigmoid(y)
        for lg in range(n_lg):
            skewed(cout, lg, j)[...] = y[:, lg * LANES:(lg + 1) * LANES]
    for lg in range(n_lg):
        mixbuf[:, ATT_W + lg * LANES:ATT_W + (lg + 1) * LANES] = cout[lg, 0:tile, :].astype(BF16)

    o_ref[...] = x_ref[...] + _dot(mixbuf[...], wout_ref[...])


def _ev_tail_kernel(sink_ref, x_ref, q_ref, k_ref, kh_ref, v_ref, vh_ref, c_ref, ch_ref,
                    convw_ref, convb_ref, lng_ref, lnb_ref, wout_ref,
                    fng_ref, wg_ref, wu_ref, wd_ref, fg_ref, o_ref,
                    kbuf, vbuf, cbuf, cout, mixbuf, actbuf, h_mid):
    _ev_mix_kernel(sink_ref, x_ref, q_ref, k_ref, kh_ref, v_ref, vh_ref, c_ref, ch_ref,
                   convw_ref, convb_ref, lng_ref, lnb_ref, wout_ref, h_mid,
                   kbuf, vbuf, cbuf, cout, mixbuf)
    _ffn_kernel(h_mid, fng_ref, wg_ref, wu_ref, wd_ref, fg_ref, o_ref, actbuf, final_norm=False)


def _ev_tail(x2, q, k, v, c, sinks, convw, convb, lng, lnb, wout, ffn, layer, batch, seq):
    tile = TOK_TILE
    nt = seq // tile
    tok = lambda w: pl.BlockSpec((tile, w), lambda b, i: (b * nt + i, 0))
    halo = lambda rows, w: pl.BlockSpec(
        (rows, w), lambda b, i: (jnp.maximum((b * nt + i) * (tile // rows) - 1, 0), 0))
    consts = [convw, convb, lng, lnb, wout]
    skew_rows = SUBLANES * _conv_skew(tile)
    return pl.pallas_call(
        _ev_tail_kernel,
        grid=(batch, nt),
        in_specs=[pl.BlockSpec(memory_space=pltpu.SMEM),
                  tok(D_MODEL), tok(ATT_W), tok(LANES), halo(BLOCK, LANES),
                  tok(LANES), halo(BLOCK, LANES), tok(CONV_CH), halo(CONV_HALO, CONV_CH)]
                 + [_const_spec(t.shape) for t in consts] + _ffn_specs(ffn, layer),
        out_specs=tok(D_MODEL),
        out_shape=jax.ShapeDtypeStruct(x2.shape, F32),
        scratch_shapes=[pltpu.VMEM((tile + BLOCK, LANES), BF16),
                        pltpu.VMEM((tile + BLOCK, LANES), BF16),
                        pltpu.VMEM((CONV_CH // LANES, CONV_HALO + skew_rows, LANES), F32),
                        pltpu.VMEM((CONV_CH // LANES, skew_rows, LANES), F32),
                        pltpu.VMEM((tile, ATT_W + CONV_CH), BF16),
                        pltpu.VMEM((tile, D_FF), BF16),
                        pltpu.VMEM((tile, D_MODEL), F32)],
        compiler_params=_cparams(2),
        name="ev_tail",
    )(sinks, x2, q, k, k, v, v, c, c, *consts, *ffn)


def _branch_kernel(q_ref, k_ref, kh_ref, v_ref, vh_ref, o_ref, lse_ref, kbuf, vbuf):
    i = pl.program_id(2)
    n_res, tile = q_ref.shape[0], q_ref.shape[1]

    band, prev_keys = _band_masks(1)
    first_penalty = jnp.where(i == 0, NEG, 0.0).astype(F32)
    low_lanes = lax.broadcasted_iota(jnp.int32, (BLOCK, LANES), 1) < HEAD_DIM

    for r in range(n_res):
        kbuf[r, 0:BLOCK, :] = kh_ref[r]
        kbuf[r, BLOCK:, :] = k_ref[r]
        vbuf[r, 0:BLOCK, :] = vh_ref[r]
        vbuf[r, BLOCK:, :] = v_ref[r]
        for jb in range(tile // BLOCK):
            rows = slice(jb * BLOCK, (jb + 1) * BLOCK)
            for p in range(ATT_W // LANES):
                lanes = slice(p * LANES, (p + 1) * LANES)
                kk = kbuf[r, jb * BLOCK:(jb + 2) * BLOCK, lanes]
                vv = vbuf[r, jb * BLOCK:(jb + 2) * BLOCK, lanes]
                qp = q_ref[r, rows, lanes]
                zero = jnp.zeros_like(qp)
                q2 = jnp.concatenate([jnp.where(low_lanes, qp, zero),
                                      jnp.where(low_lanes, zero, qp)], axis=0)
                s2 = _dot_nt(q2, kk)
                probs, inv_l, lse = [], [], []
                for half in range(2):
                    pr, m, l = _softmax_rows(s2[half * BLOCK:(half + 1) * BLOCK], band,
                                             prev_keys, first_penalty if jb == 0 else None)
                    probs.append(pr.astype(BF16))
                    inv_l.append(1.0 / l)
                    lse.append(m + jnp.log(l))
                o2 = _dot(jnp.concatenate(probs, axis=0), vv)
                o_ref[r, rows, lanes] = jnp.where(low_lanes, o2[0:BLOCK] * inv_l[0],
                                                  o2[BLOCK:] * inv_l[1])
                lse_ref[r, rows, lanes] = jnp.where(low_lanes, lse[0], lse[1])


def _branch(qkv, batch, seq, dil):
    sub = seq // dil
    tile = min(BRANCH_ROWS, sub)
    n_res = BRANCH_ROWS // tile
    assert dil % n_res == 0
    nt = sub // tile

    def cur(s):
        return pl.BlockSpec((None, None, n_res, tile, ATT_W), lambda b, r, i: (b, s, r, i, 0))

    def halo(s):
        return pl.BlockSpec(
            (None, None, n_res, BLOCK, ATT_W),
            lambda b, r, i: (b, s, r, jnp.maximum(i * (tile // BLOCK) - 1, 0), 0))

    out_spec = pl.BlockSpec((None, n_res, tile, ATT_W), lambda b, r, i: (b, r, i, 0))
    out_sds = jax.ShapeDtypeStruct((batch, dil, sub, ATT_W), F32)
    return pl.pallas_call(
        _branch_kernel,
        grid=(batch, dil // n_res, nt),
        in_specs=[cur(0), cur(1), halo(1), cur(2), halo(2)],
        out_specs=[out_spec, out_spec],
        out_shape=[out_sds, out_sds],
        scratch_shapes=[pltpu.VMEM((n_res, tile + BLOCK, ATT_W), BF16),
                        pltpu.VMEM((n_res, tile + BLOCK, ATT_W), BF16)],
        compiler_params=_cparams(3),
        name=f"od_branch_d{dil}",
    )(qkv, qkv, qkv, qkv, qkv)


def _od_mix_kernel(x_ref, o1_ref, o2_ref, o3_ref, l1_ref, l2_ref, l3_ref, u_ref, gate_ref,
                   sw_ref, sb_ref, wout_ref, out_ref, mixbuf, tok_o2, tok_l2, tok_o3, tok_l3):
    tile = x_ref.shape[0]
    for j in range(ATT_W // LANES):
        sl = slice(j * LANES, (j + 1) * LANES)
        for src, dst in ((o2_ref, tok_o2), (l2_ref, tok_l2), (o3_ref, tok_o3), (l3_ref, tok_l3)):
            dil = src.shape[0]
            for r in range(dil):
                dst[j, pl.ds(r, tile // dil, stride=dil), :] = src[r, :, sl]
        l1, l2, l3 = l1_ref[0, :, sl], tok_l2[j], tok_l3[j]
        m = jnp.maximum(jnp.maximum(l1, l2), l3)
        e1, e2, e3 = jnp.exp(l1 - m), jnp.exp(l2 - m), jnp.exp(l3 - m)
        mixed = (e1 * o1_ref[0, :, sl] + e2 * tok_o2[j] + e3 * tok_o3[j]) / (e1 + e2 + e3)
        mixbuf[:, sl] = mixed.astype(BF16)

    ti = lax.broadcasted_iota(jnp.int32, (CHUNK, CHUNK), 0)
    si = lax.broadcasted_iota(jnp.int32, (CHUNK, CHUNK), 1)
    causal = ti >= si
    low_lanes = lax.broadcasted_iota(jnp.int32, (CHUNK, LANES), 1) < HEAD_DIM
    for p in range(D_CH // LANES):
        lanes = slice(p * LANES, (p + 1) * LANES)
        w2 = jnp.concatenate(
            [jnp.where(causal, sw_ref[2 * p], 0.0), jnp.where(causal, sw_ref[2 * p + 1], 0.0)],
            axis=0).astype(BF16)
        bias = sb_ref[:, lanes]
        for ch in range(tile // CHUNK):
            rows = slice(ch * CHUNK, (ch + 1) * CHUNK)
            y2 = _dot(w2, gate_ref[rows, lanes])
            y = jnp.where(low_lanes, y2[0:CHUNK], y2[CHUNK:]) + bias
            mixbuf[rows, ATT_W + p * LANES:ATT_W + (p + 1) * LANES] = (
                u_ref[rows, lanes] * y).astype(BF16)

    out_ref[...] = x_ref[...] + _dot(mixbuf[...], wout_ref[...])


def _od_tail_kernel(x_ref, o1_ref, o2_ref, o3_ref, l1_ref, l2_ref, l3_ref, u_ref, gate_ref,
                    sw_ref, sb_ref, wout_ref, fng_ref, wg_ref, wu_ref, wd_ref, fg_ref, out_ref,
                    mixbuf, tok_o2, tok_l2, tok_o3, tok_l3, actbuf, h_mid):
    _od_mix_kernel(x_ref, o1_ref, o2_ref, o3_ref, l1_ref, l2_ref, l3_ref, u_ref, gate_ref,
                   sw_ref, sb_ref, wout_ref, h_mid, mixbuf, tok_o2, tok_l2, tok_o3, tok_l3)
    _ffn_kernel(h_mid, fng_ref, wg_ref, wu_ref, wd_ref, fg_ref, out_ref, actbuf, final_norm=True)


def _od_tail(x2, outs, lses, u, gate, sw, sb, wout, ffn, layer, seq):
    tile = TOK_TILE
    per_seq = seq // tile
    tok = lambda w: pl.BlockSpec((tile, w), lambda i: (i, 0))
    folded = [pl.BlockSpec((None, d, tile // d, ATT_W),
                           lambda i: (i // per_seq, 0, i % per_seq, 0)) for d in DILATIONS]
    consts = [sw, sb, wout]
    return pl.pallas_call(
        _od_tail_kernel,
        grid=(x2.shape[0] // tile,),
        in_specs=[tok(D_MODEL)] + folded + folded + [tok(D_CH), tok(D_CH)]
                 + [_const_spec(t.shape) for t in consts] + _ffn_specs(ffn, layer),
        out_specs=tok(D_MODEL),
        out_shape=jax.ShapeDtypeStruct(x2.shape, F32),
        scratch_shapes=[pltpu.VMEM((tile, ATT_W + D_CH), BF16)]
                       + [pltpu.VMEM((ATT_W // LANES, tile, LANES), F32)] * 4
                       + [pltpu.VMEM((tile, D_FF), BF16), pltpu.VMEM((tile, D_MODEL), F32)],
        compiler_params=_cparams(1),
        name="od_tail",
    )(x2, *outs, *lses, u, gate, *consts, *ffn)


def _ffn_kernel(x_ref, g_ref, wg_ref, wu_ref, wd_ref, fg_ref, o_ref, actbuf, *, final_norm):
    hn = _rms(x_ref[...], g_ref[...]).astype(BF16)
    for c in range(D_FF // FF_CHUNK):
        cols = slice(c * FF_CHUNK, (c + 1) * FF_CHUNK)
        gate = _dot(hn, wg_ref[:, cols])
        up = _dot(hn, wu_ref[:, cols])
        actbuf[:, cols] = (gate * jax.nn.sigmoid(gate) * up).astype(BF16)
    y = x_ref[...] + _dot(actbuf[...], wd_ref[...])
    if final_norm:
        y = _rms(y, fg_ref[...])
    o_ref[...] = y


def _cast_kernel(x_ref, o_ref):
    o_ref[...] = x_ref[...].astype(o_ref.dtype)


def _to_bf16(w):
    n_l, rows, cols = w.shape
    blk_rows = rows // CAST_SPLIT
    assert blk_rows * CAST_SPLIT == rows and blk_rows % (2 * SUBLANES) == 0
    blk = pl.BlockSpec((None, blk_rows, cols), lambda l, r: (l, r, 0))
    return pl.pallas_call(
        _cast_kernel,
        grid=(n_l, CAST_SPLIT),
        in_specs=[blk],
        out_specs=blk,
        out_shape=jax.ShapeDtypeStruct(w.shape, BF16),
        compiler_params=_cparams(2),
        name="to_bf16",
    )(w)


def kernel(x, ev_norm_g, ev_w_in, ev_sinks, ev_conv_w, ev_conv_b, ev_conv_ln_g, ev_conv_ln_b, ev_w_out, od_norm_g, od_w_in, od_sgu_ln_g, od_sgu_ln_b, od_spatial_w, od_spatial_b, od_w_out, ffn_norm_g, ffn_w_gate, ffn_w_up, ffn_w_down, final_norm_g):
    batch, seq, _ = x.shape
    assert seq % (max(DILATIONS) * BLOCK) == 0 and seq % FRONT_TILE == 0 and seq % TOK_TILE == 0
    row = lambda t: t.reshape(1, -1).astype(F32)
    x2 = x.reshape(batch * seq, D_MODEL)
    tabs = _rope_tables(seq)

    n_pairs = ATT_W // LANES
    head_order = [h for p in range(n_pairs) for h in (p, n_pairs + p)]
    q_cols = [h * HEAD_DIM + d for h in head_order for d in range(HEAD_DIM)]
    in_cols = np.asarray(q_cols + list(range(ATT_W, ev_w_in.shape[2])), np.int32)
    out_rows = np.asarray(q_cols + list(range(ATT_W, ev_w_out.shape[1])), np.int32)
    w_in0 = jnp.take(ev_w_in[0], in_cols, axis=1).astype(BF16)
    w_out0 = jnp.take(ev_w_out[0], out_rows, axis=0).astype(BF16)
    q, k, v, c = _ev_front(x2, row(ev_norm_g[0]), w_in0, tabs, seq)
    w_gate, w_up, w_down = _to_bf16(ffn_w_gate), _to_bf16(ffn_w_up), _to_bf16(ffn_w_down)
    ffn_w = lambda l: (row(ffn_norm_g[l]), w_gate, w_up, w_down, row(final_norm_g))
    h = _ev_tail(x2, q, k, v, c, ev_sinks[0].astype(F32), ev_conv_w[0].astype(F32),
                 row(ev_conv_b[0]), row(ev_conv_ln_g[0]), row(ev_conv_ln_b[0]), w_out0,
                 ffn_w(0), 0, batch, seq)

    *qkvs, u, gate = _od_front(h, row(od_norm_g[0]), od_w_in[0].astype(BF16), tabs,
                               row(od_sgu_ln_g[0]), row(od_sgu_ln_b[0]), batch, seq)
    outs, lses = [], []
    for dil, qkv in zip(DILATIONS, qkvs):
        o_r, lse_r = _branch(qkv, batch, seq, dil)
        outs.append(o_r)
        lses.append(lse_r)
    sb = jnp.repeat(od_spatial_b[0].astype(F32).T, HEAD_DIM, axis=1)
    h = _od_tail(h, outs, lses, u, gate, od_spatial_w[0].astype(F32), sb,
                 od_w_out[0].astype(BF16), ffn_w(1), 1, seq)
    return h.reshape(batch, seq, D_MODEL)
```

```python
import jax
import jax.numpy as jnp
import numpy as np
from jax import lax
from jax.experimental import pallas as pl
from jax.experimental.pallas import tpu as pltpu

F32 = jnp.float32
BF16 = jnp.bfloat16

D_MODEL = 1024
HEAD_DIM = 64
ROT_DIM = 16
ROPE_THETA = 500000.0
BLOCK = 128
RMS_EPS = 1e-6
LN_EPS = 1e-5
N_HEADS = 8
ATT_W = N_HEADS * HEAD_DIM
CONV_CH = 512
CONV_WIDTH = 31
CONV_HALO = 32
D_CH = 512
D_GROUPS = 8
CHUNK = 128
D_FF = 2816
DILATIONS = (1, 4, 16)
LANES = 128
SUBLANES = 8
NEG = -1e30

FRONT_TILE = 1024
FRONT_PARTS = 4
TOK_TILE = 512
BRANCH_ROWS = 2048
FF_CHUNK = 256
CONV_TAPS = 16
VMEM_LIMIT = 56 * 1024 * 1024


def _cparams(n_axes):
    return pltpu.CompilerParams(
        dimension_semantics=("arbitrary",) * n_axes,
        vmem_limit_bytes=VMEM_LIMIT)


def _const_spec(shape, layer=None):
    nd = len(shape)
    if layer is None:
        return pl.BlockSpec(shape, lambda *_: (0,) * nd, pipeline_mode=pl.Buffered(1))
    return pl.BlockSpec((None,) + tuple(shape[1:]), lambda *_: (layer,) + (0,) * (nd - 1),
                        pipeline_mode=pl.Buffered(1))


def _ffn_specs(ffn, layer):
    fng, wg, wu, wd, fg = ffn
    return [_const_spec(fng.shape)] + [_const_spec(t.shape, layer) for t in (wg, wu, wd)] + [
        _const_spec(fg.shape)]


def _rms(x, g):
    ms = jnp.mean(x * x, axis=-1, keepdims=True)
    return x * lax.rsqrt(ms + RMS_EPS) * g


def _layer_norm(x, g, b):
    mu = jnp.mean(x, axis=-1, keepdims=True)
    xc = x - mu
    var = jnp.mean(xc * xc, axis=-1, keepdims=True)
    return xc * lax.rsqrt(var + LN_EPS) * g + b


def _dot(a, b):
    return jnp.dot(a, b, preferred_element_type=F32)


def _dot_nt(a, b):
    return lax.dot_general(a, b, (((1,), (1,)), ((), ())), preferred_element_type=F32)


def _rope(t, cos, sin_lo, sin_hi):
    up = pltpu.roll(t, LANES - ROT_DIM // 2, 1)
    down = pltpu.roll(t, ROT_DIM // 2, 1)
    return t * cos + up * sin_lo + down * sin_hi


def _rope_tables(seq):
    half = ROT_DIM // 2
    inv_freq = ROPE_THETA ** (-np.arange(half, dtype=np.float64) * (2.0 / ROT_DIM))
    ang = np.arange(seq, dtype=np.float64)[:, None] * inv_freq[None, :]
    cos, sin = np.cos(ang), np.sin(ang)
    ones = np.ones((seq, HEAD_DIM - ROT_DIM))
    zeros = np.zeros((seq, HEAD_DIM - ROT_DIM))
    zh = np.zeros((seq, half))
    cos_h = np.concatenate([cos, cos, ones], axis=1)
    lo_h = np.concatenate([-sin, zh, zeros], axis=1)
    hi_h = np.concatenate([zh, sin, zeros], axis=1)
    two = lambda t: jnp.asarray(np.concatenate([t, t], axis=1), F32)
    return two(cos_h), two(lo_h), two(hi_h)


def _ev_front_kernel(x_ref, g_ref, w_ref, cos_ref, lo_ref, hi_ref, wg_ref, wu_ref, wd_ref,
                     q_ref, k_ref, v_ref, c_ref, wg_out, wu_out, wd_out, raw):
    for src, dst in ((wg_ref, wg_out), (wu_ref, wu_out), (wd_ref, wd_out)):
        dst[...] = src[...].astype(BF16)
    n_parts = raw.shape[0]
    part = x_ref.shape[0] // n_parts
    base = ATT_W + 2 * LANES

    def project(h):
        rows = slice(h * part, (h + 1) * part)
        hn = _rms(x_ref[rows, :], g_ref[...]).astype(BF16)
        raw[h, :, 0:base] = _dot(hn, w_ref[:, 0:base])
        raw[h, :, base:base + CONV_CH] = _dot(hn, w_ref[:, base:base + CONV_CH])
        raw[h, :, base + CONV_CH:] = _dot(hn, w_ref[:, base + CONV_CH:base + 2 * CONV_CH])

    def finish(h):
        rows = slice(h * part, (h + 1) * part)
        cos, lo, hi = cos_ref[rows, :], lo_ref[rows, :], hi_ref[rows, :]
        for j in range(ATT_W // LANES):
            t = raw[h, :, j * LANES:(j + 1) * LANES]
            q_ref[rows, j * LANES:(j + 1) * LANES] = (
                _rope(t, cos, lo, hi) * (HEAD_DIM ** -0.5)).astype(BF16)
        k_ref[rows, :] = _rope(raw[h, :, ATT_W:ATT_W + LANES], cos, lo, hi).astype(BF16)
        v_ref[rows, :] = raw[h, :, ATT_W + LANES:base].astype(BF16)
        c_ref[rows, :] = raw[h, :, base:base + CONV_CH] * jax.nn.sigmoid(raw[h, :, base + CONV_CH:])

    for h in range(n_parts):
        project(h)
        finish(h)


def _od_front_kernel(x_ref, g_ref, w_ref, cos_ref, lo_ref, hi_ref, lng_ref, lnb_ref,
                     qkv1_ref, qkv4_ref, qkv16_ref, u_ref, gate_ref, stage, stage4, raw):
    tile = x_ref.shape[0]
    n_parts = raw.shape[0]
    half = tile // n_parts
    d4, d16 = DILATIONS[1], DILATIONS[2]
    n_out = w_ref.shape[1]

    def project(h):
        rows = slice(h * half, (h + 1) * half)
        hn = _rms(x_ref[rows, :], g_ref[...]).astype(BF16)
        for c in range(n_out // ATT_W):
            cols = slice(c * ATT_W, (c + 1) * ATT_W)
            raw[h, :, cols] = _dot(hn, w_ref[:, cols])

    def finish(h):
        rows = slice(h * half, (h + 1) * half)
        cos, lo, hi = cos_ref[rows, :], lo_ref[rows, :], hi_ref[rows, :]
        zg = jax.nn.gelu(raw[h, :, 3 * ATT_W + D_CH:3 * ATT_W + 2 * D_CH])
        gate_ref[rows, :] = _layer_norm(zg, lng_ref[...], lnb_ref[...]).astype(BF16)
        u_ref[rows, :] = jax.nn.gelu(raw[h, :, 3 * ATT_W:3 * ATT_W + D_CH])
        for s in (2, 1, 0):
            for j in range(ATT_W // LANES):
                sl = slice(j * LANES, (j + 1) * LANES)
                t = raw[h, :, s * ATT_W + j * LANES:s * ATT_W + (j + 1) * LANES]
                buf, buf4 = stage.at[s, j], stage4.at[s, j]
                if s == 0:
                    buf[...] = _rope(t, cos, lo, hi) * (HEAD_DIM ** -0.5)
                elif s == 1:
                    buf[...] = _rope(t, cos, lo, hi)
                else:
                    buf[...] = t
                qkv1_ref[s, 0, rows, sl] = buf[...].astype(BF16)
                for r in range(d4):
                    rows4 = slice(r * (half // d4), (r + 1) * (half // d4))
                    buf4[rows4, :] = buf[pl.ds(r, half // d4, stride=d4), :]
                    qkv4_ref[s, r, h * (half // d4):(h + 1) * (half // d4), sl] = (
                        buf4[rows4, :].astype(BF16))
                for r in range(d16):
                    r4, a = r % d4, r // d4
                    qkv16_ref[s, r, h * (half // d16):(h + 1) * (half // d16), sl] = buf4[
                        pl.ds(r4 * (half // d4) + a, half // d16, stride=d16 // d4), :].astype(BF16)

    for h in range(n_parts):
        project(h)
        finish(h)


def _front_common(x2, seq):
    n_tok = x2.shape[0]
    nt = n_tok // FRONT_TILE
    per_seq = seq // FRONT_TILE
    x_spec = pl.BlockSpec((FRONT_TILE, D_MODEL), lambda i: (i, 0))
    tab_spec = pl.BlockSpec((FRONT_TILE, LANES), lambda i: (i % per_seq, 0))
    out = lambda w, dt: (jax.ShapeDtypeStruct((n_tok, w), dt),
                         pl.BlockSpec((FRONT_TILE, w), lambda i: (i, 0)))
    return nt, x_spec, tab_spec, out


def _ev_front(x2, g, w, tabs, ffn_weights, seq):
    nt, x_spec, tab_spec, out = _front_common(x2, seq)
    outs = [out(ATT_W, BF16), out(LANES, BF16), out(LANES, BF16), out(CONV_CH, F32)]
    flat, slab_specs, slab_outs = [], [], []
    for t in ffn_weights:
        rows = t.shape[0] * t.shape[1]
        slab = rows // nt
        assert slab * nt == rows and slab % (2 * SUBLANES) == 0
        flat.append(t.reshape(rows, t.shape[2]))
        slab_specs.append(pl.BlockSpec((slab, t.shape[2]), lambda i: (i, 0)))
        slab_outs.append(jax.ShapeDtypeStruct((rows, t.shape[2]), BF16))
    res = pl.pallas_call(
        _ev_front_kernel,
        grid=(nt,),
        in_specs=[x_spec, _const_spec(g.shape), _const_spec(w.shape),
                  tab_spec, tab_spec, tab_spec] + slab_specs,
        out_specs=[o[1] for o in outs] + slab_specs,
        out_shape=[o[0] for o in outs] + slab_outs,
        scratch_shapes=[pltpu.VMEM((FRONT_PARTS, FRONT_TILE // FRONT_PARTS, w.shape[1]), F32)],
        compiler_params=_cparams(1),
        name="ev_front",
    )(x2, g, w, *tabs, *flat)
    return res[:4], [r.reshape(t.shape) for r, t in zip(res[4:], ffn_weights)]


def _od_front(x2, g, w, tabs, lng, lnb, batch, seq):
    nt, x_spec, tab_spec, out = _front_common(x2, seq)
    per_seq = seq // FRONT_TILE

    def folded(dil):
        return (jax.ShapeDtypeStruct((batch, 3, dil, seq // dil, ATT_W), BF16),
                pl.BlockSpec((None, 3, dil, FRONT_TILE // dil, ATT_W),
                             lambda i: (i // per_seq, 0, 0, i % per_seq, 0)))

    outs = [folded(d) for d in DILATIONS] + [out(D_CH, F32), out(D_CH, BF16)]
    return pl.pallas_call(
        _od_front_kernel,
        grid=(nt,),
        in_specs=[x_spec, _const_spec(g.shape), _const_spec(w.shape),
                  tab_spec, tab_spec, tab_spec,
                  _const_spec(lng.shape), _const_spec(lnb.shape)],
        out_specs=[o[1] for o in outs],
        out_shape=[o[0] for o in outs],
        scratch_shapes=[pltpu.VMEM((3, ATT_W // LANES, FRONT_TILE // FRONT_PARTS, LANES), F32)] * 2
                       + [pltpu.VMEM((FRONT_PARTS, FRONT_TILE // FRONT_PARTS, w.shape[1]), F32)],
        compiler_params=_cparams(1),
        name="od_front",
    )(x2, g, w, *tabs, lng, lnb)


def _band_masks(min_dist):
    qi = lax.broadcasted_iota(jnp.int32, (BLOCK, 2 * BLOCK), 0)
    kj = lax.broadcasted_iota(jnp.int32, (BLOCK, 2 * BLOCK), 1)
    dist = qi + BLOCK - kj
    band = (dist >= 0) & (dist <= BLOCK - 1 + min_dist)
    return band, kj < BLOCK


def _softmax_rows(s, band, prev_keys, prev_penalty):
    s = jnp.where(band, s, NEG)
    if prev_penalty is not None:
        s = jnp.where(prev_keys, s + prev_penalty, s)
    m = jnp.max(s, axis=-1, keepdims=True)
    p = jnp.exp(s - m)
    l = jnp.sum(p, axis=-1, keepdims=True)
    return p, m, l


def _conv_skew(tile):
    skew = tile // SUBLANES + 1
    assert skew % 2 == 1
    return skew


def _ev_mix_kernel(sink_ref, x_ref, q_ref, k_ref, kh_ref, v_ref, vh_ref, c_ref, ch_ref,
                   convw_ref, convb_ref, lng_ref, lnb_ref, wout_ref, o_ref,
                   kbuf, vbuf, cbuf, cout, mixbuf):
    i = pl.program_id(1)
    tile = q_ref.shape[0]
    n_pairs = ATT_W // LANES

    kbuf[0:BLOCK, :] = kh_ref[...]
    kbuf[BLOCK:, :] = k_ref[...]
    vbuf[0:BLOCK, :] = vh_ref[...]
    vbuf[BLOCK:, :] = v_ref[...]

    row = lax.broadcasted_iota(jnp.int32, (BLOCK, BLOCK), 0)
    col = lax.broadcasted_iota(jnp.int32, (BLOCK, BLOCK), 1)
    own = col <= row
    first_penalty = jnp.where(i == 0, NEG, 0.0).astype(F32)
    low_lanes = lax.broadcasted_iota(jnp.int32, (BLOCK, LANES), 1) < HEAD_DIM

    for jb in range(tile // BLOCK):
        rows = slice(jb * BLOCK, (jb + 1) * BLOCK)
        kk = kbuf[jb * BLOCK:(jb + 2) * BLOCK, :]
        vv = vbuf[jb * BLOCK:(jb + 2) * BLOCK, :]
        parts = []
        for half in range(2):
            keep = low_lanes if half == 0 else jnp.logical_not(low_lanes)
            for p in range(n_pairs):
                qp = q_ref[rows, p * LANES:(p + 1) * LANES]
                parts.append(jnp.where(keep, qp, jnp.zeros_like(qp)))
        s_all = _dot_nt(jnp.concatenate(parts, axis=0), kk)
        probs, inv_l = [], []
        for idx in range(2 * n_pairs):
            s_prev = s_all[idx * BLOCK:(idx + 1) * BLOCK, 0:BLOCK]
            s_own = s_all[idx * BLOCK:(idx + 1) * BLOCK, BLOCK:]
            if jb == 0:
                s_prev = s_prev + first_penalty
            sc = jnp.where(own, s_own, s_prev)
            sink = sink_ref[idx]
            m = jnp.maximum(jnp.max(sc, axis=-1, keepdims=True), sink)
            pr = jnp.exp(sc - m)
            l = jnp.sum(pr, axis=-1, keepdims=True) + jnp.exp(sink - m)
            zero = jnp.zeros_like(pr)
            probs.append(jnp.concatenate([jnp.where(own, zero, pr), jnp.where(own, pr, zero)],
                                         axis=1).astype(BF16))
            inv_l.append(1.0 / l)
        o_all = _dot(jnp.concatenate(probs, axis=0), vv)
        for p in range(n_pairs):
            o_lo = o_all[p * BLOCK:(p + 1) * BLOCK] * inv_l[p]
            o_hi = o_all[(n_pairs + p) * BLOCK:(n_pairs + p + 1) * BLOCK] * inv_l[n_pairs + p]
            mixbuf[rows, p * LANES:(p + 1) * LANES] = jnp.where(low_lanes, o_lo, o_hi).astype(BF16)

    n_lg = CONV_CH // LANES
    skew = _conv_skew(tile)
    lead = CONV_HALO - (CONV_WIDTH - 1)
    for lg in range(n_lg):
        lanes = slice(lg * LANES, (lg + 1) * LANES)
        cbuf[lg, 0:CONV_HALO, :] = jnp.where(i > 0, ch_ref[:, lanes], 0.0)
        cbuf[lg, CONV_HALO:CONV_HALO + tile, :] = c_ref[:, lanes]
        cbuf[lg, CONV_HALO + tile:, :] = jnp.zeros((cbuf.shape[1] - CONV_HALO - tile, LANES), F32)
    skewed = lambda ref, lg, start: ref.at[lg, pl.ds(start, SUBLANES, stride=skew), :]
    one_trip = jnp.minimum(i, 0) + 1

    def conv_lane_group(lg):
        lanes = slice(lg * LANES, (lg + 1) * LANES)
        for t0 in range(0, CONV_WIDTH, CONV_TAPS):
            n_taps = min(CONV_TAPS, CONV_WIDTH - t0)
            w = [jnp.broadcast_to(convw_ref[t0 + tt:t0 + tt + 1, lanes], (SUBLANES, LANES))
                 for tt in range(n_taps)]
            acc = {}
            for o in range(skew + n_taps - 1):
                xs = skewed(cbuf, lg, o + lead + t0)[...]
                for tt in range(n_taps):
                    j = o - tt
                    if not 0 <= j < skew:
                        continue
                    if tt == 0:
                        acc[j] = (jnp.broadcast_to(convb_ref[:, lanes], (SUBLANES, LANES))
                                  if t0 == 0 else skewed(cout, lg, j)[...])
                    acc[j] = acc[j] + xs * w[tt]
                    if tt == n_taps - 1:
                        skewed(cout, lg, j)[...] = acc.pop(j)

    for lg in range(n_lg):
        lax.fori_loop(0, one_trip, lambda _, carry, lg=lg: (conv_lane_group(lg), carry)[1], 0)
    for j in range(skew):
        y = _layer_norm(jnp.concatenate([skewed(cout, lg, j)[...] for lg in range(n_lg)], axis=1),
                        lng_ref[...], lnb_ref[...])
        y = y * jax.nn.sigmoid(y)
        for lg in range(n_lg):
            skewed(cout, lg, j)[...] = y[:, lg * LANES:(lg + 1) * LANES]
    for lg in range(n_lg):
        mixbuf[:, ATT_W + lg * LANES:ATT_W + (lg + 1) * LANES] = cout[lg, 0:tile, :].astype(BF16)

    o_ref[...] = x_ref[...] + _dot(mixbuf[...], wout_ref[...])


def _ev_tail_kernel(sink_ref, x_ref, q_ref, k_ref, kh_ref, v_ref, vh_ref, c_ref, ch_ref,
                    convw_ref, convb_ref, lng_ref, lnb_ref, wout_ref,
                    fng_ref, wg_ref, wu_ref, wd_ref, fg_ref, o_ref,
                    kbuf, vbuf, cbuf, cout, mixbuf, actbuf, h_mid):
    _ev_mix_kernel(sink_ref, x_ref, q_ref, k_ref, kh_ref, v_ref, vh_ref, c_ref, ch_ref,
                   convw_ref, convb_ref, lng_ref, lnb_ref, wout_ref, h_mid,
                   kbuf, vbuf, cbuf, cout, mixbuf)
    _ffn_kernel(h_mid, fng_ref, wg_ref, wu_ref, wd_ref, fg_ref, o_ref, actbuf, final_norm=False)


def _ev_tail(x2, q, k, v, c, sinks, convw, convb, lng, lnb, wout, ffn, layer, batch, seq):
    tile = TOK_TILE
    nt = seq // tile
    tok = lambda w: pl.BlockSpec((tile, w), lambda b, i: (b * nt + i, 0))
    halo = lambda rows, w: pl.BlockSpec(
        (rows, w), lambda b, i: (jnp.maximum((b * nt + i) * (tile // rows) - 1, 0), 0))
    consts = [convw, convb, lng, lnb, wout]
    skew_rows = SUBLANES * _conv_skew(tile)
    return pl.pallas_call(
        _ev_tail_kernel,
        grid=(batch, nt),
        in_specs=[pl.BlockSpec(memory_space=pltpu.SMEM),
                  tok(D_MODEL), tok(ATT_W), tok(LANES), halo(BLOCK, LANES),
                  tok(LANES), halo(BLOCK, LANES), tok(CONV_CH), halo(CONV_HALO, CONV_CH)]
                 + [_const_spec(t.shape) for t in consts] + _ffn_specs(ffn, layer),
        out_specs=tok(D_MODEL),
        out_shape=jax.ShapeDtypeStruct(x2.shape, F32),
        scratch_shapes=[pltpu.VMEM((tile + BLOCK, LANES), BF16),
                        pltpu.VMEM((tile + BLOCK, LANES), BF16),
                        pltpu.VMEM((CONV_CH // LANES, CONV_HALO + skew_rows, LANES), F32),
                        pltpu.VMEM((CONV_CH // LANES, skew_rows, LANES), F32),
                        pltpu.VMEM((tile, ATT_W + CONV_CH), BF16),
                        pltpu.VMEM((tile, D_FF), BF16),
                        pltpu.VMEM((tile, D_MODEL), F32)],
        compiler_params=_cparams(2),
        name="ev_tail",
    )(sinks, x2, q, k, k, v, v, c, c, *consts, *ffn)


def _branch_kernel(q_ref, k_ref, kh_ref, v_ref, vh_ref, o_ref, lse_ref, kbuf, vbuf):
    i = pl.program_id(2)
    n_res, tile = q_ref.shape[0], q_ref.shape[1]

    band, prev_keys = _band_masks(1)
    first_penalty = jnp.where(i == 0, NEG, 0.0).astype(F32)
    low_lanes = lax.broadcasted_iota(jnp.int32, (BLOCK, LANES), 1) < HEAD_DIM

    for r in range(n_res):
        kbuf[r, 0:BLOCK, :] = kh_ref[r]
        kbuf[r, BLOCK:, :] = k_ref[r]
        vbuf[r, 0:BLOCK, :] = vh_ref[r]
        vbuf[r, BLOCK:, :] = v_ref[r]
        for jb in range(tile // BLOCK):
            rows = slice(jb * BLOCK, (jb + 1) * BLOCK)
            for p in range(ATT_W // LANES):
                lanes = slice(p * LANES, (p + 1) * LANES)
                kk = kbuf[r, jb * BLOCK:(jb + 2) * BLOCK, lanes]
                vv = vbuf[r, jb * BLOCK:(jb + 2) * BLOCK, lanes]
                qp = q_ref[r, rows, lanes]
                zero = jnp.zeros_like(qp)
                q2 = jnp.concatenate([jnp.where(low_lanes, qp, zero),
                                      jnp.where(low_lanes, zero, qp)], axis=0)
                s2 = _dot_nt(q2, kk)
                probs, inv_l, lse = [], [], []
                for half in range(2):
                    pr, m, l = _softmax_rows(s2[half * BLOCK:(half + 1) * BLOCK], band,
                                             prev_keys, first_penalty if jb == 0 else None)
                    probs.append(pr.astype(BF16))
                    inv_l.append(1.0 / l)
                    lse.append(m + jnp.log(l))
                o2 = _dot(jnp.concatenate(probs, axis=0), vv)
                o_ref[r, rows, lanes] = jnp.where(low_lanes, o2[0:BLOCK] * inv_l[0],
                                                  o2[BLOCK:] * inv_l[1])
                lse_ref[r, rows, lanes] = jnp.where(low_lanes, lse[0], lse[1])


def _branch(qkv, batch, seq, dil):
    sub = seq // dil
    tile = min(BRANCH_ROWS, sub)
    n_res = BRANCH_ROWS // tile
    assert dil % n_res == 0
    nt = sub // tile

    def cur(s):
        return pl.BlockSpec((None, None, n_res, tile, ATT_W), lambda b, r, i: (b, s, r, i, 0))

    def halo(s):
        return pl.BlockSpec(
            (None, None, n_res, BLOCK, ATT_W),
            lambda b, r, i: (b, s, r, jnp.maximum(i * (tile // BLOCK) - 1, 0), 0))

    out_spec = pl.BlockSpec((None, n_res, tile, ATT_W), lambda b, r, i: (b, r, i, 0))
    out_sds = jax.ShapeDtypeStruct((batch, dil, sub, ATT_W), F32)
    return pl.pallas_call(
        _branch_kernel,
        grid=(batch, dil // n_res, nt),
        in_specs=[cur(0), cur(1), halo(1), cur(2), halo(2)],
        out_specs=[out_spec, out_spec],
        out_shape=[out_sds, out_sds],
        scratch_shapes=[pltpu.VMEM((n_res, tile + BLOCK, ATT_W), BF16),
                        pltpu.VMEM((n_res, tile + BLOCK, ATT_W), BF16)],
        compiler_params=_cparams(3),
        name=f"od_branch_d{dil}",
    )(qkv, qkv, qkv, qkv, qkv)


def _od_mix_kernel(x_ref, o1_ref, o2_ref, o3_ref, l1_ref, l2_ref, l3_ref, u_ref, gate_ref,
                   sw_ref, sb_ref, wout_ref, out_ref, mixbuf, tok_o2, tok_l2, tok_o3, tok_l3):
    tile = x_ref.shape[0]
    for j in range(ATT_W // LANES):
        sl = slice(j * LANES, (j + 1) * LANES)
        for src, dst in ((o2_ref, tok_o2), (l2_ref, tok_l2), (o3_ref, tok_o3), (l3_ref, tok_l3)):
            dil = src.shape[0]
            for r in range(dil):
                dst[j, pl.ds(r, tile // dil, stride=dil), :] = src[r, :, sl]
        l1, l2, l3 = l1_ref[0, :, sl], tok_l2[j], tok_l3[j]
        m = jnp.maximum(jnp.maximum(l1, l2), l3)
        e1, e2, e3 = jnp.exp(l1 - m), jnp.exp(l2 - m), jnp.exp(l3 - m)
        mixed = (e1 * o1_ref[0, :, sl] + e2 * tok_o2[j] + e3 * tok_o3[j]) / (e1 + e2 + e3)
        mixbuf[:, sl] = mixed.astype(BF16)

    ti = lax.broadcasted_iota(jnp.int32, (CHUNK, CHUNK), 0)
    si = lax.broadcasted_iota(jnp.int32, (CHUNK, CHUNK), 1)
    causal = ti >= si
    low_lanes = lax.broadcasted_iota(jnp.int32, (CHUNK, LANES), 1) < HEAD_DIM
    for p in range(D_CH // LANES):
        lanes = slice(p * LANES, (p + 1) * LANES)
        w2 = jnp.concatenate(
            [jnp.where(causal, sw_ref[2 * p], 0.0), jnp.where(causal, sw_ref[2 * p + 1], 0.0)],
            axis=0).astype(BF16)
        bias = sb_ref[:, lanes]
        for ch in range(tile // CHUNK):
            rows = slice(ch * CHUNK, (ch + 1) * CHUNK)
            y2 = _dot(w2, gate_ref[rows, lanes])
            y = jnp.where(low_lanes, y2[0:CHUNK], y2[CHUNK:]) + bias
            mixbuf[rows, ATT_W + p * LANES:ATT_W + (p + 1) * LANES] = (
                u_ref[rows, lanes] * y).astype(BF16)

    out_ref[...] = x_ref[...] + _dot(mixbuf[...], wout_ref[...])


def _od_tail_kernel(x_ref, o1_ref, o2_ref, o3_ref, l1_ref, l2_ref, l3_ref, u_ref, gate_ref,
                    sw_ref, sb_ref, wout_ref, fng_ref, wg_ref, wu_ref, wd_ref, fg_ref, out_ref,
                    mixbuf, tok_o2, tok_l2, tok_o3, tok_l3, actbuf, h_mid):
    _od_mix_kernel(x_ref, o1_ref, o2_ref, o3_ref, l1_ref, l2_ref, l3_ref, u_ref, gate_ref,
                   sw_ref, sb_ref, wout_ref, h_mid, mixbuf, tok_o2, tok_l2, tok_o3, tok_l3)
    _ffn_kernel(h_mid, fng_ref, wg_ref, wu_ref, wd_ref, fg_ref, out_ref, actbuf, final_norm=True)


def _od_tail(x2, outs, lses, u, gate, sw, sb, wout, ffn, layer, seq):
    tile = TOK_TILE
    per_seq = seq // tile
    tok = lambda w: pl.BlockSpec((tile, w), lambda i: (i, 0))
    folded = [pl.BlockSpec((None, d, tile // d, ATT_W),
                           lambda i: (i // per_seq, 0, i % per_seq, 0)) for d in DILATIONS]
    consts = [sw, sb, wout]
    return pl.pallas_call(
        _od_tail_kernel,
        grid=(x2.shape[0] // tile,),
        in_specs=[tok(D_MODEL)] + folded + folded + [tok(D_CH), tok(D_CH)]
                 + [_const_spec(t.shape) for t in consts] + _ffn_specs(ffn, layer),
        out_specs=tok(D_MODEL),
        out_shape=jax.ShapeDtypeStruct(x2.shape, F32),
        scratch_shapes=[pltpu.VMEM((tile, ATT_W + D_CH), BF16)]
                       + [pltpu.VMEM((ATT_W // LANES, tile, LANES), F32)] * 4
                       + [pltpu.VMEM((tile, D_FF), BF16), pltpu.VMEM((tile, D_MODEL), F32)],
        compiler_params=_cparams(1),
        name="od_tail",
    )(x2, *outs, *lses, u, gate, *consts, *ffn)


def _ffn_kernel(x_ref, g_ref, wg_ref, wu_ref, wd_ref, fg_ref, o_ref, actbuf, *, final_norm):
    hn = _rms(x_ref[...], g_ref[...]).astype(BF16)
    for c in range(D_FF // FF_CHUNK):
        cols = slice(c * FF_CHUNK, (c + 1) * FF_CHUNK)
        gate = _dot(hn, wg_ref[:, cols])
        up = _dot(hn, wu_ref[:, cols])
        actbuf[:, cols] = (gate * jax.nn.sigmoid(gate) * up).astype(BF16)
    y = x_ref[...] + _dot(actbuf[...], wd_ref[...])
    if final_norm:
        y = _rms(y, fg_ref[...])
    o_ref[...] = y


def kernel(x, ev_norm_g, ev_w_in, ev_sinks, ev_conv_w, ev_conv_b, ev_conv_ln_g, ev_conv_ln_b, ev_w_out, od_norm_g, od_w_in, od_sgu_ln_g, od_sgu_ln_b, od_spatial_w, od_spatial_b, od_w_out, ffn_norm_g, ffn_w_gate, ffn_w_up, ffn_w_down, final_norm_g):
    batch, seq, _ = x.shape
    assert seq % (max(DILATIONS) * BLOCK) == 0 and seq % FRONT_TILE == 0 and seq % TOK_TILE == 0
    row = lambda t: t.reshape(1, -1).astype(F32)
    x2 = x.reshape(batch * seq, D_MODEL)
    tabs = _rope_tables(seq)

    n_pairs = ATT_W // LANES
    head_order = [h for p in range(n_pairs) for h in (p, n_pairs + p)]
    q_cols = [h * HEAD_DIM + d for h in head_order for d in range(HEAD_DIM)]
    in_cols = np.asarray(q_cols + list(range(ATT_W, ev_w_in.shape[2])), np.int32)
    out_rows = np.asarray(q_cols + list(range(ATT_W, ev_w_out.shape[1])), np.int32)
    w_in0 = jnp.take(ev_w_in[0], in_cols, axis=1).astype(BF16)
    w_out0 = jnp.take(ev_w_out[0], out_rows, axis=0).astype(BF16)
    (q, k, v, c), (w_gate, w_up, w_down) = _ev_front(
        x2, row(ev_norm_g[0]), w_in0, tabs, (ffn_w_gate, ffn_w_up, ffn_w_down), seq)
    ffn_w = lambda l: (row(ffn_norm_g[l]), w_gate, w_up, w_down, row(final_norm_g))
    h = _ev_tail(x2, q, k, v, c, ev_sinks[0].astype(F32), ev_conv_w[0].astype(F32),
                 row(ev_conv_b[0]), row(ev_conv_ln_g[0]), row(ev_conv_ln_b[0]), w_out0,
                 ffn_w(0), 0, batch, seq)

    *qkvs, u, gate = _od_front(h, row(od_norm_g[0]), od_w_in[0].astype(BF16), tabs,
                               row(od_sgu_ln_g[0]), row(od_sgu_ln_b[0]), batch, seq)
    outs, lses = [], []
    for dil, qkv in zip(DILATIONS, qkvs):
        o_r, lse_r = _branch(qkv, batch, seq, dil)
        outs.append(o_r)
        lses.append(lse_r)
    sb = jnp.repeat(od_spatial_b[0].astype(F32).T, HEAD_DIM, axis=1)
    h = _od_tail(h, outs, lses, u, gate, od_spatial_w[0].astype(F32), sb,
                 od_w_out[0].astype(BF16), ffn_w(1), 1, seq)
    return h.reshape(batch, seq, D_MODEL)
```

```python
import functools

import jax
import jax.numpy as jnp
import numpy as np
from jax import lax
from jax.experimental import pallas as pl
from jax.experimental.pallas import tpu as pltpu

F32 = jnp.float32
BF16 = jnp.bfloat16

D_MODEL = 1024
HEAD_DIM = 64
ROT_DIM = 16
ROPE_THETA = 500000.0
BLOCK = 128
RMS_EPS = 1e-6
LN_EPS = 1e-5
N_HEADS = 8
ATT_W = N_HEADS * HEAD_DIM
CONV_CH = 512
CONV_WIDTH = 31
CONV_HALO = 32
D_CH = 512
D_GROUPS = 8
CHUNK = 128
D_FF = 2816
DILATIONS = (1, 4, 16)
LANES = 128
SUBLANES = 8
NEG = -1e30

FRONT_TILE = 1024
FRONT_PARTS = 4
TOK_TILE = 512
BRANCH_ROWS = 2048
FF_CHUNK = 256
CONV_TAPS = 16
VMEM_LIMIT = 56 * 1024 * 1024


def _cparams(n_axes):
    return pltpu.CompilerParams(
        dimension_semantics=("arbitrary",) * n_axes,
        vmem_limit_bytes=VMEM_LIMIT)


def _const_spec(shape, layer=None):
    nd = len(shape)
    if layer is None:
        return pl.BlockSpec(shape, lambda *_: (0,) * nd, pipeline_mode=pl.Buffered(1))
    return pl.BlockSpec((None,) + tuple(shape[1:]), lambda *_: (layer,) + (0,) * (nd - 1),
                        pipeline_mode=pl.Buffered(1))


def _ffn_specs(ffn, layer):
    fng, wg, wu, wd, fg = ffn
    return [_const_spec(fng.shape)] + [_const_spec(t.shape, layer) for t in (wg, wu, wd)] + [
        _const_spec(fg.shape)]


def _rms(x, g):
    ms = jnp.mean(x * x, axis=-1, keepdims=True)
    return x * lax.rsqrt(ms + RMS_EPS) * g


def _layer_norm(x, g, b):
    mu = jnp.mean(x, axis=-1, keepdims=True)
    xc = x - mu
    var = jnp.mean(xc * xc, axis=-1, keepdims=True)
    return xc * lax.rsqrt(var + LN_EPS) * g + b


def _dot(a, b):
    return jnp.dot(a, b, preferred_element_type=F32)


def _dot_nt(a, b):
    return lax.dot_general(a, b, (((1,), (1,)), ((), ())), preferred_element_type=F32)


def _rope(t, cos, sin_lo, sin_hi):
    up = pltpu.roll(t, LANES - ROT_DIM // 2, 1)
    down = pltpu.roll(t, ROT_DIM // 2, 1)
    return t * cos + up * sin_lo + down * sin_hi


def _rope_tables(seq):
    half = ROT_DIM // 2
    inv_freq = ROPE_THETA ** (-np.arange(half, dtype=np.float64) * (2.0 / ROT_DIM))
    ang = np.arange(seq, dtype=np.float64)[:, None] * inv_freq[None, :]
    cos, sin = np.cos(ang), np.sin(ang)
    ones = np.ones((seq, HEAD_DIM - ROT_DIM))
    zeros = np.zeros((seq, HEAD_DIM - ROT_DIM))
    zh = np.zeros((seq, half))
    cos_h = np.concatenate([cos, cos, ones], axis=1)
    lo_h = np.concatenate([-sin, zh, zeros], axis=1)
    hi_h = np.concatenate([zh, sin, zeros], axis=1)
    two = lambda t: jnp.asarray(np.concatenate([t, t], axis=1), F32)
    return two(cos_h), two(lo_h), two(hi_h)


def _ev_front_kernel(x_ref, g_ref, w_ref, cos_ref, lo_ref, hi_ref, *refs, n_cast):
    cast_in, (q_ref, k_ref, v_ref, c_ref) = refs[:n_cast], refs[n_cast:n_cast + 4]
    cast_out, raw = refs[n_cast + 4:2 * n_cast + 4], refs[2 * n_cast + 4]
    for src, dst in zip(cast_in, cast_out):
        dst[...] = src[...].astype(BF16)
    n_parts = raw.shape[0]
    part = x_ref.shape[0] // n_parts
    base = ATT_W + 2 * LANES

    def project(h):
        rows = slice(h * part, (h + 1) * part)
        hn = _rms(x_ref[rows, :], g_ref[...]).astype(BF16)
        raw[h, :, 0:base] = _dot(hn, w_ref[:, 0:base])
        raw[h, :, base:base + CONV_CH] = _dot(hn, w_ref[:, base:base + CONV_CH])
        raw[h, :, base + CONV_CH:] = _dot(hn, w_ref[:, base + CONV_CH:base + 2 * CONV_CH])

    def finish(h):
        rows = slice(h * part, (h + 1) * part)
        cos, lo, hi = cos_ref[rows, :], lo_ref[rows, :], hi_ref[rows, :]
        for j in range(ATT_W // LANES):
            t = raw[h, :, j * LANES:(j + 1) * LANES]
            q_ref[rows, j * LANES:(j + 1) * LANES] = (
                _rope(t, cos, lo, hi) * (HEAD_DIM ** -0.5)).astype(BF16)
        k_ref[rows, :] = _rope(raw[h, :, ATT_W:ATT_W + LANES], cos, lo, hi).astype(BF16)
        v_ref[rows, :] = raw[h, :, ATT_W + LANES:base].astype(BF16)
        c_ref[rows, :] = raw[h, :, base:base + CONV_CH] * jax.nn.sigmoid(raw[h, :, base + CONV_CH:])

    for h in range(n_parts):
        project(h)
        finish(h)


def _od_front_kernel(x_ref, g_ref, w_ref, cos_ref, lo_ref, hi_ref, lng_ref, lnb_ref,
                     qkv1_ref, qkv4_ref, qkv16_ref, u_ref, gate_ref, stage, stage4, raw):
    tile = x_ref.shape[0]
    n_parts = raw.shape[0]
    half = tile // n_parts
    d4, d16 = DILATIONS[1], DILATIONS[2]
    n_out = w_ref.shape[1]

    def project(h):
        rows = slice(h * half, (h + 1) * half)
        hn = _rms(x_ref[rows, :], g_ref[...]).astype(BF16)
        for c in range(n_out // ATT_W):
            cols = slice(c * ATT_W, (c + 1) * ATT_W)
            raw[h, :, cols] = _dot(hn, w_ref[:, cols])

    def finish(h):
        rows = slice(h * half, (h + 1) * half)
        cos, lo, hi = cos_ref[rows, :], lo_ref[rows, :], hi_ref[rows, :]
        zg = jax.nn.gelu(raw[h, :, 3 * ATT_W + D_CH:3 * ATT_W + 2 * D_CH])
        gate_ref[rows, :] = _layer_norm(zg, lng_ref[...], lnb_ref[...]).astype(BF16)
        u_ref[rows, :] = jax.nn.gelu(raw[h, :, 3 * ATT_W:3 * ATT_W + D_CH])
        for s in (2, 1, 0):
            for j in range(ATT_W // LANES):
                sl = slice(j * LANES, (j + 1) * LANES)
                t = raw[h, :, s * ATT_W + j * LANES:s * ATT_W + (j + 1) * LANES]
                buf, buf4 = stage.at[s, j], stage4.at[s, j]
                if s == 0:
                    buf[...] = _rope(t, cos, lo, hi) * (HEAD_DIM ** -0.5)
                elif s == 1:
                    buf[...] = _rope(t, cos, lo, hi)
                else:
                    buf[...] = t
                qkv1_ref[s, 0, rows, sl] = buf[...].astype(BF16)
                for r in range(d4):
                    rows4 = slice(r * (half // d4), (r + 1) * (half // d4))
                    buf4[rows4, :] = buf[pl.ds(r, half // d4, stride=d4), :]
                    qkv4_ref[s, r, h * (half // d4):(h + 1) * (half // d4), sl] = (
                        buf4[rows4, :].astype(BF16))
                for r in range(d16):
                    r4, a = r % d4, r // d4
                    qkv16_ref[s, r, h * (half // d16):(h + 1) * (half // d16), sl] = buf4[
                        pl.ds(r4 * (half // d4) + a, half // d16, stride=d16 // d4), :].astype(BF16)

    for h in range(n_parts):
        project(h)
        finish(h)


def _front_common(x2, seq):
    n_tok = x2.shape[0]
    nt = n_tok // FRONT_TILE
    per_seq = seq // FRONT_TILE
    x_spec = pl.BlockSpec((FRONT_TILE, D_MODEL), lambda i: (i, 0))
    tab_spec = pl.BlockSpec((FRONT_TILE, LANES), lambda i: (i % per_seq, 0))
    out = lambda w, dt: (jax.ShapeDtypeStruct((n_tok, w), dt),
                         pl.BlockSpec((FRONT_TILE, w), lambda i: (i, 0)))
    return nt, x_spec, tab_spec, out


def _ev_front(x2, g, w, tabs, later_weights, seq):
    nt, x_spec, tab_spec, out = _front_common(x2, seq)
    outs = [out(ATT_W, BF16), out(LANES, BF16), out(LANES, BF16), out(CONV_CH, F32)]
    flat, slab_specs, slab_outs = [], [], []
    for t in later_weights:
        rows = t.shape[0] * t.shape[1]
        slab = rows // nt
        assert slab * nt == rows and slab % (2 * SUBLANES) == 0
        flat.append(t.reshape(rows, t.shape[2]))
        slab_specs.append(pl.BlockSpec((slab, t.shape[2]), lambda i: (i, 0)))
        slab_outs.append(jax.ShapeDtypeStruct((rows, t.shape[2]), BF16))
    res = pl.pallas_call(
        functools.partial(_ev_front_kernel, n_cast=len(later_weights)),
        grid=(nt,),
        in_specs=[x_spec, _const_spec(g.shape), _const_spec(w.shape),
                  tab_spec, tab_spec, tab_spec] + slab_specs,
        out_specs=[o[1] for o in outs] + slab_specs,
        out_shape=[o[0] for o in outs] + slab_outs,
        scratch_shapes=[pltpu.VMEM((FRONT_PARTS, FRONT_TILE // FRONT_PARTS, w.shape[1]), F32)],
        compiler_params=_cparams(1),
        name="ev_front",
    )(x2, g, w, *tabs, *flat)
    return res[:4], [r.reshape(t.shape) for r, t in zip(res[4:], later_weights)]


def _od_front(x2, g, w, tabs, lng, lnb, batch, seq):
    nt, x_spec, tab_spec, out = _front_common(x2, seq)
    per_seq = seq // FRONT_TILE

    def folded(dil):
        return (jax.ShapeDtypeStruct((batch, 3, dil, seq // dil, ATT_W), BF16),
                pl.BlockSpec((None, 3, dil, FRONT_TILE // dil, ATT_W),
                             lambda i: (i // per_seq, 0, 0, i % per_seq, 0)))

    outs = [folded(d) for d in DILATIONS] + [out(D_CH, F32), out(D_CH, BF16)]
    return pl.pallas_call(
        _od_front_kernel,
        grid=(nt,),
        in_specs=[x_spec, _const_spec(g.shape), _const_spec(w.shape),
                  tab_spec, tab_spec, tab_spec,
                  _const_spec(lng.shape), _const_spec(lnb.shape)],
        out_specs=[o[1] for o in outs],
        out_shape=[o[0] for o in outs],
        scratch_shapes=[pltpu.VMEM((3, ATT_W // LANES, FRONT_TILE // FRONT_PARTS, LANES), F32)] * 2
                       + [pltpu.VMEM((FRONT_PARTS, FRONT_TILE // FRONT_PARTS, w.shape[1]), F32)],
        compiler_params=_cparams(1),
        name="od_front",
    )(x2, g, w, *tabs, lng, lnb)


def _band_masks(min_dist):
    qi = lax.broadcasted_iota(jnp.int32, (BLOCK, 2 * BLOCK), 0)
    kj = lax.broadcasted_iota(jnp.int32, (BLOCK, 2 * BLOCK), 1)
    dist = qi + BLOCK - kj
    band = (dist >= 0) & (dist <= BLOCK - 1 + min_dist)
    return band, kj < BLOCK


def _softmax_rows(s, band, prev_keys, prev_penalty):
    s = jnp.where(band, s, NEG)
    if prev_penalty is not None:
        s = jnp.where(prev_keys, s + prev_penalty, s)
    m = jnp.max(s, axis=-1, keepdims=True)
    p = jnp.exp(s - m)
    l = jnp.sum(p, axis=-1, keepdims=True)
    return p, m, l


def _conv_skew(tile):
    skew = tile // SUBLANES + 1
    assert skew % 2 == 1
    return skew


def _ev_mix_kernel(sink_ref, x_ref, q_ref, k_ref, kh_ref, v_ref, vh_ref, c_ref, ch_ref,
                   convw_ref, convb_ref, lng_ref, lnb_ref, wout_ref, o_ref,
                   kbuf, vbuf, cbuf, cout, mixbuf):
    i = pl.program_id(1)
    tile = q_ref.shape[0]
    n_pairs = ATT_W // LANES

    kbuf[0:BLOCK, :] = kh_ref[...]
    kbuf[BLOCK:, :] = k_ref[...]
    vbuf[0:BLOCK, :] = vh_ref[...]
    vbuf[BLOCK:, :] = v_ref[...]

    row = lax.broadcasted_iota(jnp.int32, (BLOCK, BLOCK), 0)
    col = lax.broadcasted_iota(jnp.int32, (BLOCK, BLOCK), 1)
    own = col <= row
    first_penalty = jnp.where(i == 0, NEG, 0.0).astype(F32)
    low_lanes = lax.broadcasted_iota(jnp.int32, (BLOCK, LANES), 1) < HEAD_DIM

    for jb in range(tile // BLOCK):
        rows = slice(jb * BLOCK, (jb + 1) * BLOCK)
        kk = kbuf[jb * BLOCK:(jb + 2) * BLOCK, :]
        vv = vbuf[jb * BLOCK:(jb + 2) * BLOCK, :]
        parts = []
        for half in range(2):
            keep = low_lanes if half == 0 else jnp.logical_not(low_lanes)
            for p in range(n_pairs):
                qp = q_ref[rows, p * LANES:(p + 1) * LANES]
                parts.append(jnp.where(keep, qp, jnp.zeros_like(qp)))
        s_all = _dot_nt(jnp.concatenate(parts, axis=0), kk)
        probs, inv_l = [], []
        for idx in range(2 * n_pairs):
            s_prev = s_all[idx * BLOCK:(idx + 1) * BLOCK, 0:BLOCK]
            s_own = s_all[idx * BLOCK:(idx + 1) * BLOCK, BLOCK:]
            if jb == 0:
                s_prev = s_prev + first_penalty
            sc = jnp.where(own, s_own, s_prev)
            sink = sink_ref[idx]
            m = jnp.maximum(jnp.max(sc, axis=-1, keepdims=True), sink)
            pr = jnp.exp(sc - m)
            l = jnp.sum(pr, axis=-1, keepdims=True) + jnp.exp(sink - m)
            zero = jnp.zeros_like(pr)
            probs.append(jnp.concatenate([jnp.where(own, zero, pr), jnp.where(own, pr, zero)],
                                         axis=1).astype(BF16))
            inv_l.append(1.0 / l)
        o_all = _dot(jnp.concatenate(probs, axis=0), vv)
        for p in range(n_pairs):
            o_lo = o_all[p * BLOCK:(p + 1) * BLOCK] * inv_l[p]
            o_hi = o_all[(n_pairs + p) * BLOCK:(n_pairs + p + 1) * BLOCK] * inv_l[n_pairs + p]
            mixbuf[rows, p * LANES:(p + 1) * LANES] = jnp.where(low_lanes, o_lo, o_hi).astype(BF16)

    n_lg = CONV_CH // LANES
    skew = _conv_skew(tile)
    lead = CONV_HALO - (CONV_WIDTH - 1)
    for lg in range(n_lg):
        lanes = slice(lg * LANES, (lg + 1) * LANES)
        cbuf[lg, 0:CONV_HALO, :] = jnp.where(i > 0, ch_ref[:, lanes], 0.0)
        cbuf[lg, CONV_HALO:CONV_HALO + tile, :] = c_ref[:, lanes]
        cbuf[lg, CONV_HALO + tile:, :] = jnp.zeros((cbuf.shape[1] - CONV_HALO - tile, LANES), F32)
    skewed = lambda ref, lg, start: ref.at[lg, pl.ds(start, SUBLANES, stride=skew), :]
    one_trip = jnp.minimum(i, 0) + 1

    def conv_lane_group(lg):
        lanes = slice(lg * LANES, (lg + 1) * LANES)
        for t0 in range(0, CONV_WIDTH, CONV_TAPS):
            n_taps = min(CONV_TAPS, CONV_WIDTH - t0)
            w = [jnp.broadcast_to(convw_ref[t0 + tt:t0 + tt + 1, lanes], (SUBLANES, LANES))
                 for tt in range(n_taps)]
            acc = {}
            for o in range(skew + n_taps - 1):
                xs = skewed(cbuf, lg, o + lead + t0)[...]
                for tt in range(n_taps):
                    j = o - tt
                    if not 0 <= j < skew:
                        continue
                    if tt == 0:
                        acc[j] = (jnp.broadcast_to(convb_ref[:, lanes], (SUBLANES, LANES))
                                  if t0 == 0 else skewed(cout, lg, j)[...])
                    acc[j] = acc[j] + xs * w[tt]
                    if tt == n_taps - 1:
                        skewed(cout, lg, j)[...] = acc.pop(j)

    for lg in range(n_lg):
        lax.fori_loop(0, one_trip, lambda _, carry, lg=lg: (conv_lane_group(lg), carry)[1], 0)
    for j in range(skew):
        y = _layer_norm(jnp.concatenate([skewed(cout, lg, j)[...] for lg in range(n_lg)], axis=1),
                        lng_ref[...], lnb_ref[...])
        y = y * jax.nn.sigmoid(y)
        for lg in range(n_lg):
            skewed(cout, lg, j)[...] = y[:, lg * LANES:(lg + 1) * LANES]
    for lg in range(n_lg):
        mixbuf[:, ATT_W + lg * LANES:ATT_W + (lg + 1) * LANES] = cout[lg, 0:tile, :].astype(BF16)

    o_ref[...] = x_ref[...] + _dot(mixbuf[...], wout_ref[...])


def _ev_tail_kernel(sink_ref, x_ref, q_ref, k_ref, kh_ref, v_ref, vh_ref, c_ref, ch_ref,
                    convw_ref, convb_ref, lng_ref, lnb_ref, wout_ref,
                    fng_ref, wg_ref, wu_ref, wd_ref, fg_ref, o_ref,
                    kbuf, vbuf, cbuf, cout, mixbuf, actbuf, h_mid):
    _ev_mix_kernel(sink_ref, x_ref, q_ref, k_ref, kh_ref, v_ref, vh_ref, c_ref, ch_ref,
                   convw_ref, convb_ref, lng_ref, lnb_ref, wout_ref, h_mid,
                   kbuf, vbuf, cbuf, cout, mixbuf)
    _ffn_kernel(h_mid, fng_ref, wg_ref, wu_ref, wd_ref, fg_ref, o_ref, actbuf, final_norm=False)


def _ev_tail(x2, q, k, v, c, sinks, convw, convb, lng, lnb, wout, ffn, layer, batch, seq):
    tile = TOK_TILE
    nt = seq // tile
    tok = lambda w: pl.BlockSpec((tile, w), lambda b, i: (b * nt + i, 0))
    halo = lambda rows, w: pl.BlockSpec(
        (rows, w), lambda b, i: (jnp.maximum((b * nt + i) * (tile // rows) - 1, 0), 0))
    consts = [convw, convb, lng, lnb, wout]
    skew_rows = SUBLANES * _conv_skew(tile)
    return pl.pallas_call(
        _ev_tail_kernel,
        grid=(batch, nt),
        in_specs=[pl.BlockSpec(memory_space=pltpu.SMEM),
                  tok(D_MODEL), tok(ATT_W), tok(LANES), halo(BLOCK, LANES),
                  tok(LANES), halo(BLOCK, LANES), tok(CONV_CH), halo(CONV_HALO, CONV_CH)]
                 + [_const_spec(t.shape) for t in consts] + _ffn_specs(ffn, layer),
        out_specs=tok(D_MODEL),
        out_shape=jax.ShapeDtypeStruct(x2.shape, F32),
        scratch_shapes=[pltpu.VMEM((tile + BLOCK, LANES), BF16),
                        pltpu.VMEM((tile + BLOCK, LANES), BF16),
                        pltpu.VMEM((CONV_CH // LANES, CONV_HALO + skew_rows, LANES), F32),
                        pltpu.VMEM((CONV_CH // LANES, skew_rows, LANES), F32),
                        pltpu.VMEM((tile, ATT_W + CONV_CH), BF16),
                        pltpu.VMEM((tile, D_FF), BF16),
                        pltpu.VMEM((tile, D_MODEL), F32)],
        compiler_params=_cparams(2),
        name="ev_tail",
    )(sinks, x2, q, k, k, v, v, c, c, *consts, *ffn)


def _branch_kernel(q_ref, k_ref, kh_ref, v_ref, vh_ref, o_ref, lse_ref, kbuf, vbuf):
    i = pl.program_id(2)
    n_res, tile = q_ref.shape[0], q_ref.shape[1]

    band, prev_keys = _band_masks(1)
    first_penalty = jnp.where(i == 0, NEG, 0.0).astype(F32)
    low_lanes = lax.broadcasted_iota(jnp.int32, (BLOCK, LANES), 1) < HEAD_DIM

    for r in range(n_res):
        kbuf[r, 0:BLOCK, :] = kh_ref[r]
        kbuf[r, BLOCK:, :] = k_ref[r]
        vbuf[r, 0:BLOCK, :] = vh_ref[r]
        vbuf[r, BLOCK:, :] = v_ref[r]
        for jb in range(tile // BLOCK):
            rows = slice(jb * BLOCK, (jb + 1) * BLOCK)
            for p in range(ATT_W // LANES):
                lanes = slice(p * LANES, (p + 1) * LANES)
                kk = kbuf[r, jb * BLOCK:(jb + 2) * BLOCK, lanes]
                vv = vbuf[r, jb * BLOCK:(jb + 2) * BLOCK, lanes]
                qp = q_ref[r, rows, lanes]
                zero = jnp.zeros_like(qp)
                q2 = jnp.concatenate([jnp.where(low_lanes, qp, zero),
                                      jnp.where(low_lanes, zero, qp)], axis=0)
                s2 = _dot_nt(q2, kk)
                probs, inv_l, lse = [], [], []
                for half in range(2):
                    pr, m, l = _softmax_rows(s2[half * BLOCK:(half + 1) * BLOCK], band,
                                             prev_keys, first_penalty if jb == 0 else None)
                    probs.append(pr.astype(BF16))
                    inv_l.append(1.0 / l)
                    lse.append(m + jnp.log(l))
                o2 = _dot(jnp.concatenate(probs, axis=0), vv)
                o_ref[r, rows, lanes] = jnp.where(low_lanes, o2[0:BLOCK] * inv_l[0],
                                                  o2[BLOCK:] * inv_l[1])
                lse_ref[r, rows, lanes] = jnp.where(low_lanes, lse[0], lse[1])


def _branch(qkv, batch, seq, dil):
    sub = seq // dil
    tile = min(BRANCH_ROWS, sub)
    n_res = BRANCH_ROWS // tile
    assert dil % n_res == 0
    nt = sub // tile

    def cur(s):
        return pl.BlockSpec((None, None, n_res, tile, ATT_W), lambda b, r, i: (b, s, r, i, 0))

    def halo(s):
        return pl.BlockSpec(
            (None, None, n_res, BLOCK, ATT_W),
            lambda b, r, i: (b, s, r, jnp.maximum(i * (tile // BLOCK) - 1, 0), 0))

    out_spec = pl.BlockSpec((None, n_res, tile, ATT_W), lambda b, r, i: (b, r, i, 0))
    out_sds = jax.ShapeDtypeStruct((batch, dil, sub, ATT_W), F32)
    return pl.pallas_call(
        _branch_kernel,
        grid=(batch, dil // n_res, nt),
        in_specs=[cur(0), cur(1), halo(1), cur(2), halo(2)],
        out_specs=[out_spec, out_spec],
        out_shape=[out_sds, out_sds],
        scratch_shapes=[pltpu.VMEM((n_res, tile + BLOCK, ATT_W), BF16),
                        pltpu.VMEM((n_res, tile + BLOCK, ATT_W), BF16)],
        compiler_params=_cparams(3),
        name=f"od_branch_d{dil}",
    )(qkv, qkv, qkv, qkv, qkv)


def _od_mix_kernel(x_ref, o1_ref, o2_ref, o3_ref, l1_ref, l2_ref, l3_ref, u_ref, gate_ref,
                   sw_ref, sb_ref, wout_ref, out_ref, mixbuf, tok_o2, tok_l2, tok_o3, tok_l3):
    tile = x_ref.shape[0]
    for j in range(ATT_W // LANES):
        sl = slice(j * LANES, (j + 1) * LANES)
        for src, dst in ((o2_ref, tok_o2), (l2_ref, tok_l2), (o3_ref, tok_o3), (l3_ref, tok_l3)):
            dil = src.shape[0]
            for r in range(dil):
                dst[j, pl.ds(r, tile // dil, stride=dil), :] = src[r, :, sl]
        l1, l2, l3 = l1_ref[0, :, sl], tok_l2[j], tok_l3[j]
        m = jnp.maximum(jnp.maximum(l1, l2), l3)
        e1, e2, e3 = jnp.exp(l1 - m), jnp.exp(l2 - m), jnp.exp(l3 - m)
        mixed = (e1 * o1_ref[0, :, sl] + e2 * tok_o2[j] + e3 * tok_o3[j]) / (e1 + e2 + e3)
        mixbuf[:, sl] = mixed.astype(BF16)

    ti = lax.broadcasted_iota(jnp.int32, (CHUNK, CHUNK), 0)
    si = lax.broadcasted_iota(jnp.int32, (CHUNK, CHUNK), 1)
    causal = ti >= si
    low_lanes = lax.broadcasted_iota(jnp.int32, (CHUNK, LANES), 1) < HEAD_DIM
    for p in range(D_CH // LANES):
        lanes = slice(p * LANES, (p + 1) * LANES)
        w2 = jnp.concatenate(
            [jnp.where(causal, sw_ref[2 * p], 0.0), jnp.where(causal, sw_ref[2 * p + 1], 0.0)],
            axis=0).astype(BF16)
        bias = sb_ref[:, lanes]
        for ch in range(tile // CHUNK):
            rows = slice(ch * CHUNK, (ch + 1) * CHUNK)
            y2 = _dot(w2, gate_ref[rows, lanes])
            y = jnp.where(low_lanes, y2[0:CHUNK], y2[CHUNK:]) + bias
            mixbuf[rows, ATT_W + p * LANES:ATT_W + (p + 1) * LANES] = (
                u_ref[rows, lanes] * y).astype(BF16)

    out_ref[...] = x_ref[...] + _dot(mixbuf[...], wout_ref[...])


def _od_tail_kernel(x_ref, o1_ref, o2_ref, o3_ref, l1_ref, l2_ref, l3_ref, u_ref, gate_ref,
                    sw_ref, sb_ref, wout_ref, fng_ref, wg_ref, wu_ref, wd_ref, fg_ref, out_ref,
                    mixbuf, tok_o2, tok_l2, tok_o3, tok_l3, actbuf, h_mid):
    _od_mix_kernel(x_ref, o1_ref, o2_ref, o3_ref, l1_ref, l2_ref, l3_ref, u_ref, gate_ref,
                   sw_ref, sb_ref, wout_ref, h_mid, mixbuf, tok_o2, tok_l2, tok_o3, tok_l3)
    _ffn_kernel(h_mid, fng_ref, wg_ref, wu_ref, wd_ref, fg_ref, out_ref, actbuf, final_norm=True)


def _od_tail(x2, outs, lses, u, gate, sw, sb, wout, ffn, layer, seq):
    tile = TOK_TILE
    per_seq = seq // tile
    tok = lambda w: pl.BlockSpec((tile, w), lambda i: (i, 0))
    folded = [pl.BlockSpec((None, d, tile // d, ATT_W),
                           lambda i: (i // per_seq, 0, i % per_seq, 0)) for d in DILATIONS]
    consts = [sw, sb, wout]
    return pl.pallas_call(
        _od_tail_kernel,
        grid=(x2.shape[0] // tile,),
        in_specs=[tok(D_MODEL)] + folded + folded + [tok(D_CH), tok(D_CH)]
                 + [_const_spec(t.shape) for t in consts] + _ffn_specs(ffn, layer),
        out_specs=tok(D_MODEL),
        out_shape=jax.ShapeDtypeStruct(x2.shape, F32),
        scratch_shapes=[pltpu.VMEM((tile, ATT_W + D_CH), BF16)]
                       + [pltpu.VMEM((ATT_W // LANES, tile, LANES), F32)] * 4
                       + [pltpu.VMEM((tile, D_FF), BF16), pltpu.VMEM((tile, D_MODEL), F32)],
        compiler_params=_cparams(1),
        name="od_tail",
    )(x2, *outs, *lses, u, gate, *consts, *ffn)


def _ffn_kernel(x_ref, g_ref, wg_ref, wu_ref, wd_ref, fg_ref, o_ref, actbuf, *, final_norm):
    hn = _rms(x_ref[...], g_ref[...]).astype(BF16)
    for c in range(D_FF // FF_CHUNK):
        cols = slice(c * FF_CHUNK, (c + 1) * FF_CHUNK)
        gate = _dot(hn, wg_ref[:, cols])
        up = _dot(hn, wu_ref[:, cols])
        actbuf[:, cols] = (gate * jax.nn.sigmoid(gate) * up).astype(BF16)
    y = x_ref[...] + _dot(actbuf[...], wd_ref[...])
    if final_norm:
        y = _rms(y, fg_ref[...])
    o_ref[...] = y


def kernel(x, ev_norm_g, ev_w_in, ev_sinks, ev_conv_w, ev_conv_b, ev_conv_ln_g, ev_conv_ln_b, ev_w_out, od_norm_g, od_w_in, od_sgu_ln_g, od_sgu_ln_b, od_spatial_w, od_spatial_b, od_w_out, ffn_norm_g, ffn_w_gate, ffn_w_up, ffn_w_down, final_norm_g):
    batch, seq, _ = x.shape
    assert seq % (max(DILATIONS) * BLOCK) == 0 and seq % FRONT_TILE == 0 and seq % TOK_TILE == 0
    row = lambda t: t.reshape(1, -1).astype(F32)
    x2 = x.reshape(batch * seq, D_MODEL)
    tabs = _rope_tables(seq)

    n_pairs = ATT_W // LANES
    head_order = [h for p in range(n_pairs) for h in (p, n_pairs + p)]
    q_cols = [h * HEAD_DIM + d for h in head_order for d in range(HEAD_DIM)]
    in_cols = np.asarray(q_cols + list(range(ATT_W, ev_w_in.shape[2])), np.int32)
    out_rows = np.asarray(q_cols + list(range(ATT_W, ev_w_out.shape[1])), np.int32)
    w_in0 = jnp.take(ev_w_in[0], in_cols, axis=1).astype(BF16)
    w_out0 = jnp.take(ev_w_out[0], out_rows, axis=0).astype(BF16)
    (q, k, v, c), (w_gate, w_up, w_down, w_in1, w_out1) = _ev_front(
        x2, row(ev_norm_g[0]), w_in0, tabs,
        (ffn_w_gate, ffn_w_up, ffn_w_down, od_w_in, od_w_out), seq)
    ffn_w = lambda l: (row(ffn_norm_g[l]), w_gate, w_up, w_down, row(final_norm_g))
    h = _ev_tail(x2, q, k, v, c, ev_sinks[0].astype(F32), ev_conv_w[0].astype(F32),
                 row(ev_conv_b[0]), row(ev_conv_ln_g[0]), row(ev_conv_ln_b[0]), w_out0,
                 ffn_w(0), 0, batch, seq)

    *qkvs, u, gate = _od_front(h, row(od_norm_g[0]), w_in1[0], tabs,
                               row(od_sgu_ln_g[0]), row(od_sgu_ln_b[0]), batch, seq)
    outs, lses = [], []
    for dil, qkv in zip(DILATIONS, qkvs):
        o_r, lse_r = _branch(qkv, batch, seq, dil)
        outs.append(o_r)
        lses.append(lse_r)
    sb = jnp.repeat(od_spatial_b[0].astype(F32).T, HEAD_DIM, axis=1)
    h = _od_tail(h, outs, lses, u, gate, od_spatial_w[0].astype(F32), sb,
                 w_out1[0], ffn_w(1), 1, seq)
    return h.reshape(batch, seq, D_MODEL)
```

```python
import jax
import jax.numpy as jnp
import numpy as np
from jax import lax
from jax.experimental import pallas as pl
from jax.experimental.pallas import tpu as pltpu

F32 = jnp.float32
BF16 = jnp.bfloat16

D_MODEL = 1024
HEAD_DIM = 64
ROT_DIM = 16
ROPE_THETA = 500000.0
BLOCK = 128
RMS_EPS = 1e-6
LN_EPS = 1e-5
N_HEADS = 8
ATT_W = N_HEADS * HEAD_DIM
CONV_CH = 512
CONV_WIDTH = 31
CONV_HALO = 32
D_CH = 512
D_GROUPS = 8
CHUNK = 128
D_FF = 2816
DILATIONS = (1, 4, 16)
LANES = 128
SUBLANES = 8
NEG = -1e30

FRONT_TILE = 1024
FRONT_PARTS = 4
TOK_TILE = 512
BRANCH_ROWS = 2048
FF_CHUNK = 256
CONV_TAPS = 16
VMEM_LIMIT = 56 * 1024 * 1024


def _cparams(n_axes):
    return pltpu.CompilerParams(
        dimension_semantics=("arbitrary",) * n_axes,
        vmem_limit_bytes=VMEM_LIMIT)


def _const_spec(shape, layer=None):
    nd = len(shape)
    if layer is None:
        return pl.BlockSpec(shape, lambda *_: (0,) * nd, pipeline_mode=pl.Buffered(1))
    return pl.BlockSpec((None,) + tuple(shape[1:]), lambda *_: (layer,) + (0,) * (nd - 1),
                        pipeline_mode=pl.Buffered(1))


def _ffn_specs(ffn, layer):
    fng, wg, wu, wd, fg = ffn
    return [_const_spec(fng.shape)] + [_const_spec(t.shape, layer) for t in (wg, wu, wd)] + [
        _const_spec(fg.shape)]


def _rms(x, g):
    ms = jnp.mean(x * x, axis=-1, keepdims=True)
    return x * lax.rsqrt(ms + RMS_EPS) * g


def _layer_norm(x, g, b):
    mu = jnp.mean(x, axis=-1, keepdims=True)
    xc = x - mu
    var = jnp.mean(xc * xc, axis=-1, keepdims=True)
    return xc * lax.rsqrt(var + LN_EPS) * g + b


def _dot(a, b):
    return jnp.dot(a, b, preferred_element_type=F32)


def _dot_nt(a, b):
    return lax.dot_general(a, b, (((1,), (1,)), ((), ())), preferred_element_type=F32)


def _rope(t, cos, sin_lo, sin_hi):
    up = pltpu.roll(t, LANES - ROT_DIM // 2, 1)
    down = pltpu.roll(t, ROT_DIM // 2, 1)
    return t * cos + up * sin_lo + down * sin_hi


def _rope_tables(seq):
    half = ROT_DIM // 2
    inv_freq = ROPE_THETA ** (-np.arange(half, dtype=np.float64) * (2.0 / ROT_DIM))
    ang = np.arange(seq, dtype=np.float64)[:, None] * inv_freq[None, :]
    cos, sin = np.cos(ang), np.sin(ang)
    ones = np.ones((seq, HEAD_DIM - ROT_DIM))
    zeros = np.zeros((seq, HEAD_DIM - ROT_DIM))
    zh = np.zeros((seq, half))
    cos_h = np.concatenate([cos, cos, ones], axis=1)
    lo_h = np.concatenate([-sin, zh, zeros], axis=1)
    hi_h = np.concatenate([zh, sin, zeros], axis=1)
    two = lambda t: jnp.asarray(np.concatenate([t, t], axis=1), F32)
    return two(cos_h), two(lo_h), two(hi_h)


def _ev_front_kernel(x_ref, g_ref, w_ref, cos_ref, lo_ref, hi_ref, wg_ref, wu_ref, wd_ref,
                     q_ref, k_ref, v_ref, c_ref, wg_out, wu_out, wd_out, raw):
    for src, dst in ((wg_ref, wg_out), (wu_ref, wu_out), (wd_ref, wd_out)):
        dst[...] = src[...].astype(BF16)
    n_parts = raw.shape[0]
    part = x_ref.shape[0] // n_parts
    base = ATT_W + 2 * LANES

    def project(h):
        rows = slice(h * part, (h + 1) * part)
        hn = _rms(x_ref[rows, :], g_ref[...]).astype(BF16)
        raw[h, :, 0:base] = _dot(hn, w_ref[:, 0:base])
        raw[h, :, base:base + CONV_CH] = _dot(hn, w_ref[:, base:base + CONV_CH])
        raw[h, :, base + CONV_CH:] = _dot(hn, w_ref[:, base + CONV_CH:base + 2 * CONV_CH])

    def finish(h):
        rows = slice(h * part, (h + 1) * part)
        cos, lo, hi = cos_ref[rows, :], lo_ref[rows, :], hi_ref[rows, :]
        for j in range(ATT_W // LANES):
            t = raw[h, :, j * LANES:(j + 1) * LANES]
            q_ref[rows, j * LANES:(j + 1) * LANES] = (
                _rope(t, cos, lo, hi) * (HEAD_DIM ** -0.5)).astype(BF16)
        k_ref[rows, :] = _rope(raw[h, :, ATT_W:ATT_W + LANES], cos, lo, hi).astype(BF16)
        v_ref[rows, :] = raw[h, :, ATT_W + LANES:base].astype(BF16)
        c_ref[rows, :] = raw[h, :, base:base + CONV_CH] * jax.nn.sigmoid(raw[h, :, base + CONV_CH:])

    for h in range(n_parts):
        project(h)
        finish(h)


def _od_front_kernel(x_ref, g_ref, w_ref, cos_ref, lo_ref, hi_ref, lng_ref, lnb_ref,
                     qkv1_ref, qkv4_ref, qkv16_ref, u_ref, gate_ref, stage, stage4, raw):
    tile = x_ref.shape[0]
    n_parts = raw.shape[0]
    half = tile // n_parts
    d4, d16 = DILATIONS[1], DILATIONS[2]
    n_out = w_ref.shape[1]

    def project(h):
        rows = slice(h * half, (h + 1) * half)
        hn = _rms(x_ref[rows, :], g_ref[...]).astype(BF16)
        for c in range(n_out // ATT_W):
            cols = slice(c * ATT_W, (c + 1) * ATT_W)
            raw[h, :, cols] = _dot(hn, w_ref[:, cols])

    def finish(h):
        rows = slice(h * half, (h + 1) * half)
        cos, lo, hi = cos_ref[rows, :], lo_ref[rows, :], hi_ref[rows, :]
        zg = jax.nn.gelu(raw[h, :, 3 * ATT_W + D_CH:3 * ATT_W + 2 * D_CH])
        gate_ref[rows, :] = _layer_norm(zg, lng_ref[...], lnb_ref[...]).astype(BF16)
        u_ref[rows, :] = jax.nn.gelu(raw[h, :, 3 * ATT_W:3 * ATT_W + D_CH])
        for s in (2, 1, 0):
            for j in range(ATT_W // LANES):
                sl = slice(j * LANES, (j + 1) * LANES)
                t = raw[h, :, s * ATT_W + j * LANES:s * ATT_W + (j + 1) * LANES]
                buf, buf4 = stage.at[s, j], stage4.at[s, j]
                if s == 0:
                    buf[...] = _rope(t, cos, lo, hi) * (HEAD_DIM ** -0.5)
                elif s == 1:
                    buf[...] = _rope(t, cos, lo, hi)
                else:
                    buf[...] = t
                qkv1_ref[s, 0, rows, sl] = buf[...].astype(BF16)
                for r in range(d4):
                    rows4 = slice(r * (half // d4), (r + 1) * (half // d4))
                    buf4[rows4, :] = buf[pl.ds(r, half // d4, stride=d4), :]
                    qkv4_ref[s, r, h * (half // d4):(h + 1) * (half // d4), sl] = (
                        buf4[rows4, :].astype(BF16))
                for r in range(d16):
                    r4, a = r % d4, r // d4
                    qkv16_ref[s, r, h * (half // d16):(h + 1) * (half // d16), sl] = buf4[
                        pl.ds(r4 * (half // d4) + a, half // d16, stride=d16 // d4), :].astype(BF16)

    for h in range(n_parts):
        project(h)
        finish(h)


def _front_common(x2, seq):
    n_tok = x2.shape[0]
    nt = n_tok // FRONT_TILE
    per_seq = seq // FRONT_TILE
    x_spec = pl.BlockSpec((FRONT_TILE, D_MODEL), lambda i: (i, 0))
    tab_spec = pl.BlockSpec((FRONT_TILE, LANES), lambda i: (i % per_seq, 0))
    out = lambda w, dt: (jax.ShapeDtypeStruct((n_tok, w), dt),
                         pl.BlockSpec((FRONT_TILE, w), lambda i: (i, 0)))
    return nt, x_spec, tab_spec, out


def _ev_front(x2, g, w, tabs, ffn_weights, seq):
    nt, x_spec, tab_spec, out = _front_common(x2, seq)
    outs = [out(ATT_W, BF16), out(LANES, BF16), out(LANES, BF16), out(CONV_CH, F32)]
    flat, slab_specs, slab_outs = [], [], []
    for t in ffn_weights:
        rows = t.shape[0] * t.shape[1]
        slab = rows // nt
        assert slab * nt == rows and slab % (2 * SUBLANES) == 0
        flat.append(t.reshape(rows, t.shape[2]))
        slab_specs.append(pl.BlockSpec((slab, t.shape[2]), lambda i: (i, 0)))
        slab_outs.append(jax.ShapeDtypeStruct((rows, t.shape[2]), BF16))
    res = pl.pallas_call(
        _ev_front_kernel,
        grid=(nt,),
        in_specs=[x_spec, _const_spec(g.shape), _const_spec(w.shape),
                  tab_spec, tab_spec, tab_spec] + slab_specs,
        out_specs=[o[1] for o in outs] + slab_specs,
        out_shape=[o[0] for o in outs] + slab_outs,
        scratch_shapes=[pltpu.VMEM((FRONT_PARTS, FRONT_TILE // FRONT_PARTS, w.shape[1]), F32)],
        compiler_params=_cparams(1),
        name="ev_front",
    )(x2, g, w, *tabs, *flat)
    return res[:4], [r.reshape(t.shape) for r, t in zip(res[4:], ffn_weights)]


def _od_front(x2, g, w, tabs, lng, lnb, batch, seq):
    nt, x_spec, tab_spec, out = _front_common(x2, seq)
    per_seq = seq // FRONT_TILE

    def folded(dil):
        return (jax.ShapeDtypeStruct((batch, 3, dil, seq // dil, ATT_W), BF16),
                pl.BlockSpec((None, 3, dil, FRONT_TILE // dil, ATT_W),
                             lambda i: (i // per_seq, 0, 0, i % per_seq, 0)))

    outs = [folded(d) for d in DILATIONS] + [out(D_CH, F32), out(D_CH, BF16)]
    return pl.pallas_call(
        _od_front_kernel,
        grid=(nt,),
        in_specs=[x_spec, _const_spec(g.shape), _const_spec(w.shape),
                  tab_spec, tab_spec, tab_spec,
                  _const_spec(lng.shape), _const_spec(lnb.shape)],
        out_specs=[o[1] for o in outs],
        out_shape=[o[0] for o in outs],
        scratch_shapes=[pltpu.VMEM((3, ATT_W // LANES, FRONT_TILE // FRONT_PARTS, LANES), F32)] * 2
                       + [pltpu.VMEM((FRONT_PARTS, FRONT_TILE // FRONT_PARTS, w.shape[1]), F32)],
        compiler_params=_cparams(1),
        name="od_front",
    )(x2, g, w, *tabs, lng, lnb)


def _band_masks(min_dist):
    qi = lax.broadcasted_iota(jnp.int32, (BLOCK, 2 * BLOCK), 0)
    kj = lax.broadcasted_iota(jnp.int32, (BLOCK, 2 * BLOCK), 1)
    dist = qi + BLOCK - kj
    band = (dist >= 0) & (dist <= BLOCK - 1 + min_dist)
    return band, kj < BLOCK


def _softmax_rows(s, band, prev_keys, prev_penalty):
    s = jnp.where(band, s, NEG)
    if prev_penalty is not None:
        s = jnp.where(prev_keys, s + prev_penalty, s)
    m = jnp.max(s, axis=-1, keepdims=True)
    return jnp.exp((s - m).astype(BF16)), m


def _conv_skew(tile):
    skew = tile // SUBLANES + 1
    assert skew % 2 == 1
    return skew


def _ev_mix_kernel(sink_ref, x_ref, q_ref, k_ref, kh_ref, v_ref, vh_ref, c_ref, ch_ref,
                   convw_ref, convb_ref, lng_ref, lnb_ref, wout_ref, o_ref,
                   kbuf, vbuf, cbuf, cout, mixbuf):
    i = pl.program_id(1)
    tile = q_ref.shape[0]
    n_pairs = ATT_W // LANES

    kbuf[0:BLOCK, :] = kh_ref[...]
    kbuf[BLOCK:, :] = k_ref[...]
    vbuf[0:BLOCK, :] = vh_ref[...]
    vbuf[BLOCK:, :] = v_ref[...]

    row = lax.broadcasted_iota(jnp.int32, (BLOCK, BLOCK), 0)
    col = lax.broadcasted_iota(jnp.int32, (BLOCK, BLOCK), 1)
    own = col <= row
    first_penalty = jnp.where(i == 0, NEG, 0.0).astype(F32)
    low_lanes = lax.broadcasted_iota(jnp.int32, (BLOCK, LANES), 1) < HEAD_DIM

    for jb in range(tile // BLOCK):
        rows = slice(jb * BLOCK, (jb + 1) * BLOCK)
        kk = kbuf[jb * BLOCK:(jb + 2) * BLOCK, :]
        vv = vbuf[jb * BLOCK:(jb + 2) * BLOCK, :]
        parts = []
        for half in range(2):
            keep = low_lanes if half == 0 else jnp.logical_not(low_lanes)
            for p in range(n_pairs):
                qp = q_ref[rows, p * LANES:(p + 1) * LANES]
                parts.append(jnp.where(keep, qp, jnp.zeros_like(qp)))
        s_all = _dot_nt(jnp.concatenate(parts, axis=0), kk)
        probs, sink_terms = [], []
        for idx in range(2 * n_pairs):
            s_prev = s_all[idx * BLOCK:(idx + 1) * BLOCK, 0:BLOCK]
            s_own = s_all[idx * BLOCK:(idx + 1) * BLOCK, BLOCK:]
            if jb == 0:
                s_prev = s_prev + first_penalty
            sc = jnp.where(own, s_own, s_prev)
            sink = sink_ref[idx]
            m = jnp.maximum(jnp.max(sc, axis=-1, keepdims=True), sink)
            pr = jnp.exp((sc - m).astype(BF16))
            zero = jnp.zeros_like(pr)
            probs.append(jnp.concatenate([jnp.where(own, zero, pr), jnp.where(own, pr, zero)],
                                         axis=1))
            sink_terms.append(jnp.exp(sink - m))
        o_all = _dot(jnp.concatenate(probs, axis=0),
                     jnp.concatenate([vv, jnp.ones_like(vv)], axis=1))

        def head_out(idx):
            blk = o_all[idx * BLOCK:(idx + 1) * BLOCK]
            return blk[:, 0:LANES] / (blk[:, LANES:] + sink_terms[idx])

        for p in range(n_pairs):
            mixbuf[rows, p * LANES:(p + 1) * LANES] = jnp.where(
                low_lanes, head_out(p), head_out(n_pairs + p)).astype(BF16)

    n_lg = CONV_CH // LANES
    skew = _conv_skew(tile)
    lead = CONV_HALO - (CONV_WIDTH - 1)
    for lg in range(n_lg):
        lanes = slice(lg * LANES, (lg + 1) * LANES)
        cbuf[lg, 0:CONV_HALO, :] = jnp.where(i > 0, ch_ref[:, lanes], 0.0)
        cbuf[lg, CONV_HALO:CONV_HALO + tile, :] = c_ref[:, lanes]
        cbuf[lg, CONV_HALO + tile:, :] = jnp.zeros((cbuf.shape[1] - CONV_HALO - tile, LANES), F32)
    skewed = lambda ref, lg, start: ref.at[lg, pl.ds(start, SUBLANES, stride=skew), :]
    one_trip = jnp.minimum(i, 0) + 1

    def conv_lane_group(lg):
        lanes = slice(lg * LANES, (lg + 1) * LANES)
        for t0 in range(0, CONV_WIDTH, CONV_TAPS):
            n_taps = min(CONV_TAPS, CONV_WIDTH - t0)
            w = [jnp.broadcast_to(convw_ref[t0 + tt:t0 + tt + 1, lanes], (SUBLANES, LANES))
                 for tt in range(n_taps)]
            acc = {}
            for o in range(skew + n_taps - 1):
                xs = skewed(cbuf, lg, o + lead + t0)[...]
                for tt in range(n_taps):
                    j = o - tt
                    if not 0 <= j < skew:
                        continue
                    if tt == 0:
                        acc[j] = (jnp.broadcast_to(convb_ref[:, lanes], (SUBLANES, LANES))
                                  if t0 == 0 else skewed(cout, lg, j)[...])
                    acc[j] = acc[j] + xs * w[tt]
                    if tt == n_taps - 1:
                        skewed(cout, lg, j)[...] = acc.pop(j)

    for lg in range(n_lg):
        lax.fori_loop(0, one_trip, lambda _, carry, lg=lg: (conv_lane_group(lg), carry)[1], 0)
    for j in range(skew):
        y = _layer_norm(jnp.concatenate([skewed(cout, lg, j)[...] for lg in range(n_lg)], axis=1),
                        lng_ref[...], lnb_ref[...])
        y = y * jax.nn.sigmoid(y)
        for lg in range(n_lg):
            skewed(cout, lg, j)[...] = y[:, lg * LANES:(lg + 1) * LANES]
    for lg in range(n_lg):
        mixbuf[:, ATT_W + lg * LANES:ATT_W + (lg + 1) * LANES] = cout[lg, 0:tile, :].astype(BF16)

    o_ref[...] = x_ref[...] + _dot(mixbuf[...], wout_ref[...])


def _ev_tail_kernel(sink_ref, x_ref, q_ref, k_ref, kh_ref, v_ref, vh_ref, c_ref, ch_ref,
                    convw_ref, convb_ref, lng_ref, lnb_ref, wout_ref,
                    fng_ref, wg_ref, wu_ref, wd_ref, fg_ref, o_ref,
                    kbuf, vbuf, cbuf, cout, mixbuf, actbuf, h_mid):
    _ev_mix_kernel(sink_ref, x_ref, q_ref, k_ref, kh_ref, v_ref, vh_ref, c_ref, ch_ref,
                   convw_ref, convb_ref, lng_ref, lnb_ref, wout_ref, h_mid,
                   kbuf, vbuf, cbuf, cout, mixbuf)
    _ffn_kernel(h_mid, fng_ref, wg_ref, wu_ref, wd_ref, fg_ref, o_ref, actbuf, final_norm=False)


def _ev_tail(x2, q, k, v, c, sinks, convw, convb, lng, lnb, wout, ffn, layer, batch, seq):
    tile = TOK_TILE
    nt = seq // tile
    tok = lambda w: pl.BlockSpec((tile, w), lambda b, i: (b * nt + i, 0))
    halo = lambda rows, w: pl.BlockSpec(
        (rows, w), lambda b, i: (jnp.maximum((b * nt + i) * (tile // rows) - 1, 0), 0))
    consts = [convw, convb, lng, lnb, wout]
    skew_rows = SUBLANES * _conv_skew(tile)
    return pl.pallas_call(
        _ev_tail_kernel,
        grid=(batch, nt),
        in_specs=[pl.BlockSpec(memory_space=pltpu.SMEM),
                  tok(D_MODEL), tok(ATT_W), tok(LANES), halo(BLOCK, LANES),
                  tok(LANES), halo(BLOCK, LANES), tok(CONV_CH), halo(CONV_HALO, CONV_CH)]
                 + [_const_spec(t.shape) for t in consts] + _ffn_specs(ffn, layer),
        out_specs=tok(D_MODEL),
        out_shape=jax.ShapeDtypeStruct(x2.shape, F32),
        scratch_shapes=[pltpu.VMEM((tile + BLOCK, LANES), BF16),
                        pltpu.VMEM((tile + BLOCK, LANES), BF16),
                        pltpu.VMEM((CONV_CH // LANES, CONV_HALO + skew_rows, LANES), F32),
                        pltpu.VMEM((CONV_CH // LANES, skew_rows, LANES), F32),
                        pltpu.VMEM((tile, ATT_W + CONV_CH), BF16),
                        pltpu.VMEM((tile, D_FF), BF16),
                        pltpu.VMEM((tile, D_MODEL), F32)],
        compiler_params=_cparams(2),
        name="ev_tail",
    )(sinks, x2, q, k, k, v, v, c, c, *consts, *ffn)


def _branch_kernel(q_ref, k_ref, kh_ref, v_ref, vh_ref, o_ref, lse_ref, kbuf, vbuf):
    i = pl.program_id(2)
    n_res, tile = q_ref.shape[0], q_ref.shape[1]

    band, prev_keys = _band_masks(1)
    first_penalty = jnp.where(i == 0, NEG, 0.0).astype(F32)
    low_lanes = lax.broadcasted_iota(jnp.int32, (BLOCK, LANES), 1) < HEAD_DIM

    for r in range(n_res):
        kbuf[r, 0:BLOCK, :] = kh_ref[r]
        kbuf[r, BLOCK:, :] = k_ref[r]
        vbuf[r, 0:BLOCK, :] = vh_ref[r]
        vbuf[r, BLOCK:, :] = v_ref[r]
        for jb in range(tile // BLOCK):
            rows = slice(jb * BLOCK, (jb + 1) * BLOCK)
            for p in range(ATT_W // LANES):
                lanes = slice(p * LANES, (p + 1) * LANES)
                kk = kbuf[r, jb * BLOCK:(jb + 2) * BLOCK, lanes]
                vv = vbuf[r, jb * BLOCK:(jb + 2) * BLOCK, lanes]
                qp = q_ref[r, rows, lanes]
                zero = jnp.zeros_like(qp)
                q2 = jnp.concatenate([jnp.where(low_lanes, qp, zero),
                                      jnp.where(low_lanes, zero, qp)], axis=0)
                s2 = _dot_nt(q2, kk)
                probs, ms = [], []
                for half in range(2):
                    pr, m = _softmax_rows(s2[half * BLOCK:(half + 1) * BLOCK], band,
                                          prev_keys, first_penalty if jb == 0 else None)
                    probs.append(pr)
                    ms.append(m)
                o2 = _dot(jnp.concatenate(probs, axis=0),
                          jnp.concatenate([vv, jnp.ones_like(vv)], axis=1))
                l0, l1 = o2[0:BLOCK, LANES:], o2[BLOCK:, LANES:]
                o_ref[r, rows, lanes] = jnp.where(low_lanes, o2[0:BLOCK, 0:LANES] / l0,
                                                  o2[BLOCK:, 0:LANES] / l1)
                lse_ref[r, rows, lanes] = jnp.where(low_lanes, ms[0] + jnp.log(l0),
                                                    ms[1] + jnp.log(l1))


def _branch(qkv, batch, seq, dil):
    sub = seq // dil
    tile = min(BRANCH_ROWS, sub)
    n_res = BRANCH_ROWS // tile
    assert dil % n_res == 0
    nt = sub // tile

    def cur(s):
        return pl.BlockSpec((None, None, n_res, tile, ATT_W), lambda b, r, i: (b, s, r, i, 0))

    def halo(s):
        return pl.BlockSpec(
            (None, None, n_res, BLOCK, ATT_W),
            lambda b, r, i: (b, s, r, jnp.maximum(i * (tile // BLOCK) - 1, 0), 0))

    out_spec = pl.BlockSpec((None, n_res, tile, ATT_W), lambda b, r, i: (b, r, i, 0))
    out_sds = jax.ShapeDtypeStruct((batch, dil, sub, ATT_W), F32)
    return pl.pallas_call(
        _branch_kernel,
        grid=(batch, dil // n_res, nt),
        in_specs=[cur(0), cur(1), halo(1), cur(2), halo(2)],
        out_specs=[out_spec, out_spec],
        out_shape=[out_sds, out_sds],
        scratch_shapes=[pltpu.VMEM((n_res, tile + BLOCK, ATT_W), BF16),
                        pltpu.VMEM((n_res, tile + BLOCK, ATT_W), BF16)],
        compiler_params=_cparams(3),
        name=f"od_branch_d{dil}",
    )(qkv, qkv, qkv, qkv, qkv)


def _od_mix_kernel(x_ref, o1_ref, o2_ref, o3_ref, l1_ref, l2_ref, l3_ref, u_ref, gate_ref,
                   sw_ref, sb_ref, wout_ref, out_ref, mixbuf, tok_o2, tok_l2, tok_o3, tok_l3):
    tile = x_ref.shape[0]
    for j in range(ATT_W // LANES):
        sl = slice(j * LANES, (j + 1) * LANES)
        for src, dst in ((o2_ref, tok_o2), (l2_ref, tok_l2), (o3_ref, tok_o3), (l3_ref, tok_l3)):
            dil = src.shape[0]
            for r in range(dil):
                dst[j, pl.ds(r, tile // dil, stride=dil), :] = src[r, :, sl]
        l1, l2, l3 = l1_ref[0, :, sl], tok_l2[j], tok_l3[j]
        m = jnp.maximum(jnp.maximum(l1, l2), l3)
        e1, e2, e3 = jnp.exp(l1 - m), jnp.exp(l2 - m), jnp.exp(l3 - m)
        mixed = (e1 * o1_ref[0, :, sl] + e2 * tok_o2[j] + e3 * tok_o3[j]) / (e1 + e2 + e3)
        mixbuf[:, sl] = mixed.astype(BF16)

    ti = lax.broadcasted_iota(jnp.int32, (CHUNK, CHUNK), 0)
    si = lax.broadcasted_iota(jnp.int32, (CHUNK, CHUNK), 1)
    causal = ti >= si
    low_lanes = lax.broadcasted_iota(jnp.int32, (CHUNK, LANES), 1) < HEAD_DIM
    for p in range(D_CH // LANES):
        lanes = slice(p * LANES, (p + 1) * LANES)
        w2 = jnp.concatenate(
            [jnp.where(causal, sw_ref[2 * p], 0.0), jnp.where(causal, sw_ref[2 * p + 1], 0.0)],
            axis=0).astype(BF16)
        bias = sb_ref[:, lanes]
        for ch in range(tile // CHUNK):
            rows = slice(ch * CHUNK, (ch + 1) * CHUNK)
            y2 = _dot(w2, gate_ref[rows, lanes])
            y = jnp.where(low_lanes, y2[0:CHUNK], y2[CHUNK:]) + bias
            mixbuf[rows, ATT_W + p * LANES:ATT_W + (p + 1) * LANES] = (
                u_ref[rows, lanes] * y).astype(BF16)

    out_ref[...] = x_ref[...] + _dot(mixbuf[...], wout_ref[...])


def _od_tail_kernel(x_ref, o1_ref, o2_ref, o3_ref, l1_ref, l2_ref, l3_ref, u_ref, gate_ref,
                    sw_ref, sb_ref, wout_ref, fng_ref, wg_ref, wu_ref, wd_ref, fg_ref, out_ref,
                    mixbuf, tok_o2, tok_l2, tok_o3, tok_l3, actbuf, h_mid):
    _od_mix_kernel(x_ref, o1_ref, o2_ref, o3_ref, l1_ref, l2_ref, l3_ref, u_ref, gate_ref,
                   sw_ref, sb_ref, wout_ref, h_mid, mixbuf, tok_o2, tok_l2, tok_o3, tok_l3)
    _ffn_kernel(h_mid, fng_ref, wg_ref, wu_ref, wd_ref, fg_ref, out_ref, actbuf, final_norm=True)


def _od_tail(x2, outs, lses, u, gate, sw, sb, wout, ffn, layer, seq):
    tile = TOK_TILE
    per_seq = seq // tile
    tok = lambda w: pl.BlockSpec((tile, w), lambda i: (i, 0))
    folded = [pl.BlockSpec((None, d, tile // d, ATT_W),
                           lambda i: (i // per_seq, 0, i % per_seq, 0)) for d in DILATIONS]
    consts = [sw, sb, wout]
    return pl.pallas_call(
        _od_tail_kernel,
        grid=(x2.shape[0] // tile,),
        in_specs=[tok(D_MODEL)] + folded + folded + [tok(D_CH), tok(D_CH)]
                 + [_const_spec(t.shape) for t in consts] + _ffn_specs(ffn, layer),
        out_specs=tok(D_MODEL),
        out_shape=jax.ShapeDtypeStruct(x2.shape, F32),
        scratch_shapes=[pltpu.VMEM((tile, ATT_W + D_CH), BF16)]
                       + [pltpu.VMEM((ATT_W // LANES, tile, LANES), F32)] * 4
                       + [pltpu.VMEM((tile, D_FF), BF16), pltpu.VMEM((tile, D_MODEL), F32)],
        compiler_params=_cparams(1),
        name="od_tail",
    )(x2, *outs, *lses, u, gate, *consts, *ffn)


def _ffn_kernel(x_ref, g_ref, wg_ref, wu_ref, wd_ref, fg_ref, o_ref, actbuf, *, final_norm):
    hn = _rms(x_ref[...], g_ref[...]).astype(BF16)
    for c in range(D_FF // FF_CHUNK):
        cols = slice(c * FF_CHUNK, (c + 1) * FF_CHUNK)
        gate = _dot(hn, wg_ref[:, cols])
        up = _dot(hn, wu_ref[:, cols])
        actbuf[:, cols] = (gate * jax.nn.sigmoid(gate) * up).astype(BF16)
    y = x_ref[...] + _dot(actbuf[...], wd_ref[...])
    if final_norm:
        y = _rms(y, fg_ref[...])
    o_ref[...] = y


def kernel(x, ev_norm_g, ev_w_in, ev_sinks, ev_conv_w, ev_conv_b, ev_conv_ln_g, ev_conv_ln_b, ev_w_out, od_norm_g, od_w_in, od_sgu_ln_g, od_sgu_ln_b, od_spatial_w, od_spatial_b, od_w_out, ffn_norm_g, ffn_w_gate, ffn_w_up, ffn_w_down, final_norm_g):
    batch, seq, _ = x.shape
    assert seq % (max(DILATIONS) * BLOCK) == 0 and seq % FRONT_TILE == 0 and seq % TOK_TILE == 0
    row = lambda t: t.reshape(1, -1).astype(F32)
    x2 = x.reshape(batch * seq, D_MODEL)
    tabs = _rope_tables(seq)

    n_pairs = ATT_W // LANES
    head_order = [h for p in range(n_pairs) for h in (p, n_pairs + p)]
    q_cols = [h * HEAD_DIM + d for h in head_order for d in range(HEAD_DIM)]
    in_cols = np.asarray(q_cols + list(range(ATT_W, ev_w_in.shape[2])), np.int32)
    out_rows = np.asarray(q_cols + list(range(ATT_W, ev_w_out.shape[1])), np.int32)
    w_in0 = jnp.take(ev_w_in[0], in_cols, axis=1).astype(BF16)
    w_out0 = jnp.take(ev_w_out[0], out_rows, axis=0).astype(BF16)
    (q, k, v, c), (w_gate, w_up, w_down) = _ev_front(
        x2, row(ev_norm_g[0]), w_in0, tabs, (ffn_w_gate, ffn_w_up, ffn_w_down), seq)
    ffn_w = lambda l: (row(ffn_norm_g[l]), w_gate, w_up, w_down, row(final_norm_g))
    h = _ev_tail(x2, q, k, v, c, ev_sinks[0].astype(F32), ev_conv_w[0].astype(F32),
                 row(ev_conv_b[0]), row(ev_conv_ln_g[0]), row(ev_conv_ln_b[0]), w_out0,
                 ffn_w(0), 0, batch, seq)

    *qkvs, u, gate = _od_front(h, row(od_norm_g[0]), od_w_in[0].astype(BF16), tabs,
                               row(od_sgu_ln_g[0]), row(od_sgu_ln_b[0]), batch, seq)
    outs, lses = [], []
    for dil, qkv in zip(DILATIONS, qkvs):
        o_r, lse_r = _branch(qkv, batch, seq, dil)
        outs.append(o_r)
        lses.append(lse_r)
    sb = jnp.repeat(od_spatial_b[0].astype(F32).T, HEAD_DIM, axis=1)
    h = _od_tail(h, outs, lses, u, gate, od_spatial_w[0].astype(F32), sb,
                 od_w_out[0].astype(BF16), ffn_w(1), 1, seq)
    return h.reshape(batch, seq, D_MODEL)
```

```python
import jax
import jax.numpy as jnp
import numpy as np
from jax import lax
from jax.experimental import pallas as pl
from jax.experimental.pallas import tpu as pltpu

F32 = jnp.float32
BF16 = jnp.bfloat16

D_MODEL = 1024
HEAD_DIM = 64
ROT_DIM = 16
ROPE_THETA = 500000.0
BLOCK = 128
RMS_EPS = 1e-6
LN_EPS = 1e-5
N_HEADS = 8
ATT_W = N_HEADS * HEAD_DIM
CONV_CH = 512
CONV_WIDTH = 31
CONV_HALO = 32
D_CH = 512
D_GROUPS = 8
CHUNK = 128
D_FF = 2816
DILATIONS = (1, 4, 16)
LANES = 128
SUBLANES = 8
NEG = -1e30

FRONT_TILE = 1024
FRONT_PARTS = 4
TOK_TILE = 512
BRANCH_ROWS = 2048
FF_CHUNK = 256
CONV_TAPS = 16
VMEM_LIMIT = 56 * 1024 * 1024


def _cparams(n_axes):
    return pltpu.CompilerParams(
        dimension_semantics=("arbitrary",) * n_axes,
        vmem_limit_bytes=VMEM_LIMIT)


def _const_spec(shape, layer=None):
    nd = len(shape)
    if layer is None:
        return pl.BlockSpec(shape, lambda *_: (0,) * nd, pipeline_mode=pl.Buffered(1))
    return pl.BlockSpec((None,) + tuple(shape[1:]), lambda *_: (layer,) + (0,) * (nd - 1),
                        pipeline_mode=pl.Buffered(1))


def _ffn_specs(ffn, layer):
    fng, wg, wu, wd, fg = ffn
    return [_const_spec(fng.shape)] + [_const_spec(t.shape, layer) for t in (wg, wu, wd)] + [
        _const_spec(fg.shape)]


def _rms(x, g):
    ms = jnp.mean(x * x, axis=-1, keepdims=True)
    return x * lax.rsqrt(ms + RMS_EPS) * g


def _layer_norm(x, g, b):
    mu = jnp.mean(x, axis=-1, keepdims=True)
    xc = x - mu
    var = jnp.mean(xc * xc, axis=-1, keepdims=True)
    return xc * lax.rsqrt(var + LN_EPS) * g + b


def _dot(a, b):
    return jnp.dot(a, b, preferred_element_type=F32)


def _dot_nt(a, b):
    return lax.dot_general(a, b, (((1,), (1,)), ((), ())), preferred_element_type=F32)


def _rope(t, cos, sin_lo, sin_hi):
    up = pltpu.roll(t, LANES - ROT_DIM // 2, 1)
    down = pltpu.roll(t, ROT_DIM // 2, 1)
    return t * cos + up * sin_lo + down * sin_hi


def _rope_tables(seq):
    half = ROT_DIM // 2
    inv_freq = ROPE_THETA ** (-np.arange(half, dtype=np.float64) * (2.0 / ROT_DIM))
    ang = np.arange(seq, dtype=np.float64)[:, None] * inv_freq[None, :]
    cos, sin = np.cos(ang), np.sin(ang)
    ones = np.ones((seq, HEAD_DIM - ROT_DIM))
    zeros = np.zeros((seq, HEAD_DIM - ROT_DIM))
    zh = np.zeros((seq, half))
    cos_h = np.concatenate([cos, cos, ones], axis=1)
    lo_h = np.concatenate([-sin, zh, zeros], axis=1)
    hi_h = np.concatenate([zh, sin, zeros], axis=1)
    two = lambda t: jnp.asarray(np.concatenate([t, t], axis=1), F32)
    return two(cos_h), two(lo_h), two(hi_h)


def _ev_front_kernel(x_ref, g_ref, w_ref, cos_ref, lo_ref, hi_ref, wg_ref, wu_ref, wd_ref,
                     q_ref, k_ref, v_ref, c_ref, wg_out, wu_out, wd_out, raw, wbuf):
    for src, dst in ((wg_ref, wg_out), (wu_ref, wu_out), (wd_ref, wd_out)):
        dst[...] = src[...].astype(BF16)

    @pl.when(pl.program_id(0) == 0)
    def _():
        n_pairs = ATT_W // LANES
        for p in range(n_pairs):
            lo_head = w_ref[:, p * HEAD_DIM:(p + 1) * HEAD_DIM]
            hi_head = w_ref[:, (n_pairs + p) * HEAD_DIM:(n_pairs + p + 1) * HEAD_DIM]
            wbuf[:, p * LANES:(p + 1) * LANES] = jnp.concatenate(
                [lo_head, hi_head], axis=1).astype(BF16)
        wbuf[:, ATT_W:] = w_ref[:, ATT_W:].astype(BF16)

    w_ref = wbuf
    n_parts = raw.shape[0]
    part = x_ref.shape[0] // n_parts
    base = ATT_W + 2 * LANES

    def project(h):
        rows = slice(h * part, (h + 1) * part)
        hn = _rms(x_ref[rows, :], g_ref[...]).astype(BF16)
        raw[h, :, 0:base] = _dot(hn, w_ref[:, 0:base])
        raw[h, :, base:base + CONV_CH] = _dot(hn, w_ref[:, base:base + CONV_CH])
        raw[h, :, base + CONV_CH:] = _dot(hn, w_ref[:, base + CONV_CH:base + 2 * CONV_CH])

    def finish(h):
        rows = slice(h * part, (h + 1) * part)
        cos, lo, hi = cos_ref[rows, :], lo_ref[rows, :], hi_ref[rows, :]
        for j in range(ATT_W // LANES):
            t = raw[h, :, j * LANES:(j + 1) * LANES]
            q_ref[rows, j * LANES:(j + 1) * LANES] = (
                _rope(t, cos, lo, hi) * (HEAD_DIM ** -0.5)).astype(BF16)
        k_ref[rows, :] = _rope(raw[h, :, ATT_W:ATT_W + LANES], cos, lo, hi).astype(BF16)
        v_ref[rows, :] = raw[h, :, ATT_W + LANES:base].astype(BF16)
        c_ref[rows, :] = raw[h, :, base:base + CONV_CH] * jax.nn.sigmoid(raw[h, :, base + CONV_CH:])

    for h in range(n_parts):
        project(h)
        finish(h)


def _od_front_kernel(x_ref, g_ref, w_ref, cos_ref, lo_ref, hi_ref, lng_ref, lnb_ref,
                     qkv1_ref, qkv4_ref, qkv16_ref, u_ref, gate_ref, stage, stage4, raw):
    tile = x_ref.shape[0]
    n_parts = raw.shape[0]
    half = tile // n_parts
    d4, d16 = DILATIONS[1], DILATIONS[2]
    n_out = w_ref.shape[1]

    def project(h):
        rows = slice(h * half, (h + 1) * half)
        hn = _rms(x_ref[rows, :], g_ref[...]).astype(BF16)
        for c in range(n_out // ATT_W):
            cols = slice(c * ATT_W, (c + 1) * ATT_W)
            raw[h, :, cols] = _dot(hn, w_ref[:, cols])

    def finish(h):
        rows = slice(h * half, (h + 1) * half)
        cos, lo, hi = cos_ref[rows, :], lo_ref[rows, :], hi_ref[rows, :]
        zg = jax.nn.gelu(raw[h, :, 3 * ATT_W + D_CH:3 * ATT_W + 2 * D_CH])
        gate_ref[rows, :] = _layer_norm(zg, lng_ref[...], lnb_ref[...]).astype(BF16)
        u_ref[rows, :] = jax.nn.gelu(raw[h, :, 3 * ATT_W:3 * ATT_W + D_CH])
        for s in (2, 1, 0):
            for j in range(ATT_W // LANES):
                sl = slice(j * LANES, (j + 1) * LANES)
                t = raw[h, :, s * ATT_W + j * LANES:s * ATT_W + (j + 1) * LANES]
                buf, buf4 = stage.at[s, j], stage4.at[s, j]
                if s == 0:
                    buf[...] = _rope(t, cos, lo, hi) * (HEAD_DIM ** -0.5)
                elif s == 1:
                    buf[...] = _rope(t, cos, lo, hi)
                else:
                    buf[...] = t
                qkv1_ref[s, 0, rows, sl] = buf[...].astype(BF16)
                for r in range(d4):
                    rows4 = slice(r * (half // d4), (r + 1) * (half // d4))
                    buf4[rows4, :] = buf[pl.ds(r, half // d4, stride=d4), :]
                    qkv4_ref[s, r, h * (half // d4):(h + 1) * (half // d4), sl] = (
                        buf4[rows4, :].astype(BF16))
                for r in range(d16):
                    r4, a = r % d4, r // d4
                    qkv16_ref[s, r, h * (half // d16):(h + 1) * (half // d16), sl] = buf4[
                        pl.ds(r4 * (half // d4) + a, half // d16, stride=d16 // d4), :].astype(BF16)

    for h in range(n_parts):
        project(h)
        finish(h)


def _front_common(x2, seq):
    n_tok = x2.shape[0]
    nt = n_tok // FRONT_TILE
    per_seq = seq // FRONT_TILE
    x_spec = pl.BlockSpec((FRONT_TILE, D_MODEL), lambda i: (i, 0))
    tab_spec = pl.BlockSpec((FRONT_TILE, LANES), lambda i: (i % per_seq, 0))
    out = lambda w, dt: (jax.ShapeDtypeStruct((n_tok, w), dt),
                         pl.BlockSpec((FRONT_TILE, w), lambda i: (i, 0)))
    return nt, x_spec, tab_spec, out


def _ev_front(x2, g, w, tabs, ffn_weights, seq):
    nt, x_spec, tab_spec, out = _front_common(x2, seq)
    outs = [out(ATT_W, BF16), out(LANES, BF16), out(LANES, BF16), out(CONV_CH, F32)]
    flat, slab_specs, slab_outs = [], [], []
    for t in ffn_weights:
        rows = t.shape[0] * t.shape[1]
        slab = rows // nt
        assert slab * nt == rows and slab % (2 * SUBLANES) == 0
        flat.append(t.reshape(rows, t.shape[2]))
        slab_specs.append(pl.BlockSpec((slab, t.shape[2]), lambda i: (i, 0)))
        slab_outs.append(jax.ShapeDtypeStruct((rows, t.shape[2]), BF16))
    res = pl.pallas_call(
        _ev_front_kernel,
        grid=(nt,),
        in_specs=[x_spec, _const_spec(g.shape), _const_spec(w.shape),
                  tab_spec, tab_spec, tab_spec] + slab_specs,
        out_specs=[o[1] for o in outs] + slab_specs,
        out_shape=[o[0] for o in outs] + slab_outs,
        scratch_shapes=[pltpu.VMEM((FRONT_PARTS, FRONT_TILE // FRONT_PARTS, w.shape[1]), F32),
                        pltpu.VMEM(w.shape, BF16)],
        compiler_params=_cparams(1),
        name="ev_front",
    )(x2, g, w, *tabs, *flat)
    return res[:4], [r.reshape(t.shape) for r, t in zip(res[4:], ffn_weights)]


def _od_front(x2, g, w, tabs, lng, lnb, batch, seq):
    nt, x_spec, tab_spec, out = _front_common(x2, seq)
    per_seq = seq // FRONT_TILE

    def folded(dil):
        return (jax.ShapeDtypeStruct((batch, 3, dil, seq // dil, ATT_W), BF16),
                pl.BlockSpec((None, 3, dil, FRONT_TILE // dil, ATT_W),
                             lambda i: (i // per_seq, 0, 0, i % per_seq, 0)))

    outs = [folded(d) for d in DILATIONS] + [out(D_CH, F32), out(D_CH, BF16)]
    return pl.pallas_call(
        _od_front_kernel,
        grid=(nt,),
        in_specs=[x_spec, _const_spec(g.shape), _const_spec(w.shape),
                  tab_spec, tab_spec, tab_spec,
                  _const_spec(lng.shape), _const_spec(lnb.shape)],
        out_specs=[o[1] for o in outs],
        out_shape=[o[0] for o in outs],
        scratch_shapes=[pltpu.VMEM((3, ATT_W // LANES, FRONT_TILE // FRONT_PARTS, LANES), F32)] * 2
                       + [pltpu.VMEM((FRONT_PARTS, FRONT_TILE // FRONT_PARTS, w.shape[1]), F32)],
        compiler_params=_cparams(1),
        name="od_front",
    )(x2, g, w, *tabs, lng, lnb)


def _band_masks(min_dist):
    qi = lax.broadcasted_iota(jnp.int32, (BLOCK, 2 * BLOCK), 0)
    kj = lax.broadcasted_iota(jnp.int32, (BLOCK, 2 * BLOCK), 1)
    dist = qi + BLOCK - kj
    band = (dist >= 0) & (dist <= BLOCK - 1 + min_dist)
    return band, kj < BLOCK


def _softmax_rows(s, band, prev_keys, prev_penalty):
    s = jnp.where(band, s, NEG)
    if prev_penalty is not None:
        s = jnp.where(prev_keys, s + prev_penalty, s)
    m = jnp.max(s, axis=-1, keepdims=True)
    return jnp.exp((s - m).astype(BF16)), m


def _conv_skew(tile):
    skew = tile // SUBLANES + 1
    assert skew % 2 == 1
    return skew


def _ev_mix_kernel(sink_ref, x_ref, q_ref, k_ref, kh_ref, v_ref, vh_ref, c_ref, ch_ref,
                   convw_ref, convb_ref, lng_ref, lnb_ref, wout_ref, o_ref,
                   kbuf, vbuf, cbuf, cout, mixbuf):
    i = pl.program_id(1)
    tile = q_ref.shape[0]
    n_pairs = ATT_W // LANES

    kbuf[0:BLOCK, :] = kh_ref[...]
    kbuf[BLOCK:, :] = k_ref[...]
    vbuf[0:BLOCK, :] = vh_ref[...]
    vbuf[BLOCK:, :] = v_ref[...]

    row = lax.broadcasted_iota(jnp.int32, (BLOCK, BLOCK), 0)
    col = lax.broadcasted_iota(jnp.int32, (BLOCK, BLOCK), 1)
    own = col <= row
    first_penalty = jnp.where(i == 0, NEG, 0.0).astype(F32)
    low_lanes = lax.broadcasted_iota(jnp.int32, (BLOCK, LANES), 1) < HEAD_DIM

    for jb in range(tile // BLOCK):
        rows = slice(jb * BLOCK, (jb + 1) * BLOCK)
        kk = kbuf[jb * BLOCK:(jb + 2) * BLOCK, :]
        vv = vbuf[jb * BLOCK:(jb + 2) * BLOCK, :]
        parts = []
        for half in range(2):
            keep = low_lanes if half == 0 else jnp.logical_not(low_lanes)
            for p in range(n_pairs):
                qp = q_ref[rows, p * LANES:(p + 1) * LANES]
                parts.append(jnp.where(keep, qp, jnp.zeros_like(qp)))
        s_all = _dot_nt(jnp.concatenate(parts, axis=0), kk)
        probs, sink_terms = [], []
        for idx in range(2 * n_pairs):
            s_prev = s_all[idx * BLOCK:(idx + 1) * BLOCK, 0:BLOCK]
            s_own = s_all[idx * BLOCK:(idx + 1) * BLOCK, BLOCK:]
            if jb == 0:
                s_prev = s_prev + first_penalty
            sc = jnp.where(own, s_own, s_prev)
            sink = sink_ref[idx]
            m = jnp.maximum(jnp.max(sc, axis=-1, keepdims=True), sink)
            pr = jnp.exp((sc - m).astype(BF16))
            zero = jnp.zeros_like(pr)
            probs.append(jnp.concatenate([jnp.where(own, zero, pr), jnp.where(own, pr, zero)],
                                         axis=1))
            sink_terms.append(jnp.exp(sink - m))
        o_all = _dot(jnp.concatenate(probs, axis=0),
                     jnp.concatenate([vv, jnp.ones_like(vv)], axis=1))

        def head_out(idx):
            blk = o_all[idx * BLOCK:(idx + 1) * BLOCK]
            return blk[:, 0:LANES] / (blk[:, LANES:] + sink_terms[idx])

        for p in range(n_pairs):
            mixbuf[rows, p * LANES:(p + 1) * LANES] = jnp.where(
                low_lanes, head_out(p), head_out(n_pairs + p)).astype(BF16)

    n_lg = CONV_CH // LANES
    skew = _conv_skew(tile)
    lead = CONV_HALO - (CONV_WIDTH - 1)
    for lg in range(n_lg):
        lanes = slice(lg * LANES, (lg + 1) * LANES)
        cbuf[lg, 0:CONV_HALO, :] = jnp.where(i > 0, ch_ref[:, lanes], 0.0)
        cbuf[lg, CONV_HALO:CONV_HALO + tile, :] = c_ref[:, lanes]
        cbuf[lg, CONV_HALO + tile:, :] = jnp.zeros((cbuf.shape[1] - CONV_HALO - tile, LANES), F32)
    skewed = lambda ref, lg, start: ref.at[lg, pl.ds(start, SUBLANES, stride=skew), :]
    one_trip = jnp.minimum(i, 0) + 1

    def conv_lane_group(lg):
        lanes = slice(lg * LANES, (lg + 1) * LANES)
        for t0 in range(0, CONV_WIDTH, CONV_TAPS):
            n_taps = min(CONV_TAPS, CONV_WIDTH - t0)
            w = [jnp.broadcast_to(convw_ref[t0 + tt:t0 + tt + 1, lanes], (SUBLANES, LANES))
                 for tt in range(n_taps)]
            acc = {}
            for o in range(skew + n_taps - 1):
                xs = skewed(cbuf, lg, o + lead + t0)[...]
                for tt in range(n_taps):
                    j = o - tt
                    if not 0 <= j < skew:
                        continue
                    if tt == 0:
                        acc[j] = (jnp.broadcast_to(convb_ref[:, lanes], (SUBLANES, LANES))
                                  if t0 == 0 else skewed(cout, lg, j)[...])
                    acc[j] = acc[j] + xs * w[tt]
                    if tt == n_taps - 1:
                        skewed(cout, lg, j)[...] = acc.pop(j)

    for lg in range(n_lg):
        lax.fori_loop(0, one_trip, lambda _, carry, lg=lg: (conv_lane_group(lg), carry)[1], 0)
    for j in range(skew):
        y = _layer_norm(jnp.concatenate([skewed(cout, lg, j)[...] for lg in range(n_lg)], axis=1),
                        lng_ref[...], lnb_ref[...])
        y = y * jax.nn.sigmoid(y)
        for lg in range(n_lg):
            skewed(cout, lg, j)[...] = y[:, lg * LANES:(lg + 1) * LANES]
    for lg in range(n_lg):
        mixbuf[:, ATT_W + lg * LANES:ATT_W + (lg + 1) * LANES] = cout[lg, 0:tile, :].astype(BF16)

    o_ref[...] = x_ref[...] + _dot(mixbuf[...], wout_ref[...])


def _ev_tail_kernel(sink_ref, x_ref, q_ref, k_ref, kh_ref, v_ref, vh_ref, c_ref, ch_ref,
                    convw_ref, convb_ref, lng_ref, lnb_ref, wout_ref,
                    fng_ref, wg_ref, wu_ref, wd_ref, fg_ref, o_ref,
                    kbuf, vbuf, cbuf, cout, mixbuf, actbuf, h_mid, woutbuf):
    @pl.when((pl.program_id(0) == 0) & (pl.program_id(1) == 0))
    def _():
        n_pairs = ATT_W // LANES
        for p in range(n_pairs):
            for half, head in enumerate((p, n_pairs + p)):
                dst = p * LANES + half * HEAD_DIM
                woutbuf[dst:dst + HEAD_DIM, :] = wout_ref[
                    head * HEAD_DIM:(head + 1) * HEAD_DIM, :].astype(BF16)
        woutbuf[ATT_W:, :] = wout_ref[ATT_W:, :].astype(BF16)

    _ev_mix_kernel(sink_ref, x_ref, q_ref, k_ref, kh_ref, v_ref, vh_ref, c_ref, ch_ref,
                   convw_ref, convb_ref, lng_ref, lnb_ref, woutbuf, h_mid,
                   kbuf, vbuf, cbuf, cout, mixbuf)
    _ffn_kernel(h_mid, fng_ref, wg_ref, wu_ref, wd_ref, fg_ref, o_ref, actbuf, final_norm=False)


def _ev_tail(x2, q, k, v, c, sinks, convw, convb, lng, lnb, wout, ffn, layer, batch, seq):
    tile = TOK_TILE
    nt = seq // tile
    tok = lambda w: pl.BlockSpec((tile, w), lambda b, i: (b * nt + i, 0))
    halo = lambda rows, w: pl.BlockSpec(
        (rows, w), lambda b, i: (jnp.maximum((b * nt + i) * (tile // rows) - 1, 0), 0))
    consts = [convw, convb, lng, lnb, wout]
    skew_rows = SUBLANES * _conv_skew(tile)
    return pl.pallas_call(
        _ev_tail_kernel,
        grid=(batch, nt),
        in_specs=[pl.BlockSpec(memory_space=pltpu.SMEM),
                  tok(D_MODEL), tok(ATT_W), tok(LANES), halo(BLOCK, LANES),
                  tok(LANES), halo(BLOCK, LANES), tok(CONV_CH), halo(CONV_HALO, CONV_CH)]
                 + [_const_spec(t.shape) for t in consts] + _ffn_specs(ffn, layer),
        out_specs=tok(D_MODEL),
        out_shape=jax.ShapeDtypeStruct(x2.shape, F32),
        scratch_shapes=[pltpu.VMEM((tile + BLOCK, LANES), BF16),
                        pltpu.VMEM((tile + BLOCK, LANES), BF16),
                        pltpu.VMEM((CONV_CH // LANES, CONV_HALO + skew_rows, LANES), F32),
                        pltpu.VMEM((CONV_CH // LANES, skew_rows, LANES), F32),
                        pltpu.VMEM((tile, ATT_W + CONV_CH), BF16),
                        pltpu.VMEM((tile, D_FF), BF16),
                        pltpu.VMEM((tile, D_MODEL), F32),
                        pltpu.VMEM(wout.shape, BF16)],
        compiler_params=_cparams(2),
        name="ev_tail",
    )(sinks, x2, q, k, k, v, v, c, c, *consts, *ffn)


def _branch_kernel(q_ref, k_ref, kh_ref, v_ref, vh_ref, o_ref, lse_ref, kbuf, vbuf):
    i = pl.program_id(2)
    n_res, tile = q_ref.shape[0], q_ref.shape[1]

    band, prev_keys = _band_masks(1)
    first_penalty = jnp.where(i == 0, NEG, 0.0).astype(F32)
    low_lanes = lax.broadcasted_iota(jnp.int32, (BLOCK, LANES), 1) < HEAD_DIM

    for r in range(n_res):
        kbuf[r, 0:BLOCK, :] = kh_ref[r]
        kbuf[r, BLOCK:, :] = k_ref[r]
        vbuf[r, 0:BLOCK, :] = vh_ref[r]
        vbuf[r, BLOCK:, :] = v_ref[r]
        for jb in range(tile // BLOCK):
            rows = slice(jb * BLOCK, (jb + 1) * BLOCK)
            for p in range(ATT_W // LANES):
                lanes = slice(p * LANES, (p + 1) * LANES)
                kk = kbuf[r, jb * BLOCK:(jb + 2) * BLOCK, lanes]
                vv = vbuf[r, jb * BLOCK:(jb + 2) * BLOCK, lanes]
                qp = q_ref[r, rows, lanes]
                zero = jnp.zeros_like(qp)
                q2 = jnp.concatenate([jnp.where(low_lanes, qp, zero),
                                      jnp.where(low_lanes, zero, qp)], axis=0)
                s2 = _dot_nt(q2, kk)
                probs, ms = [], []
                for half in range(2):
                    pr, m = _softmax_rows(s2[half * BLOCK:(half + 1) * BLOCK], band,
                                          prev_keys, first_penalty if jb == 0 else None)
                    probs.append(pr)
                    ms.append(m)
                o2 = _dot(jnp.concatenate(probs, axis=0),
                          jnp.concatenate([vv, jnp.ones_like(vv)], axis=1))
                l0, l1 = o2[0:BLOCK, LANES:], o2[BLOCK:, LANES:]
                o_ref[r, rows, lanes] = jnp.where(low_lanes, o2[0:BLOCK, 0:LANES] / l0,
                                                  o2[BLOCK:, 0:LANES] / l1)
                lse_ref[r, rows, lanes] = jnp.where(low_lanes, ms[0] + jnp.log(l0),
                                                    ms[1] + jnp.log(l1))


def _branch(qkv, batch, seq, dil):
    sub = seq // dil
    tile = min(BRANCH_ROWS, sub)
    n_res = BRANCH_ROWS // tile
    assert dil % n_res == 0
    nt = sub // tile

    def cur(s):
        return pl.BlockSpec((None, None, n_res, tile, ATT_W), lambda b, r, i: (b, s, r, i, 0))

    def halo(s):
        return pl.BlockSpec(
            (None, None, n_res, BLOCK, ATT_W),
            lambda b, r, i: (b, s, r, jnp.maximum(i * (tile // BLOCK) - 1, 0), 0))

    out_spec = pl.BlockSpec((None, n_res, tile, ATT_W), lambda b, r, i: (b, r, i, 0))
    out_sds = jax.ShapeDtypeStruct((batch, dil, sub, ATT_W), F32)
    return pl.pallas_call(
        _branch_kernel,
        grid=(batch, dil // n_res, nt),
        in_specs=[cur(0), cur(1), halo(1), cur(2), halo(2)],
        out_specs=[out_spec, out_spec],
        out_shape=[out_sds, out_sds],
        scratch_shapes=[pltpu.VMEM((n_res, tile + BLOCK, ATT_W), BF16),
                        pltpu.VMEM((n_res, tile + BLOCK, ATT_W), BF16)],
        compiler_params=_cparams(3),
        name=f"od_branch_d{dil}",
    )(qkv, qkv, qkv, qkv, qkv)


def _od_mix_kernel(x_ref, o1_ref, o2_ref, o3_ref, l1_ref, l2_ref, l3_ref, u_ref, gate_ref,
                   sw_ref, sb_ref, wout_ref, out_ref, mixbuf, tok_o2, tok_l2, tok_o3, tok_l3):
    tile = x_ref.shape[0]
    for j in range(ATT_W // LANES):
        sl = slice(j * LANES, (j + 1) * LANES)
        for src, dst in ((o2_ref, tok_o2), (l2_ref, tok_l2), (o3_ref, tok_o3), (l3_ref, tok_l3)):
            dil = src.shape[0]
            for r in range(dil):
                dst[j, pl.ds(r, tile // dil, stride=dil), :] = src[r, :, sl]
        l1, l2, l3 = l1_ref[0, :, sl], tok_l2[j], tok_l3[j]
        m = jnp.maximum(jnp.maximum(l1, l2), l3)
        e1, e2, e3 = jnp.exp(l1 - m), jnp.exp(l2 - m), jnp.exp(l3 - m)
        mixed = (e1 * o1_ref[0, :, sl] + e2 * tok_o2[j] + e3 * tok_o3[j]) / (e1 + e2 + e3)
        mixbuf[:, sl] = mixed.astype(BF16)

    ti = lax.broadcasted_iota(jnp.int32, (CHUNK, CHUNK), 0)
    si = lax.broadcasted_iota(jnp.int32, (CHUNK, CHUNK), 1)
    causal = ti >= si
    low_lanes = lax.broadcasted_iota(jnp.int32, (CHUNK, LANES), 1) < HEAD_DIM
    for p in range(D_CH // LANES):
        lanes = slice(p * LANES, (p + 1) * LANES)
        w2 = jnp.concatenate(
            [jnp.where(causal, sw_ref[2 * p], 0.0), jnp.where(causal, sw_ref[2 * p + 1], 0.0)],
            axis=0).astype(BF16)
        bias = sb_ref[:, lanes]
        for ch in range(tile // CHUNK):
            rows = slice(ch * CHUNK, (ch + 1) * CHUNK)
            y2 = _dot(w2, gate_ref[rows, lanes])
            y = jnp.where(low_lanes, y2[0:CHUNK], y2[CHUNK:]) + bias
            mixbuf[rows, ATT_W + p * LANES:ATT_W + (p + 1) * LANES] = (
                u_ref[rows, lanes] * y).astype(BF16)

    out_ref[...] = x_ref[...] + _dot(mixbuf[...], wout_ref[...])


def _od_tail_kernel(x_ref, o1_ref, o2_ref, o3_ref, l1_ref, l2_ref, l3_ref, u_ref, gate_ref,
                    sw_ref, sb_ref, wout_ref, fng_ref, wg_ref, wu_ref, wd_ref, fg_ref, out_ref,
                    mixbuf, tok_o2, tok_l2, tok_o3, tok_l3, actbuf, h_mid):
    _od_mix_kernel(x_ref, o1_ref, o2_ref, o3_ref, l1_ref, l2_ref, l3_ref, u_ref, gate_ref,
                   sw_ref, sb_ref, wout_ref, h_mid, mixbuf, tok_o2, tok_l2, tok_o3, tok_l3)
    _ffn_kernel(h_mid, fng_ref, wg_ref, wu_ref, wd_ref, fg_ref, out_ref, actbuf, final_norm=True)


def _od_tail(x2, outs, lses, u, gate, sw, sb, wout, ffn, layer, seq):
    tile = TOK_TILE
    per_seq = seq // tile
    tok = lambda w: pl.BlockSpec((tile, w), lambda i: (i, 0))
    folded = [pl.BlockSpec((None, d, tile // d, ATT_W),
                           lambda i: (i // per_seq, 0, i % per_seq, 0)) for d in DILATIONS]
    consts = [sw, sb, wout]
    return pl.pallas_call(
        _od_tail_kernel,
        grid=(x2.shape[0] // tile,),
        in_specs=[tok(D_MODEL)] + folded + folded + [tok(D_CH), tok(D_CH)]
                 + [_const_spec(t.shape) for t in consts] + _ffn_specs(ffn, layer),
        out_specs=tok(D_MODEL),
        out_shape=jax.ShapeDtypeStruct(x2.shape, F32),
        scratch_shapes=[pltpu.VMEM((tile, ATT_W + D_CH), BF16)]
                       + [pltpu.VMEM((ATT_W // LANES, tile, LANES), F32)] * 4
                       + [pltpu.VMEM((tile, D_FF), BF16), pltpu.VMEM((tile, D_MODEL), F32)],
        compiler_params=_cparams(1),
        name="od_tail",
    )(x2, *outs, *lses, u, gate, *consts, *ffn)


def _ffn_kernel(x_ref, g_ref, wg_ref, wu_ref, wd_ref, fg_ref, o_ref, actbuf, *, final_norm):
    hn = _rms(x_ref[...], g_ref[...]).astype(BF16)
    for c in range(D_FF // FF_CHUNK):
        cols = slice(c * FF_CHUNK, (c + 1) * FF_CHUNK)
        gate = _dot(hn, wg_ref[:, cols])
        up = _dot(hn, wu_ref[:, cols])
        actbuf[:, cols] = (gate * jax.nn.sigmoid(gate) * up).astype(BF16)
    y = x_ref[...] + _dot(actbuf[...], wd_ref[...])
    if final_norm:
        y = _rms(y, fg_ref[...])
    o_ref[...] = y


def kernel(x, ev_norm_g, ev_w_in, ev_sinks, ev_conv_w, ev_conv_b, ev_conv_ln_g, ev_conv_ln_b, ev_w_out, od_norm_g, od_w_in, od_sgu_ln_g, od_sgu_ln_b, od_spatial_w, od_spatial_b, od_w_out, ffn_norm_g, ffn_w_gate, ffn_w_up, ffn_w_down, final_norm_g):
    batch, seq, _ = x.shape
    assert seq % (max(DILATIONS) * BLOCK) == 0 and seq % FRONT_TILE == 0 and seq % TOK_TILE == 0
    row = lambda t: t.reshape(1, -1).astype(F32)
    x2 = x.reshape(batch * seq, D_MODEL)
    tabs = _rope_tables(seq)

    n_pairs = ATT_W // LANES
    (q, k, v, c), (w_gate, w_up, w_down) = _ev_front(
        x2, row(ev_norm_g[0]), ev_w_in[0], tabs, (ffn_w_gate, ffn_w_up, ffn_w_down), seq)
    ffn_w = lambda l: (row(ffn_norm_g[l]), w_gate, w_up, w_down, row(final_norm_g))
    h = _ev_tail(x2, q, k, v, c, ev_sinks[0].astype(F32), ev_conv_w[0].astype(F32),
                 row(ev_conv_b[0]), row(ev_conv_ln_g[0]), row(ev_conv_ln_b[0]), ev_w_out[0],
                 ffn_w(0), 0, batch, seq)

    *qkvs, u, gate = _od_front(h, row(od_norm_g[0]), od_w_in[0].astype(BF16), tabs,
                               row(od_sgu_ln_g[0]), row(od_sgu_ln_b[0]), batch, seq)
    outs, lses = [], []
    for dil, qkv in zip(DILATIONS, qkvs):
        o_r, lse_r = _branch(qkv, batch, seq, dil)
        outs.append(o_r)
        lses.append(lse_r)
    sb = jnp.repeat(od_spatial_b[0].astype(F32).T, HEAD_DIM, axis=1)
    h = _od_tail(h, outs, lses, u, gate, od_spatial_w[0].astype(F32), sb,
                 od_w_out[0].astype(BF16), ffn_w(1), 1, seq)
    return h.reshape(batch, seq, D_MODEL)
```

```python
import functools

import jax
import jax.numpy as jnp
import numpy as np
from jax import lax
from jax.experimental import pallas as pl
from jax.experimental.pallas import tpu as pltpu

F32 = jnp.float32
BF16 = jnp.bfloat16

D_MODEL = 1024
HEAD_DIM = 64
ROT_DIM = 16
ROPE_THETA = 500000.0
BLOCK = 128
RMS_EPS = 1e-6
LN_EPS = 1e-5
N_HEADS = 8
ATT_W = N_HEADS * HEAD_DIM
CONV_CH = 512
CONV_WIDTH = 31
CONV_HALO = 32
D_CH = 512
D_GROUPS = 8
CHUNK = 128
D_FF = 2816
DILATIONS = (1, 4, 16)
LANES = 128
SUBLANES = 8
NEG = -1e30

FRONT_TILE = 1024
FRONT_PARTS = 4
TOK_TILE = 512
BRANCH_ROWS = 2048
FF_CHUNK = 256
CONV_TAPS = 16
VMEM_LIMIT = 56 * 1024 * 1024


def _cparams(n_axes):
    return pltpu.CompilerParams(
        dimension_semantics=("arbitrary",) * n_axes,
        vmem_limit_bytes=VMEM_LIMIT)


def _const_spec(shape, layer=None):
    nd = len(shape)
    if layer is None:
        return pl.BlockSpec(shape, lambda *_: (0,) * nd, pipeline_mode=pl.Buffered(1))
    return pl.BlockSpec((None,) + tuple(shape[1:]), lambda *_: (layer,) + (0,) * (nd - 1),
                        pipeline_mode=pl.Buffered(1))


def _ffn_specs(ffn, layer):
    fng, wg, wu, wd, fg = ffn
    return [_const_spec(fng.shape)] + [_const_spec(t.shape, layer) for t in (wg, wu, wd)] + [
        _const_spec(fg.shape)]


def _rms(x, g):
    ms = jnp.mean(x * x, axis=-1, keepdims=True)
    return x * lax.rsqrt(ms + RMS_EPS) * g


def _layer_norm(x, g, b):
    mu = jnp.mean(x, axis=-1, keepdims=True)
    xc = x - mu
    var = jnp.mean(xc * xc, axis=-1, keepdims=True)
    return xc * lax.rsqrt(var + LN_EPS) * g + b


def _dot(a, b):
    return jnp.dot(a, b, preferred_element_type=F32)


def _dot_nt(a, b):
    return lax.dot_general(a, b, (((1,), (1,)), ((), ())), preferred_element_type=F32)


def _rope(t, cos, sin_lo, sin_hi):
    up = pltpu.roll(t, LANES - ROT_DIM // 2, 1)
    down = pltpu.roll(t, ROT_DIM // 2, 1)
    return t * cos + up * sin_lo + down * sin_hi


def _rope_tables(seq):
    half = ROT_DIM // 2
    inv_freq = ROPE_THETA ** (-np.arange(half, dtype=np.float64) * (2.0 / ROT_DIM))
    ang = np.arange(seq, dtype=np.float64)[:, None] * inv_freq[None, :]
    cos, sin = np.cos(ang), np.sin(ang)
    ones = np.ones((seq, HEAD_DIM - ROT_DIM))
    zeros = np.zeros((seq, HEAD_DIM - ROT_DIM))
    zh = np.zeros((seq, half))
    cos_h = np.concatenate([cos, cos, ones], axis=1)
    lo_h = np.concatenate([-sin, zh, zeros], axis=1)
    hi_h = np.concatenate([zh, sin, zeros], axis=1)
    two = lambda t: jnp.asarray(np.concatenate([t, t], axis=1), F32)
    return two(cos_h), two(lo_h), two(hi_h)


def _ev_front_kernel(x_ref, g_ref, w_ref, cos_ref, lo_ref, hi_ref, *refs, n_cast):
    cast_in, (q_ref, k_ref, v_ref, c_ref) = refs[:n_cast], refs[n_cast:n_cast + 4]
    cast_out, (raw, wbuf) = refs[n_cast + 4:2 * n_cast + 4], refs[2 * n_cast + 4:]
    for src, dst in zip(cast_in, cast_out):
        dst[...] = src[...].astype(BF16)

    @pl.when(pl.program_id(0) == 0)
    def _():
        n_pairs = ATT_W // LANES
        for p in range(n_pairs):
            lo_head = w_ref[:, p * HEAD_DIM:(p + 1) * HEAD_DIM]
            hi_head = w_ref[:, (n_pairs + p) * HEAD_DIM:(n_pairs + p + 1) * HEAD_DIM]
            wbuf[:, p * LANES:(p + 1) * LANES] = jnp.concatenate(
                [lo_head, hi_head], axis=1).astype(BF16)
        wbuf[:, ATT_W:] = w_ref[:, ATT_W:].astype(BF16)

    w_ref = wbuf
    n_parts = raw.shape[0]
    part = x_ref.shape[0] // n_parts
    base = ATT_W + 2 * LANES

    def project(h):
        rows = slice(h * part, (h + 1) * part)
        hn = _rms(x_ref[rows, :], g_ref[...]).astype(BF16)
        raw[h, :, 0:base] = _dot(hn, w_ref[:, 0:base])
        raw[h, :, base:base + CONV_CH] = _dot(hn, w_ref[:, base:base + CONV_CH])
        raw[h, :, base + CONV_CH:] = _dot(hn, w_ref[:, base + CONV_CH:base + 2 * CONV_CH])

    def finish(h):
        rows = slice(h * part, (h + 1) * part)
        cos, lo, hi = cos_ref[rows, :], lo_ref[rows, :], hi_ref[rows, :]
        for j in range(ATT_W // LANES):
            t = raw[h, :, j * LANES:(j + 1) * LANES]
            q_ref[rows, j * LANES:(j + 1) * LANES] = (
                _rope(t, cos, lo, hi) * (HEAD_DIM ** -0.5)).astype(BF16)
        k_ref[rows, :] = _rope(raw[h, :, ATT_W:ATT_W + LANES], cos, lo, hi).astype(BF16)
        v_ref[rows, :] = raw[h, :, ATT_W + LANES:base].astype(BF16)
        c_ref[rows, :] = raw[h, :, base:base + CONV_CH] * jax.nn.sigmoid(raw[h, :, base + CONV_CH:])

    for h in range(n_parts):
        project(h)
        finish(h)


def _od_front_kernel(x_ref, g_ref, w_ref, cos_ref, lo_ref, hi_ref, lng_ref, lnb_ref,
                     qkv1_ref, qkv4_ref, qkv16_ref, u_ref, gate_ref, stage, stage4, raw):
    tile = x_ref.shape[0]
    n_parts = raw.shape[0]
    half = tile // n_parts
    d4, d16 = DILATIONS[1], DILATIONS[2]
    n_out = w_ref.shape[1]

    def project(h):
        rows = slice(h * half, (h + 1) * half)
        hn = _rms(x_ref[rows, :], g_ref[...]).astype(BF16)
        for c in range(n_out // ATT_W):
            cols = slice(c * ATT_W, (c + 1) * ATT_W)
            raw[h, :, cols] = _dot(hn, w_ref[:, cols])

    def finish(h):
        rows = slice(h * half, (h + 1) * half)
        cos, lo, hi = cos_ref[rows, :], lo_ref[rows, :], hi_ref[rows, :]
        zg = jax.nn.gelu(raw[h, :, 3 * ATT_W + D_CH:3 * ATT_W + 2 * D_CH])
        gate_ref[rows, :] = _layer_norm(zg, lng_ref[...], lnb_ref[...]).astype(BF16)
        u_ref[rows, :] = jax.nn.gelu(raw[h, :, 3 * ATT_W:3 * ATT_W + D_CH])
        for s in (2, 1, 0):
            for j in range(ATT_W // LANES):
                sl = slice(j * LANES, (j + 1) * LANES)
                t = raw[h, :, s * ATT_W + j * LANES:s * ATT_W + (j + 1) * LANES]
                buf, buf4 = stage.at[s, j], stage4.at[s, j]
                if s == 0:
                    buf[...] = _rope(t, cos, lo, hi) * (HEAD_DIM ** -0.5)
                elif s == 1:
                    buf[...] = _rope(t, cos, lo, hi)
                else:
                    buf[...] = t
                qkv1_ref[s, 0, rows, sl] = buf[...].astype(BF16)
                for r in range(d4):
                    rows4 = slice(r * (half // d4), (r + 1) * (half // d4))
                    buf4[rows4, :] = buf[pl.ds(r, half // d4, stride=d4), :]
                    qkv4_ref[s, r, h * (half // d4):(h + 1) * (half // d4), sl] = (
                        buf4[rows4, :].astype(BF16))
                for r in range(d16):
                    r4, a = r % d4, r // d4
                    qkv16_ref[s, r, h * (half // d16):(h + 1) * (half // d16), sl] = buf4[
                        pl.ds(r4 * (half // d4) + a, half // d16, stride=d16 // d4), :].astype(BF16)

    for h in range(n_parts):
        project(h)
        finish(h)


def _front_common(x2, seq):
    n_tok = x2.shape[0]
    nt = n_tok // FRONT_TILE
    per_seq = seq // FRONT_TILE
    x_spec = pl.BlockSpec((FRONT_TILE, D_MODEL), lambda i: (i, 0))
    tab_spec = pl.BlockSpec((FRONT_TILE, LANES), lambda i: (i % per_seq, 0))
    out = lambda w, dt: (jax.ShapeDtypeStruct((n_tok, w), dt),
                         pl.BlockSpec((FRONT_TILE, w), lambda i: (i, 0)))
    return nt, x_spec, tab_spec, out


def _ev_front(x2, g, w, tabs, later_weights, seq):
    nt, x_spec, tab_spec, out = _front_common(x2, seq)
    outs = [out(ATT_W, BF16), out(LANES, BF16), out(LANES, BF16), out(CONV_CH, F32)]
    flat, slab_specs, slab_outs = [], [], []
    for t in later_weights:
        rows = t.shape[0] * t.shape[1]
        slab = rows // nt
        assert slab * nt == rows and slab % (2 * SUBLANES) == 0
        flat.append(t.reshape(rows, t.shape[2]))
        slab_specs.append(pl.BlockSpec((slab, t.shape[2]), lambda i: (i, 0)))
        slab_outs.append(jax.ShapeDtypeStruct((rows, t.shape[2]), BF16))
    res = pl.pallas_call(
        functools.partial(_ev_front_kernel, n_cast=len(later_weights)),
        grid=(nt,),
        in_specs=[x_spec, _const_spec(g.shape), _const_spec(w.shape),
                  tab_spec, tab_spec, tab_spec] + slab_specs,
        out_specs=[o[1] for o in outs] + slab_specs,
        out_shape=[o[0] for o in outs] + slab_outs,
        scratch_shapes=[pltpu.VMEM((FRONT_PARTS, FRONT_TILE // FRONT_PARTS, w.shape[1]), F32),
                        pltpu.VMEM(w.shape, BF16)],
        compiler_params=_cparams(1),
        name="ev_front",
    )(x2, g, w, *tabs, *flat)
    return res[:4], [r.reshape(t.shape) for r, t in zip(res[4:], later_weights)]


def _od_front(x2, g, w, tabs, lng, lnb, batch, seq):
    nt, x_spec, tab_spec, out = _front_common(x2, seq)
    per_seq = seq // FRONT_TILE

    def folded(dil):
        return (jax.ShapeDtypeStruct((batch, 3, dil, seq // dil, ATT_W), BF16),
                pl.BlockSpec((None, 3, dil, FRONT_TILE // dil, ATT_W),
                             lambda i: (i // per_seq, 0, 0, i % per_seq, 0)))

    outs = [folded(d) for d in DILATIONS] + [out(D_CH, F32), out(D_CH, BF16)]
    return pl.pallas_call(
        _od_front_kernel,
        grid=(nt,),
        in_specs=[x_spec, _const_spec(g.shape), _const_spec(w.shape),
                  tab_spec, tab_spec, tab_spec,
                  _const_spec(lng.shape), _const_spec(lnb.shape)],
        out_specs=[o[1] for o in outs],
        out_shape=[o[0] for o in outs],
        scratch_shapes=[pltpu.VMEM((3, ATT_W // LANES, FRONT_TILE // FRONT_PARTS, LANES), F32)] * 2
                       + [pltpu.VMEM((FRONT_PARTS, FRONT_TILE // FRONT_PARTS, w.shape[1]), F32)],
        compiler_params=_cparams(1),
        name="od_front",
    )(x2, g, w, *tabs, lng, lnb)


def _band_masks(min_dist):
    qi = lax.broadcasted_iota(jnp.int32, (BLOCK, 2 * BLOCK), 0)
    kj = lax.broadcasted_iota(jnp.int32, (BLOCK, 2 * BLOCK), 1)
    dist = qi + BLOCK - kj
    band = (dist >= 0) & (dist <= BLOCK - 1 + min_dist)
    return band, kj < BLOCK


def _softmax_rows(s, band, prev_keys, prev_penalty):
    s = jnp.where(band, s, NEG)
    if prev_penalty is not None:
        s = jnp.where(prev_keys, s + prev_penalty, s)
    m = jnp.max(s, axis=-1, keepdims=True)
    return jnp.exp((s - m).astype(BF16)), m


def _conv_skew(tile):
    skew = tile // SUBLANES + 1
    assert skew % 2 == 1
    return skew


def _ev_mix_kernel(sink_ref, x_ref, q_ref, k_ref, kh_ref, v_ref, vh_ref, c_ref, ch_ref,
                   convw_ref, convb_ref, lng_ref, lnb_ref, wout_ref, o_ref,
                   kbuf, vbuf, cbuf, cout, mixbuf):
    i = pl.program_id(1)
    tile = q_ref.shape[0]
    n_pairs = ATT_W // LANES

    kbuf[0:BLOCK, :] = kh_ref[...]
    kbuf[BLOCK:, :] = k_ref[...]
    vbuf[0:BLOCK, :] = vh_ref[...]
    vbuf[BLOCK:, :] = v_ref[...]

    row = lax.broadcasted_iota(jnp.int32, (BLOCK, BLOCK), 0)
    col = lax.broadcasted_iota(jnp.int32, (BLOCK, BLOCK), 1)
    own = col <= row
    first_penalty = jnp.where(i == 0, NEG, 0.0).astype(F32)
    low_lanes = lax.broadcasted_iota(jnp.int32, (BLOCK, LANES), 1) < HEAD_DIM

    for jb in range(tile // BLOCK):
        rows = slice(jb * BLOCK, (jb + 1) * BLOCK)
        kk = kbuf[jb * BLOCK:(jb + 2) * BLOCK, :]
        vv = vbuf[jb * BLOCK:(jb + 2) * BLOCK, :]
        parts = []
        for half in range(2):
            keep = low_lanes if half == 0 else jnp.logical_not(low_lanes)
            for p in range(n_pairs):
                qp = q_ref[rows, p * LANES:(p + 1) * LANES]
                parts.append(jnp.where(keep, qp, jnp.zeros_like(qp)))
        s_all = _dot_nt(jnp.concatenate(parts, axis=0), kk)
        probs, sink_terms = [], []
        for idx in range(2 * n_pairs):
            s_prev = s_all[idx * BLOCK:(idx + 1) * BLOCK, 0:BLOCK]
            s_own = s_all[idx * BLOCK:(idx + 1) * BLOCK, BLOCK:]
            if jb == 0:
                s_prev = s_prev + first_penalty
            sc = jnp.where(own, s_own, s_prev)
            sink = sink_ref[idx]
            m = jnp.maximum(jnp.max(sc, axis=-1, keepdims=True), sink)
            pr = jnp.exp((sc - m).astype(BF16))
            zero = jnp.zeros_like(pr)
            probs.append(jnp.concatenate([jnp.where(own, zero, pr), jnp.where(own, pr, zero)],
                                         axis=1))
            sink_terms.append(jnp.exp(sink - m))
        o_all = _dot(jnp.concatenate(probs, axis=0),
                     jnp.concatenate([vv, jnp.ones_like(vv)], axis=1))

        def head_out(idx):
            blk = o_all[idx * BLOCK:(idx + 1) * BLOCK]
            return blk[:, 0:LANES] / (blk[:, LANES:] + sink_terms[idx])

        for p in range(n_pairs):
            mixbuf[rows, p * LANES:(p + 1) * LANES] = jnp.where(
                low_lanes, head_out(p), head_out(n_pairs + p)).astype(BF16)

    n_lg = CONV_CH // LANES
    skew = _conv_skew(tile)
    lead = CONV_HALO - (CONV_WIDTH - 1)
    for lg in range(n_lg):
        lanes = slice(lg * LANES, (lg + 1) * LANES)
        cbuf[lg, 0:CONV_HALO, :] = jnp.where(i > 0, ch_ref[:, lanes], 0.0)
        cbuf[lg, CONV_HALO:CONV_HALO + tile, :] = c_ref[:, lanes]
        cbuf[lg, CONV_HALO + tile:, :] = jnp.zeros((cbuf.shape[1] - CONV_HALO - tile, LANES), F32)
    skewed = lambda ref, lg, start: ref.at[lg, pl.ds(start, SUBLANES, stride=skew), :]
    one_trip = jnp.minimum(i, 0) + 1

    def conv_lane_group(lg):
        lanes = slice(lg * LANES, (lg + 1) * LANES)
        for t0 in range(0, CONV_WIDTH, CONV_TAPS):
            n_taps = min(CONV_TAPS, CONV_WIDTH - t0)
            w = [jnp.broadcast_to(convw_ref[t0 + tt:t0 + tt + 1, lanes], (SUBLANES, LANES))
                 for tt in range(n_taps)]
            acc = {}
            for o in range(skew + n_taps - 1):
                xs = skewed(cbuf, lg, o + lead + t0)[...]
                for tt in range(n_taps):
                    j = o - tt
                    if not 0 <= j < skew:
                        continue
                    if tt == 0:
                        acc[j] = (jnp.broadcast_to(convb_ref[:, lanes], (SUBLANES, LANES))
                                  if t0 == 0 else skewed(cout, lg, j)[...])
                    acc[j] = acc[j] + xs * w[tt]
                    if tt == n_taps - 1:
                        skewed(cout, lg, j)[...] = acc.pop(j)

    for lg in range(n_lg):
        lax.fori_loop(0, one_trip, lambda _, carry, lg=lg: (conv_lane_group(lg), carry)[1], 0)
    for j in range(skew):
        y = _layer_norm(jnp.concatenate([skewed(cout, lg, j)[...] for lg in range(n_lg)], axis=1),
                        lng_ref[...], lnb_ref[...])
        y = y * jax.nn.sigmoid(y)
        for lg in range(n_lg):
            skewed(cout, lg, j)[...] = y[:, lg * LANES:(lg + 1) * LANES]
    for lg in range(n_lg):
        mixbuf[:, ATT_W + lg * LANES:ATT_W + (lg + 1) * LANES] = cout[lg, 0:tile, :].astype(BF16)

    o_ref[...] = x_ref[...] + _dot(mixbuf[...], wout_ref[...])


def _ev_tail_kernel(sink_ref, x_ref, q_ref, k_ref, kh_ref, v_ref, vh_ref, c_ref, ch_ref,
                    convw_ref, convb_ref, lng_ref, lnb_ref, wout_ref,
                    fng_ref, wg_ref, wu_ref, wd_ref, fg_ref, o_ref,
                    kbuf, vbuf, cbuf, cout, mixbuf, actbuf, h_mid, woutbuf):
    @pl.when((pl.program_id(0) == 0) & (pl.program_id(1) == 0))
    def _():
        n_pairs = ATT_W // LANES
        for p in range(n_pairs):
            for half, head in enumerate((p, n_pairs + p)):
                dst = p * LANES + half * HEAD_DIM
                woutbuf[dst:dst + HEAD_DIM, :] = wout_ref[
                    head * HEAD_DIM:(head + 1) * HEAD_DIM, :].astype(BF16)
        woutbuf[ATT_W:, :] = wout_ref[ATT_W:, :].astype(BF16)

    _ev_mix_kernel(sink_ref, x_ref, q_ref, k_ref, kh_ref, v_ref, vh_ref, c_ref, ch_ref,
                   convw_ref, convb_ref, lng_ref, lnb_ref, woutbuf, h_mid,
                   kbuf, vbuf, cbuf, cout, mixbuf)
    _ffn_kernel(h_mid, fng_ref, wg_ref, wu_ref, wd_ref, fg_ref, o_ref, actbuf, final_norm=False)


def _ev_tail(x2, q, k, v, c, sinks, convw, convb, lng, lnb, wout, ffn, layer, batch, seq):
    tile = TOK_TILE
    nt = seq // tile
    tok = lambda w: pl.BlockSpec((tile, w), lambda b, i: (b * nt + i, 0))
    halo = lambda rows, w: pl.BlockSpec(
        (rows, w), lambda b, i: (jnp.maximum((b * nt + i) * (tile // rows) - 1, 0), 0))
    consts = [convw, convb, lng, lnb, wout]
    skew_rows = SUBLANES * _conv_skew(tile)
    return pl.pallas_call(
        _ev_tail_kernel,
        grid=(batch, nt),
        in_specs=[pl.BlockSpec(memory_space=pltpu.SMEM),
                  tok(D_MODEL), tok(ATT_W), tok(LANES), halo(BLOCK, LANES),
                  tok(LANES), halo(BLOCK, LANES), tok(CONV_CH), halo(CONV_HALO, CONV_CH)]
                 + [_const_spec(t.shape) for t in consts] + _ffn_specs(ffn, layer),
        out_specs=tok(D_MODEL),
        out_shape=jax.ShapeDtypeStruct(x2.shape, F32),
        scratch_shapes=[pltpu.VMEM((tile + BLOCK, LANES), BF16),
                        pltpu.VMEM((tile + BLOCK, LANES), BF16),
                        pltpu.VMEM((CONV_CH // LANES, CONV_HALO + skew_rows, LANES), F32),
                        pltpu.VMEM((CONV_CH // LANES, skew_rows, LANES), F32),
                        pltpu.VMEM((tile, ATT_W + CONV_CH), BF16),
                        pltpu.VMEM((tile, D_FF), BF16),
                        pltpu.VMEM((tile, D_MODEL), F32),
                        pltpu.VMEM(wout.shape, BF16)],
        compiler_params=_cparams(2),
        name="ev_tail",
    )(sinks, x2, q, k, k, v, v, c, c, *consts, *ffn)


def _branch_kernel(q_ref, k_ref, kh_ref, v_ref, vh_ref, o_ref, lse_ref, kbuf, vbuf):
    i = pl.program_id(2)
    n_res, tile = q_ref.shape[0], q_ref.shape[1]

    band, prev_keys = _band_masks(1)
    first_penalty = jnp.where(i == 0, NEG, 0.0).astype(F32)
    low_lanes = lax.broadcasted_iota(jnp.int32, (BLOCK, LANES), 1) < HEAD_DIM

    for r in range(n_res):
        kbuf[r, 0:BLOCK, :] = kh_ref[r]
        kbuf[r, BLOCK:, :] = k_ref[r]
        vbuf[r, 0:BLOCK, :] = vh_ref[r]
        vbuf[r, BLOCK:, :] = v_ref[r]
        for jb in range(tile // BLOCK):
            rows = slice(jb * BLOCK, (jb + 1) * BLOCK)
            for p in range(ATT_W // LANES):
                lanes = slice(p * LANES, (p + 1) * LANES)
                kk = kbuf[r, jb * BLOCK:(jb + 2) * BLOCK, lanes]
                vv = vbuf[r, jb * BLOCK:(jb + 2) * BLOCK, lanes]
                qp = q_ref[r, rows, lanes]
                zero = jnp.zeros_like(qp)
                q2 = jnp.concatenate([jnp.where(low_lanes, qp, zero),
                                      jnp.where(low_lanes, zero, qp)], axis=0)
                s2 = _dot_nt(q2, kk)
                probs, ms = [], []
                for half in range(2):
                    pr, m = _softmax_rows(s2[half * BLOCK:(half + 1) * BLOCK], band,
                                          prev_keys, first_penalty if jb == 0 else None)
                    probs.append(pr)
                    ms.append(m)
                o2 = _dot(jnp.concatenate(probs, axis=0),
                          jnp.concatenate([vv, jnp.ones_like(vv)], axis=1))
                l0, l1 = o2[0:BLOCK, LANES:], o2[BLOCK:, LANES:]
                o_ref[r, rows, lanes] = jnp.where(low_lanes, o2[0:BLOCK, 0:LANES] / l0,
                                                  o2[BLOCK:, 0:LANES] / l1)
                lse_ref[r, rows, lanes] = jnp.where(low_lanes, ms[0] + jnp.log(l0),
                                                    ms[1] + jnp.log(l1))


def _branch(qkv, batch, seq, dil):
    sub = seq // dil
    tile = min(BRANCH_ROWS, sub)
    n_res = BRANCH_ROWS // tile
    assert dil % n_res == 0
    nt = sub // tile

    def cur(s):
        return pl.BlockSpec((None, None, n_res, tile, ATT_W), lambda b, r, i: (b, s, r, i, 0))

    def halo(s):
        return pl.BlockSpec(
            (None, None, n_res, BLOCK, ATT_W),
            lambda b, r, i: (b, s, r, jnp.maximum(i * (tile // BLOCK) - 1, 0), 0))

    out_spec = pl.BlockSpec((None, n_res, tile, ATT_W), lambda b, r, i: (b, r, i, 0))
    out_sds = jax.ShapeDtypeStruct((batch, dil, sub, ATT_W), F32)
    return pl.pallas_call(
        _branch_kernel,
        grid=(batch, dil // n_res, nt),
        in_specs=[cur(0), cur(1), halo(1), cur(2), halo(2)],
        out_specs=[out_spec, out_spec],
        out_shape=[out_sds, out_sds],
        scratch_shapes=[pltpu.VMEM((n_res, tile + BLOCK, ATT_W), BF16),
                        pltpu.VMEM((n_res, tile + BLOCK, ATT_W), BF16)],
        compiler_params=_cparams(3),
        name=f"od_branch_d{dil}",
    )(qkv, qkv, qkv, qkv, qkv)


def _od_mix_kernel(x_ref, o1_ref, o2_ref, o3_ref, l1_ref, l2_ref, l3_ref, u_ref, gate_ref,
                   sw_ref, sb_ref, wout_ref, out_ref, mixbuf, tok_o2, tok_l2, tok_o3, tok_l3):
    tile = x_ref.shape[0]
    for j in range(ATT_W // LANES):
        sl = slice(j * LANES, (j + 1) * LANES)
        for src, dst in ((o2_ref, tok_o2), (l2_ref, tok_l2), (o3_ref, tok_o3), (l3_ref, tok_l3)):
            dil = src.shape[0]
            for r in range(dil):
                dst[j, pl.ds(r, tile // dil, stride=dil), :] = src[r, :, sl]
        l1, l2, l3 = l1_ref[0, :, sl], tok_l2[j], tok_l3[j]
        m = jnp.maximum(jnp.maximum(l1, l2), l3)
        e1, e2, e3 = jnp.exp(l1 - m), jnp.exp(l2 - m), jnp.exp(l3 - m)
        mixed = (e1 * o1_ref[0, :, sl] + e2 * tok_o2[j] + e3 * tok_o3[j]) / (e1 + e2 + e3)
        mixbuf[:, sl] = mixed.astype(BF16)

    ti = lax.broadcasted_iota(jnp.int32, (CHUNK, CHUNK), 0)
    si = lax.broadcasted_iota(jnp.int32, (CHUNK, CHUNK), 1)
    causal = ti >= si
    low_lanes = lax.broadcasted_iota(jnp.int32, (CHUNK, LANES), 1) < HEAD_DIM
    for p in range(D_CH // LANES):
        lanes = slice(p * LANES, (p + 1) * LANES)
        w2 = jnp.concatenate(
            [jnp.where(causal, sw_ref[2 * p], 0.0), jnp.where(causal, sw_ref[2 * p + 1], 0.0)],
            axis=0).astype(BF16)
        bias = sb_ref[:, lanes]
        for ch in range(tile // CHUNK):
            rows = slice(ch * CHUNK, (ch + 1) * CHUNK)
            y2 = _dot(w2, gate_ref[rows, lanes])
            y = jnp.where(low_lanes, y2[0:CHUNK], y2[CHUNK:]) + bias
            mixbuf[rows, ATT_W + p * LANES:ATT_W + (p + 1) * LANES] = (
                u_ref[rows, lanes] * y).astype(BF16)

    out_ref[...] = x_ref[...] + _dot(mixbuf[...], wout_ref[...])


def _od_tail_kernel(x_ref, o1_ref, o2_ref, o3_ref, l1_ref, l2_ref, l3_ref, u_ref, gate_ref,
                    sw_ref, sb_ref, wout_ref, fng_ref, wg_ref, wu_ref, wd_ref, fg_ref, out_ref,
                    mixbuf, tok_o2, tok_l2, tok_o3, tok_l3, actbuf, h_mid):
    _od_mix_kernel(x_ref, o1_ref, o2_ref, o3_ref, l1_ref, l2_ref, l3_ref, u_ref, gate_ref,
                   sw_ref, sb_ref, wout_ref, h_mid, mixbuf, tok_o2, tok_l2, tok_o3, tok_l3)
    _ffn_kernel(h_mid, fng_ref, wg_ref, wu_ref, wd_ref, fg_ref, out_ref, actbuf, final_norm=True)


def _od_tail(x2, outs, lses, u, gate, sw, sb, wout, ffn, layer, seq):
    tile = TOK_TILE
    per_seq = seq // tile
    tok = lambda w: pl.BlockSpec((tile, w), lambda i: (i, 0))
    folded = [pl.BlockSpec((None, d, tile // d, ATT_W),
                           lambda i: (i // per_seq, 0, i % per_seq, 0)) for d in DILATIONS]
    consts = [sw, sb, wout]
    return pl.pallas_call(
        _od_tail_kernel,
        grid=(x2.shape[0] // tile,),
        in_specs=[tok(D_MODEL)] + folded + folded + [tok(D_CH), tok(D_CH)]
                 + [_const_spec(t.shape) for t in consts] + _ffn_specs(ffn, layer),
        out_specs=tok(D_MODEL),
        out_shape=jax.ShapeDtypeStruct(x2.shape, F32),
        scratch_shapes=[pltpu.VMEM((tile, ATT_W + D_CH), BF16)]
                       + [pltpu.VMEM((ATT_W // LANES, tile, LANES), F32)] * 4
                       + [pltpu.VMEM((tile, D_FF), BF16), pltpu.VMEM((tile, D_MODEL), F32)],
        compiler_params=_cparams(1),
        name="od_tail",
    )(x2, *outs, *lses, u, gate, *consts, *ffn)


def _ffn_kernel(x_ref, g_ref, wg_ref, wu_ref, wd_ref, fg_ref, o_ref, actbuf, *, final_norm):
    hn = _rms(x_ref[...], g_ref[...]).astype(BF16)
    for c in range(D_FF // FF_CHUNK):
        cols = slice(c * FF_CHUNK, (c + 1) * FF_CHUNK)
        gate = _dot(hn, wg_ref[:, cols])
        up = _dot(hn, wu_ref[:, cols])
        actbuf[:, cols] = (gate * jax.nn.sigmoid(gate) * up).astype(BF16)
    y = x_ref[...] + _dot(actbuf[...], wd_ref[...])
    if final_norm:
        y = _rms(y, fg_ref[...])
    o_ref[...] = y


def kernel(x, ev_norm_g, ev_w_in, ev_sinks, ev_conv_w, ev_conv_b, ev_conv_ln_g, ev_conv_ln_b, ev_w_out, od_norm_g, od_w_in, od_sgu_ln_g, od_sgu_ln_b, od_spatial_w, od_spatial_b, od_w_out, ffn_norm_g, ffn_w_gate, ffn_w_up, ffn_w_down, final_norm_g):
    batch, seq, _ = x.shape
    assert seq % (max(DILATIONS) * BLOCK) == 0 and seq % FRONT_TILE == 0 and seq % TOK_TILE == 0
    row = lambda t: t.reshape(1, -1).astype(F32)
    x2 = x.reshape(batch * seq, D_MODEL)
    tabs = _rope_tables(seq)

    n_pairs = ATT_W // LANES
    (q, k, v, c), (w_gate, w_up, w_down, w_in1, w_out1) = _ev_front(
        x2, row(ev_norm_g[0]), ev_w_in[0], tabs,
        (ffn_w_gate, ffn_w_up, ffn_w_down, od_w_in, od_w_out), seq)
    ffn_w = lambda l: (row(ffn_norm_g[l]), w_gate, w_up, w_down, row(final_norm_g))
    h = _ev_tail(x2, q, k, v, c, ev_sinks[0].astype(F32), ev_conv_w[0].astype(F32),
                 row(ev_conv_b[0]), row(ev_conv_ln_g[0]), row(ev_conv_ln_b[0]), ev_w_out[0],
                 ffn_w(0), 0, batch, seq)

    *qkvs, u, gate = _od_front(h, row(od_norm_g[0]), w_in1[0], tabs,
                               row(od_sgu_ln_g[0]), row(od_sgu_ln_b[0]), batch, seq)
    outs, lses = [], []
    for dil, qkv in zip(DILATIONS, qkvs):
        o_r, lse_r = _branch(qkv, batch, seq, dil)
        outs.append(o_r)
        lses.append(lse_r)
    sb = jnp.repeat(od_spatial_b[0].astype(F32).T, HEAD_DIM, axis=1)
    h = _od_tail(h, outs, lses, u, gate, od_spatial_w[0].astype(F32), sb,
                 w_out1[0], ffn_w(1), 1, seq)
    return h.reshape(batch, seq, D_MODEL)
```
